```python
import jax, jax.numpy as jnp
from jax import lax
import numpy as np

D_MODEL = 1024
BATCH = 4
SEQ = 4096
DEPTH = 1
DEC_BATCH = 128
DEC_SEQ = 4
PAST_LEN = 8192
PAGE_SIZE = 128

N_HEADS = 8
HEAD_DIM = 64
D_ATT = N_HEADS * HEAD_DIM
D_LRU = D_MODEL // 2
LRU_BLOCKS = 8
LRU_BLOCK_DIM = D_LRU // LRU_BLOCKS
D_MIX = D_ATT + D_LRU
D_IN = 4 * D_ATT + 2 * D_LRU
CONV_WIDTH = 4
LRU_C = 8.0
DIL_PATTERNS = ((128, 1), (512, 4), (2048, 16))
N_DIL_KEYS = 128
ATT_WINDOW_MAX = 2048
BAND_BLOCK = 128
ALPHA = (2.0 * DEPTH) ** 0.25
BETA = (8.0 * DEPTH) ** -0.25
LN_EPS = 1e-5
NEG_INF = -1e30

kernel_name = "hymba_dilated_attn_rglru_deepnorm_step"


def _project(x, w_in):
    h = jnp.einsum('btd,de->bte', x, w_in)
    cuts = [D_ATT, 2 * D_ATT, 3 * D_ATT, 4 * D_ATT, 4 * D_ATT + D_LRU]
    return jnp.split(h, cuts, axis=-1)


def _band_attention(q, k, v):
    n, l, h, dh = q.shape
    nb = -(-l // BAND_BLOCK)
    pad = nb * BAND_BLOCK - l
    padw = ((0, 0), (0, pad), (0, 0), (0, 0))
    qb = jnp.pad(q, padw).reshape(n, nb, BAND_BLOCK, h, dh)
    kb = jnp.pad(k, padw).reshape(n, nb, BAND_BLOCK, h, dh)
    vb = jnp.pad(v, padw).reshape(n, nb, BAND_BLOCK, h, dh)
    prev = ((0, 0), (1, 0), (0, 0), (0, 0), (0, 0))
    kc = jnp.concatenate([jnp.pad(kb, prev)[:, :-1], kb], axis=2)
    vc = jnp.concatenate([jnp.pad(vb, prev)[:, :-1], vb], axis=2)
    s = jnp.einsum('nbqhd,nbkhd->nbhqk', qb, kc) * (dh ** -0.5)
    qi = jnp.arange(BAND_BLOCK)[:, None]
    ki = jnp.arange(2 * BAND_BLOCK)[None, :]
    dist = qi + BAND_BLOCK - ki
    key_pos = jnp.arange(nb)[:, None, None] * BAND_BLOCK + ki[None] - BAND_BLOCK
    valid = (dist >= 0) & (dist <= N_DIL_KEYS) & (key_pos >= 0)
    s = jnp.where(valid[None, :, None], s, NEG_INF)
    m = s.max(-1)
    p = jnp.exp(s - m[..., None])
    den = p.sum(-1)
    o = jnp.einsum('nbhqk,nbkhd->nbqhd', p, vc) / den.transpose(0, 1, 3, 2)[..., None]
    o = o.reshape(n, nb * BAND_BLOCK, h, dh)[:, :l]
    m = m.transpose(0, 1, 3, 2).reshape(n, nb * BAND_BLOCK, h)[:, :l]
    den = den.transpose(0, 1, 3, 2).reshape(n, nb * BAND_BLOCK, h)[:, :l]
    return o, m, den


def _dilated_prompt(q, k, v, d):
    b, s, h, dh = q.shape
    l = s // d

    def sub(t):
        return t.reshape(b, l, d, h, dh).transpose(0, 2, 1, 3, 4).reshape(b * d, l, h, dh)

    o, m, den = _band_attention(sub(q), sub(k), sub(v))
    o = o.reshape(b, d, l, h, dh).transpose(0, 2, 1, 3, 4).reshape(b, s, h, dh)
    m = m.reshape(b, d, l, h).transpose(0, 2, 1, 3).reshape(b, s, h)
    den = den.reshape(b, d, l, h).transpose(0, 2, 1, 3).reshape(b, s, h)
    return o, m, den


def _dilated_sample(q, k_all, v_all, d):
    t = q.shape[1]
    w_buf = k_all.shape[1] - t
    idx = w_buf + jnp.arange(t)[:, None] - d * jnp.arange(N_DIL_KEYS + 1)[None, :]
    valid = idx >= 0
    idx = jnp.maximum(idx, 0)
    kg = jnp.take(k_all, idx, axis=1)
    vg = jnp.take(v_all, idx, axis=1)
    s = jnp.einsum('bqhd,bqjhd->bhqj', q, kg) * (q.shape[-1] ** -0.5)
    s = jnp.where(valid[None, None], s, NEG_INF)
    m = s.max(-1)
    p = jnp.exp(s - m[..., None])
    den = p.sum(-1)
    o = jnp.einsum('bhqj,bqjhd->bqhd', p, vg) / den.transpose(0, 2, 1)[..., None]
    return o, m.transpose(0, 2, 1), den.transpose(0, 2, 1)


def _merge(parts):
    m_max = jnp.max(jnp.stack([p[1] for p in parts]), axis=0)
    ws = [den * jnp.exp(m - m_max) for (_, m, den) in parts]
    total = sum(ws)
    return sum(w[..., None] * o for w, (o, _, _) in zip(ws, parts)) / total[..., None]


def _lin_combine(e1, e2):
    a1, b1 = e1
    a2, b2 = e2
    return a1 * a2, a2 * b1 + b2


def _lru_branch(xl, conv_state, h0, conv_w, conv_b, w_ra, b_ra, w_ri, b_ri, lru_lambda):
    b, t, _ = xl.shape
    xc = jnp.concatenate([conv_state.astype(xl.dtype), xl], axis=1)
    u = conv_b + sum(xc[:, w:w + t] * conv_w[w] for w in range(CONV_WIDTH))
    new_conv = xc[:, t:]
    u32 = u.astype(jnp.float32)
    ub = u32.reshape(b, t, LRU_BLOCKS, LRU_BLOCK_DIM)
    r = jax.nn.sigmoid(jnp.einsum('bthi,hij->bthj', ub, w_ra.astype(jnp.float32)).reshape(b, t, D_LRU)
                       + b_ra.astype(jnp.float32))
    ig = jax.nn.sigmoid(jnp.einsum('bthi,hij->bthj', ub, w_ri.astype(jnp.float32)).reshape(b, t, D_LRU)
                        + b_ri.astype(jnp.float32))
    log_a = -LRU_C * r * jax.nn.softplus(-lru_lambda.astype(jnp.float32))
    a = jnp.exp(log_a)
    bx = jnp.sqrt(-jnp.expm1(2.0 * log_a)) * ig * u32
    bx = bx.at[:, 0].add(a[:, 0] * h0.astype(jnp.float32))
    _, h = lax.associative_scan(_lin_combine, (a, bx), axis=1)
    return h, new_conv, h[:, -1]


def _finish(x, y_att, g_att, h_lru, g_lru, w_out, ln_g, ln_b):
    mixed = jnp.concatenate([y_att * jax.nn.silu(g_att.astype(jnp.float32)),
                             h_lru * jax.nn.silu(g_lru.astype(jnp.float32))], axis=-1)
    sub = jnp.einsum('bte,ed->btd', mixed, w_out.astype(jnp.float32))
    z = ALPHA * x.astype(jnp.float32) + sub
    mu = z.mean(-1, keepdims=True)
    var = jnp.square(z - mu).mean(-1, keepdims=True)
    zn = (z - mu) * lax.rsqrt(var + LN_EPS)
    return (zn * ln_g.astype(jnp.float32) + ln_b.astype(jnp.float32)).astype(x.dtype)


def _layer(x_p, x_s, ck, cv, sc, sh, w_in, conv_w, conv_b, w_ra, b_ra, w_ri, b_ri, lru_lambda,
           w_out, ln_g, ln_b):
    f32 = jnp.float32
    b, s, _ = x_p.shape
    q, k, v, ga, xl, gl = _project(x_p, w_in)
    qh, kh, vh = (t_.reshape(b, s, N_HEADS, HEAD_DIM) for t_ in (q, k, v))
    y_att = _merge([_dilated_prompt(qh.astype(f32), kh.astype(f32), vh.astype(f32), d)
                    for _, d in DIL_PATTERNS]).reshape(b, s, D_ATT)
    h_lru, conv_p, h_p = _lru_branch(xl, jnp.zeros((b, CONV_WIDTH - 1, D_LRU), xl.dtype),
                                     jnp.zeros((b, D_LRU), f32), conv_w, conv_b, w_ra, b_ra,
                                     w_ri, b_ri, lru_lambda)
    y_p = _finish(x_p, y_att, ga, h_lru, gl, w_out, ln_g, ln_b)
    n_keep = min(ATT_WINDOW_MAX, s)
    k_p, v_p = kh[:, s - n_keep:], vh[:, s - n_keep:]
    bs, t, _ = x_s.shape
    q2, k2, v2, ga2, xl2, gl2 = _project(x_s, w_in)
    qh2, kh2, vh2 = (t_.reshape(bs, t, N_HEADS, HEAD_DIM) for t_ in (q2, k2, v2))
    k_all = jnp.concatenate([ck.astype(f32), kh2.astype(f32)], axis=1)
    v_all = jnp.concatenate([cv.astype(f32), vh2.astype(f32)], axis=1)
    y_att2 = _merge([_dilated_sample(qh2.astype(f32), k_all, v_all, d)
                     for _, d in DIL_PATTERNS]).reshape(bs, t, D_ATT)
    h_lru2, conv_s, h_s = _lru_branch(xl2, sc, sh, conv_w, conv_b, w_ra, b_ra, w_ri, b_ri, lru_lambda)
    y_s = _finish(x_s, y_att2, ga2, h_lru2, gl2, w_out, ln_g, ln_b)
    return y_p, y_s, k_p, v_p, conv_p, h_p, kh2, vh2, conv_s, h_s


def setup_inputs(seed: int = 0) -> dict:
    key = jax.random.key(seed)
    ks = jax.random.split(key, 17)
    w_buf = min(ATT_WINDOW_MAX, PAST_LEN)
    nrm = jax.random.normal
    col_scale = jnp.concatenate([jnp.ones((2 * D_ATT,)), jnp.full((D_ATT,), BETA), jnp.ones((D_ATT,)),
                                 jnp.full((D_LRU,), BETA), jnp.ones((D_LRU,))]).astype(jnp.float32)
    u = jax.random.uniform(ks[12], (DEPTH, D_LRU), minval=0.9, maxval=0.999)
    a_base = u ** (1.0 / LRU_C)
    return {
        "x_prompt": nrm(ks[0], (BATCH, SEQ, D_MODEL), jnp.float32),
        "x_sample": nrm(ks[1], (DEC_BATCH, DEC_SEQ, D_MODEL), jnp.float32),
        "cache_k": nrm(ks[2], (DEPTH, DEC_BATCH, w_buf, N_HEADS, HEAD_DIM), jnp.float32),
        "cache_v": nrm(ks[3], (DEPTH, DEC_BATCH, w_buf, N_HEADS, HEAD_DIM), jnp.float32) * BETA,
        "state_conv": nrm(ks[4], (DEPTH, DEC_BATCH, CONV_WIDTH - 1, D_LRU), jnp.float32) * BETA,
        "state_h": nrm(ks[5], (DEPTH, DEC_BATCH, D_LRU), jnp.float32) * 0.5,
        "w_in": nrm(ks[6], (DEPTH, D_MODEL, D_IN), jnp.float32) * (D_MODEL ** -0.5) * col_scale,
        "conv_w": nrm(ks[7], (DEPTH, CONV_WIDTH, D_LRU), jnp.float32) * (CONV_WIDTH ** -0.5),
        "conv_b": nrm(ks[8], (DEPTH, D_LRU), jnp.float32) * 0.01,
        "w_ra": nrm(ks[9], (DEPTH, LRU_BLOCKS, LRU_BLOCK_DIM, LRU_BLOCK_DIM), jnp.float32) * (LRU_BLOCK_DIM ** -0.5),
        "b_ra": nrm(ks[10], (DEPTH, D_LRU), jnp.float32) * 0.01,
        "w_ri": nrm(ks[11], (DEPTH, LRU_BLOCKS, LRU_BLOCK_DIM, LRU_BLOCK_DIM), jnp.float32) * (LRU_BLOCK_DIM ** -0.5),
        "b_ri": nrm(ks[13], (DEPTH, D_LRU), jnp.float32) * 0.01,
        "lru_lambda": jnp.log(a_base) - jnp.log1p(-a_base),
        "w_out": nrm(ks[14], (DEPTH, D_MIX, D_MODEL), jnp.float32) * (D_MIX ** -0.5) * BETA,
        "ln_g": 1.0 + 0.02 * nrm(ks[15], (DEPTH, D_MODEL), jnp.float32),
        "ln_b": 0.02 * nrm(ks[16], (DEPTH, D_MODEL), jnp.float32),
    }


def reference(x_prompt, x_sample, cache_k, cache_v, state_conv, state_h, w_in, conv_w, conv_b,
              w_ra, b_ra, w_ri, b_ri, lru_lambda, w_out, ln_g, ln_b):
    xp, xs = x_prompt, x_sample
    kp_l, vp_l, cp_l, hp_l, ks_l, vs_l, cs_l, hs_l = [], [], [], [], [], [], [], []
    for l in range(DEPTH):
        xp, xs, kp, vp, cp, hp, ks_, vs_, cs, hs = _layer(
            xp, xs, cache_k[l], cache_v[l], state_conv[l], state_h[l], w_in[l], conv_w[l], conv_b[l],
            w_ra[l], b_ra[l], w_ri[l], b_ri[l], lru_lambda[l], w_out[l], ln_g[l], ln_b[l])
        kp_l.append(kp); vp_l.append(vp); cp_l.append(cp); hp_l.append(hp)
        ks_l.append(ks_); vs_l.append(vs_); cs_l.append(cs); hs_l.append(hs)
    return (xp, xs, jnp.stack(kp_l), jnp.stack(vp_l), jnp.stack(cp_l), jnp.stack(hp_l),
            jnp.stack(ks_l), jnp.stack(vs_l), jnp.stack(cs_l), jnp.stack(hs_l))
```

```python
import functools

import numpy as np
import jax
import jax.numpy as jnp
from jax import lax
from jax.experimental import pallas as pl
from jax.experimental.pallas import tpu as pltpu

F32 = jnp.float32
BF16 = jnp.bfloat16

D_MODEL = 1024
N_HEADS = 8
HEAD_DIM = 64
D_ATT = N_HEADS * HEAD_DIM
D_LRU = 512
LRU_BLOCKS = 8
D_IN = 4 * D_ATT + 2 * D_LRU
CONV_WIDTH = 4
LRU_C = 8.0
DILATIONS = (1, 4, 16)
N_DIL_KEYS = 128
BAND = 128
ALPHA = 2.0 ** 0.25
LN_EPS = 1e-5
NEG_INF = -1e30

C_Q, C_K, C_V, C_GA, C_XL, C_GL = (0, D_ATT, 2 * D_ATT, 3 * D_ATT, 4 * D_ATT,
                                   4 * D_ATT + D_LRU)

VMEM_LIMIT = 52 * 1024 * 1024
PROJ_ROWS = 512
HEAD_PAIR = 2 * HEAD_DIM


def _dot(a, b):
    return jnp.dot(a, b, preferred_element_type=F32)


def _dot_nt(a, b):
    return lax.dot_general(a, b, (((1,), (1,)), ((), ())),
                           preferred_element_type=F32)


def _silu(x):
    return x * jax.nn.sigmoid(x)


def _softplus(x):
    return jnp.maximum(x, 0.0) + jnp.log1p(jnp.exp(-jnp.abs(x)))


def _lru_gates(u, wg_ref, vec_ref):
    g = _dot(u.astype(BF16), wg_ref[...])
    r = jax.nn.sigmoid(g[:, :D_LRU] + vec_ref[5:6, :])
    ig = jax.nn.sigmoid(g[:, D_LRU:] + vec_ref[6:7, :])
    log_a = (-LRU_C * _softplus(-vec_ref[7:8, :])) * r
    a = jnp.exp(log_a)
    bx = jnp.sqrt(jnp.tanh(-log_a) * (1.0 + a * a)) * ig * u
    return a, bx


def _proj_lru_kernel(x_ref, w_ref, wg_ref, vec_ref,
                     q_ref, k_ref, v_ref, hl_ref, sga_ref, conv_ref, hlast_ref,
                     xc_ref, a_ref, b_ref, carry_ref):
    j = pl.program_id(1)
    rows = x_ref.shape[0]

    @pl.when(j == 0)
    def _():
        xc_ref[0:8, :] = jnp.zeros((8, D_LRU), F32)
        carry_ref[...] = jnp.zeros((1, D_LRU), F32)

    xb = x_ref[...].astype(BF16)
    q_ref[...] = _dot(xb, w_ref[:, C_Q:C_K])
    k_ref[...] = _dot(xb, w_ref[:, C_K:C_V])
    v_ref[...] = _dot(xb, w_ref[:, C_V:C_GA])
    sga_ref[...] = _silu(_dot(xb, w_ref[:, C_GA:C_XL])).astype(BF16)

    xl = _dot(xb, w_ref[:, C_XL:C_GL])
    xc_ref[8:8 + rows, :] = xl
    u = (vec_ref[4:5, :]
         + xc_ref[5:5 + rows, :] * vec_ref[0:1, :]
         + xc_ref[6:6 + rows, :] * vec_ref[1:2, :]
         + xc_ref[7:7 + rows, :] * vec_ref[2:3, :]
         + xl * vec_ref[3:4, :])
    xc_ref[0:8, :] = xc_ref[rows:rows + 8, :]

    a, bx = _lru_gates(u, wg_ref, vec_ref)
    a_ref[...] = a
    b_ref[...] = bx

    row = lax.broadcasted_iota(jnp.int32, (8, D_LRU), 0)

    def group(g, carry):
        r0 = pl.multiple_of(g * 8, 8)
        av = a_ref[pl.ds(r0, 8), :]
        bv = b_ref[pl.ds(r0, 8), :]
        for s in (1, 2, 4):
            keep = row >= s
            bv = jnp.where(keep, av * pltpu.roll(bv, s, axis=0) + bv, bv)
            av = jnp.where(keep, av * pltpu.roll(av, s, axis=0), av)
        h8 = av * carry + bv
        b_ref[pl.ds(r0, 8), :] = h8
        return h8[7:8, :]

    carry = lax.fori_loop(0, rows // 8, group, carry_ref[...], unroll=4)
    carry_ref[...] = carry

    gl = _dot(xb, w_ref[:, C_GL:D_IN])
    hl_ref[...] = (b_ref[...] * _silu(gl)).astype(BF16)

    @pl.when(j == pl.num_programs(1) - 1)
    def _():
        conv_ref[...] = xc_ref[5:8, :]
        hlast_ref[...] = carry


def _proj_lru(x, w_bf, w_gate, vecs):
    b, s, _ = x.shape
    nt = s // PROJ_ROWS
    row_spec = lambda width: pl.BlockSpec((None, PROJ_ROWS, width), lambda i, j: (i, j, 0))
    const = lambda shape: pl.BlockSpec(shape, lambda i, j: (0,) * len(shape),
                                       pipeline_mode=pl.Buffered(1))
    return pl.pallas_call(
        _proj_lru_kernel,
        grid=(b, nt),
        in_specs=[row_spec(D_MODEL), const((D_MODEL, D_IN)),
                  const((D_LRU, 2 * D_LRU)), const((8, D_LRU))],
        out_specs=[row_spec(D_ATT), row_spec(D_ATT), row_spec(D_ATT),
                   row_spec(D_LRU), row_spec(D_ATT),
                   pl.BlockSpec((None, CONV_WIDTH - 1, D_LRU), lambda i, j: (i, 0, 0)),
                   pl.BlockSpec((None, 1, D_LRU), lambda i, j: (i, 0, 0))],
        out_shape=[jax.ShapeDtypeStruct((b, s, D_ATT), F32),
                   jax.ShapeDtypeStruct((b, s, D_ATT), F32),
                   jax.ShapeDtypeStruct((b, s, D_ATT), F32),
                   jax.ShapeDtypeStruct((b, s, D_LRU), BF16),
                   jax.ShapeDtypeStruct((b, s, D_ATT), BF16),
                   jax.ShapeDtypeStruct((b, CONV_WIDTH - 1, D_LRU), F32),
                   jax.ShapeDtypeStruct((b, 1, D_LRU), F32)],
        scratch_shapes=[pltpu.VMEM((PROJ_ROWS + 8, D_LRU), F32),
                        pltpu.VMEM((PROJ_ROWS, D_LRU), F32),
                        pltpu.VMEM((PROJ_ROWS, D_LRU), F32),
                        pltpu.VMEM((1, D_LRU), F32)],
        compiler_params=pltpu.CompilerParams(
            dimension_semantics=("arbitrary", "arbitrary"),
            vmem_limit_bytes=VMEM_LIMIT),
        name="prompt_proj_lru",
    )(x, w_bf, w_gate, vecs)


def _prompt_attn_kernel(q_ref, k_ref, v_ref, sga_ref, o_ref, *stats):
    seq = q_ref.shape[0]
    accs, ms, ls = stats[0:3], stats[3:6], stats[6:9]

    row = lax.broadcasted_iota(jnp.int32, (BAND, 2 * BAND), 0)
    col = lax.broadcasted_iota(jnp.int32, (BAND, 2 * BAND), 1)
    band = (col >= row) & (col <= row + N_DIL_KEYS)
    head_a = lax.broadcasted_iota(jnp.int32, (BAND, HEAD_PAIR), 1) < HEAD_DIM

    for pi, d in enumerate(DILATIONS):
        span = BAND * d
        acc_ref, m_ref, l_ref = accs[pi], ms[pi], ls[pi]

        def rows_at(start, d=d):
            if d == 1:
                return pl.ds(start, BAND)
            return pl.ds(start, BAND, stride=d)

        def group(g, carry, d=d, span=span, rows_at=rows_at,
                  acc_ref=acc_ref, m_ref=m_ref, l_ref=l_ref):
            r = g % d
            blk = g // d
            cur = blk * span + r
            prev = jnp.maximum(cur - span, r)
            valid = band & (col >= jnp.where(blk == 0, BAND, 0))

            qf = q_ref[rows_at(cur), :]
            k2 = jnp.concatenate([k_ref[rows_at(prev), :], k_ref[rows_at(cur), :]],
                                 axis=0).astype(BF16)
            v2 = jnp.concatenate([v_ref[rows_at(prev), :], v_ref[rows_at(cur), :]],
                                 axis=0).astype(BF16)

            def head(qh):
                s = jnp.where(valid, _dot_nt(qh.astype(BF16), k2), NEG_INF)
                m = jnp.max(s, axis=-1, keepdims=True)
                p = jnp.exp(s - m)
                l = jnp.sum(p, axis=-1, keepdims=True)
                return _dot(p.astype(BF16), v2), m, l

            oa, ma, la = head(jnp.where(head_a, qf, 0.0))
            ob, mb, lb = head(jnp.where(head_a, 0.0, qf))
            acc_ref[rows_at(cur), :] = jnp.where(head_a, oa, ob)
            m_ref[rows_at(cur), :] = jnp.where(head_a, ma, mb)
            l_ref[rows_at(cur), :] = jnp.where(head_a, la, lb)
            return carry

        lax.fori_loop(0, seq // BAND, group, 0)

    chunk = 256

    def merge(c, carry):
        rows = pl.ds(pl.multiple_of(c * chunk, chunk), chunk)
        m = [r[rows, :] for r in ms]
        m_max = jnp.maximum(jnp.maximum(m[0], m[1]), m[2])
        e = [jnp.exp(mi - m_max) for mi in m]
        num = e[0] * accs[0][rows, :] + e[1] * accs[1][rows, :] + e[2] * accs[2][rows, :]
        den = e[0] * ls[0][rows, :] + e[1] * ls[1][rows, :] + e[2] * ls[2][rows, :]
        o_ref[rows, :] = (num / den * sga_ref[rows, :].astype(F32)).astype(BF16)
        return carry

    lax.fori_loop(0, seq // chunk, merge, 0)


def _prompt_attn(q, k, v, sga):
    b, s, _ = q.shape
    spec = pl.BlockSpec((None, s, HEAD_PAIR), lambda i, j: (i, 0, j))
    return pl.pallas_call(
        _prompt_attn_kernel,
        grid=(b, D_ATT // HEAD_PAIR),
        in_specs=[spec, spec, spec, spec],
        out_specs=spec,
        out_shape=jax.ShapeDtypeStruct((b, s, D_ATT), BF16),
        scratch_shapes=[pltpu.VMEM((s, HEAD_PAIR), F32) for _ in range(9)],
        compiler_params=pltpu.CompilerParams(
            dimension_semantics=("arbitrary", "arbitrary"),
            vmem_limit_bytes=VMEM_LIMIT),
        name="prompt_attn",
    )(q, k, v, sga)


def _finish_kernel(ya_ref, hl_ref, x_ref, wo_ref, ln_ref, o_ref):
    sub = (_dot(ya_ref[...], wo_ref[0:D_ATT, :])
           + _dot(hl_ref[...], wo_ref[D_ATT:D_ATT + D_LRU, :]))
    z = ALPHA * x_ref[...] + sub
    mu = jnp.mean(z, axis=-1, keepdims=True)
    zc = z - mu
    var = jnp.mean(zc * zc, axis=-1, keepdims=True)
    o_ref[...] = zc * lax.rsqrt(var + LN_EPS) * ln_ref[0:1, :] + ln_ref[1:2, :]


def _finish(ya, hl, x, w_o, ln, name):
    n = x.shape[0]
    row_spec = lambda width: pl.BlockSpec((PROJ_ROWS, width), lambda i: (i, 0))
    const = lambda shape: pl.BlockSpec(shape, lambda i: (0, 0),
                                       pipeline_mode=pl.Buffered(1))
    return pl.pallas_call(
        _finish_kernel,
        grid=(n // PROJ_ROWS,),
        in_specs=[row_spec(D_ATT), row_spec(D_LRU), row_spec(D_MODEL),
                  const((D_ATT + D_LRU, D_MODEL)), const((2, D_MODEL))],
        out_specs=row_spec(D_MODEL),
        out_shape=jax.ShapeDtypeStruct((n, D_MODEL), F32),
        compiler_params=pltpu.CompilerParams(
            dimension_semantics=("arbitrary",), vmem_limit_bytes=VMEM_LIMIT),
        name=name,
    )(ya, hl, x, w_o, ln)


def _sample_proj_kernel(x_ref, sc_ref, sh_ref, w_ref, wg_ref, vec_ref,
                        q_ref, k_ref, v_ref, hl_ref, sga_ref, conv_ref, hlast_ref):
    t_steps, nb, _ = x_ref.shape
    xb = x_ref[...].reshape(t_steps * nb, D_MODEL).astype(BF16)
    split = lambda y: y.reshape(t_steps, nb, y.shape[-1])
    q_ref[...] = split(_dot(xb, w_ref[:, C_Q:C_K]))
    k_ref[...] = split(_dot(xb, w_ref[:, C_K:C_V]))
    v_ref[...] = split(_dot(xb, w_ref[:, C_V:C_GA]))
    sga_ref[...] = split(_silu(_dot(xb, w_ref[:, C_GA:C_XL])))

    xl = _dot(xb, w_ref[:, C_XL:C_GL])
    xc = [sc_ref[i] for i in range(CONV_WIDTH - 1)]
    xc += [xl[t * nb:(t + 1) * nb, :] for t in range(t_steps)]
    u = jnp.concatenate(
        [vec_ref[4:5, :] + sum(xc[t + w] * vec_ref[w:w + 1, :] for w in range(CONV_WIDTH))
         for t in range(t_steps)], axis=0)
    for i in range(CONV_WIDTH - 1):
        conv_ref[i] = xc[t_steps + i]

    a, bx = _lru_gates(u, wg_ref, vec_ref)
    sgl = _silu(_dot(xb, w_ref[:, C_GL:D_IN]))
    h = sh_ref[...]
    for t in range(t_steps):
        rows = slice(t * nb, (t + 1) * nb)
        h = a[rows, :] * h + bx[rows, :]
        hl_ref[t] = (h * sgl[rows, :]).astype(BF16)
    hlast_ref[...] = h


def _sample_proj(xs_t, sc_t, sh, w_bf, w_gate, vecs):
    t_steps, nb, _ = xs_t.shape
    full = lambda shape: pl.BlockSpec(shape, lambda i: (0,) * len(shape))
    tok = lambda dt: jax.ShapeDtypeStruct((t_steps, nb, D_ATT), dt)
    return pl.pallas_call(
        _sample_proj_kernel,
        grid=(1,),
        in_specs=[full(xs_t.shape), full(sc_t.shape), full(sh.shape),
                  full(w_bf.shape), full(w_gate.shape), full(vecs.shape)],
        out_specs=[full((t_steps, nb, D_ATT))] * 5
                  + [full(sc_t.shape), full(sh.shape)],
        out_shape=[tok(F32), tok(F32), tok(F32), tok(BF16), tok(F32),
                   jax.ShapeDtypeStruct(sc_t.shape, F32),
                   jax.ShapeDtypeStruct(sh.shape, F32)],
        compiler_params=pltpu.CompilerParams(
            dimension_semantics=("arbitrary",), vmem_limit_bytes=VMEM_LIMIT),
        name="sample_proj_lru",
    )(xs_t, sc_t, sh, w_bf, w_gate, vecs)


def _sample_attn_kernel(q_ref, kn_ref, vn_ref, sga_ref, cnt_ref, kt_ref, vt_ref, o_ref):
    t_steps = q_ref.shape[0]
    n_rows = t_steps * N_HEADS
    sub = lax.broadcasted_iota(jnp.int32, (N_HEADS, D_ATT), 0)
    lane_head = lax.broadcasted_iota(jnp.int32, (N_HEADS, D_ATT), 1) // HEAD_DIM
    own = sub == lane_head
    qbd = jnp.concatenate(
        [jnp.where(own, jnp.broadcast_to(q_ref[t], (N_HEADS, D_ATT)), 0.0)
         for t in range(t_steps)], axis=0)

    cnt = cnt_ref[...]
    s = jnp.where(cnt > 0.0, _dot(qbd.astype(BF16), kt_ref[...].astype(BF16)), NEG_INF)
    m = jnp.max(s, axis=-1, keepdims=True)

    row_t = lax.broadcasted_iota(jnp.int32, (n_rows, 1), 0) // N_HEADS
    s_new, c_new = [], []
    for t2 in range(t_steps):
        c = jnp.where(row_t == t2, float(len(DILATIONS)),
                      jnp.where(row_t > t2, 1.0, 0.0))
        sn = jnp.sum(qbd * kn_ref[t2], axis=-1, keepdims=True)
        sn = jnp.where(c > 0.0, sn, NEG_INF)
        m = jnp.maximum(m, sn)
        s_new.append(sn)
        c_new.append(c)

    p = jnp.exp(s - m) * cnt
    den = jnp.sum(p, axis=-1, keepdims=True)
    acc = _dot_nt(p.astype(BF16), vt_ref[...].astype(BF16))
    for t2 in range(t_steps):
        pn = jnp.exp(s_new[t2] - m) * c_new[t2]
        den = den + pn
        acc = acc + pn * vn_ref[t2]
    y = acc / den
    for t in range(t_steps):
        yt = jnp.sum(jnp.where(own, y[t * N_HEADS:(t + 1) * N_HEADS, :], 0.0),
                     axis=0, keepdims=True)
        o_ref[t] = yt * sga_ref[t]


def _pattern_counts(t_steps, w_buf):
    delta = w_buf + np.arange(t_steps)[:, None] - np.arange(w_buf)[None, :]
    cnt = np.zeros((t_steps, w_buf), np.float32)
    for d in DILATIONS:
        cnt += (delta % d == 0) & (delta // d <= N_DIL_KEYS)
    return np.repeat(cnt, N_HEADS, axis=0)


def _sample_attn(q4, kn4, vn4, sga4, kt, vt):
    t_steps, nb = q4.shape[:2]
    w_buf = kt.shape[-1]
    cnt = jnp.asarray(_pattern_counts(t_steps, w_buf))
    tok = pl.BlockSpec((t_steps, None, 1, D_ATT), lambda i: (0, i, 0, 0))
    cache = pl.BlockSpec((None, D_ATT, w_buf), lambda i: (i, 0, 0))
    return pl.pallas_call(
        _sample_attn_kernel,
        grid=(nb,),
        in_specs=[tok, tok, tok, tok,
                  pl.BlockSpec(cnt.shape, lambda i: (0, 0)), cache, cache],
        out_specs=tok,
        out_shape=jax.ShapeDtypeStruct((t_steps, nb, 1, D_ATT), F32),
        compiler_params=pltpu.CompilerParams(
            dimension_semantics=("arbitrary",), vmem_limit_bytes=VMEM_LIMIT),
        name="sample_attn",
    )(q4, kn4, vn4, sga4, cnt, kt, vt)


def kernel(x_prompt, x_sample, cache_k, cache_v, state_conv, state_h, w_in, conv_w, conv_b, w_ra, b_ra, w_ri, b_ri, lru_lambda, w_out, ln_g, ln_b):
    assert w_in.shape[0] == 1, "single layer"
    b, s, _ = x_prompt.shape
    nb, t_steps, _ = x_sample.shape
    w_buf = cache_k.shape[2]

    col_scale = jnp.concatenate([jnp.full((D_ATT,), HEAD_DIM ** -0.5, F32),
                                 jnp.ones((D_IN - D_ATT,), F32)])
    w_bf = (w_in[0] * col_scale).astype(BF16)
    eye = jnp.eye(LRU_BLOCKS, dtype=F32)
    block_diag = lambda wb: (eye[:, None, :, None] * wb[:, :, None, :]).reshape(D_LRU, D_LRU)
    w_gate = jnp.concatenate([block_diag(w_ra[0]), block_diag(w_ri[0])], axis=1).astype(BF16)
    vecs = jnp.concatenate([conv_w[0], conv_b, b_ra, b_ri, lru_lambda], axis=0)
    w_o = w_out[0].astype(BF16)
    ln = jnp.concatenate([ln_g, ln_b], axis=0)

    q, k, v, hl, sga, conv_p, h_p = _proj_lru(x_prompt, w_bf, w_gate, vecs)
    ya = _prompt_attn(q, k, v, sga)
    y_p = _finish(ya.reshape(b * s, D_ATT), hl.reshape(b * s, D_LRU),
                  x_prompt.reshape(b * s, D_MODEL), w_o, ln, "prompt_finish")
    n_keep = min(w_buf, s)
    heads = lambda a: a.reshape(a.shape[:-1] + (N_HEADS, HEAD_DIM))
    k_p = heads(k[:, s - n_keep:])[None]
    v_p = heads(v[:, s - n_keep:])[None]

    xs_t = jnp.transpose(x_sample, (1, 0, 2))
    sc_t = jnp.transpose(state_conv[0], (1, 0, 2))
    q_s, k_s, v_s, hl_s, sga_s, conv_s, h_s = _sample_proj(
        xs_t, sc_t, state_h[0], w_bf, w_gate, vecs)
    four_d = lambda a: a.reshape(t_steps, nb, 1, D_ATT)
    kt = jnp.transpose(cache_k[0], (0, 2, 3, 1)).reshape(nb, D_ATT, w_buf)
    vt = jnp.transpose(cache_v[0], (0, 2, 3, 1)).reshape(nb, D_ATT, w_buf)
    ya_s = _sample_attn(four_d(q_s), four_d(k_s), four_d(v_s), four_d(sga_s), kt, vt)
    y_s = _finish(ya_s.reshape(t_steps * nb, D_ATT).astype(BF16),
                  hl_s.reshape(t_steps * nb, D_LRU),
                  xs_t.reshape(t_steps * nb, D_MODEL), w_o, ln, "sample_finish")
    to_batch_major = lambda a: jnp.transpose(a, (1, 0, 2))

    return (y_p.reshape(b, s, D_MODEL),
            to_batch_major(y_s.reshape(t_steps, nb, D_MODEL)),
            k_p, v_p,
            conv_p[None], h_p.reshape(1, b, D_LRU),
            heads(to_batch_major(k_s))[None], heads(to_batch_major(v_s))[None],
            to_batch_major(conv_s)[None], h_s[None])
```

```python
import functools

import numpy as np
import jax
import jax.numpy as jnp
from jax import lax
from jax.experimental import pallas as pl
from jax.experimental.pallas import tpu as pltpu

F32 = jnp.float32
BF16 = jnp.bfloat16

D_MODEL = 1024
N_HEADS = 8
HEAD_DIM = 64
D_ATT = N_HEADS * HEAD_DIM
D_LRU = 512
LRU_BLOCKS = 8
D_IN = 4 * D_ATT + 2 * D_LRU
CONV_WIDTH = 4
LRU_C = 8.0
DILATIONS = (1, 4, 16)
N_DIL_KEYS = 128
BAND = 128
ALPHA = 2.0 ** 0.25
LN_EPS = 1e-5
NEG_INF = -1e30

C_Q, C_K, C_V, C_GA, C_XL, C_GL = (0, D_ATT, 2 * D_ATT, 3 * D_ATT, 4 * D_ATT,
                                   4 * D_ATT + D_LRU)

VMEM_LIMIT = 52 * 1024 * 1024
PROJ_ROWS = 512
HEAD_PAIR = 2 * HEAD_DIM
ATTN_UNROLL = 8


def _dot(a, b):
    return jnp.dot(a, b, preferred_element_type=F32)


def _dot_nt(a, b):
    return lax.dot_general(a, b, (((1,), (1,)), ((), ())),
                           preferred_element_type=F32)


def _silu(x):
    return x * jax.nn.sigmoid(x)


def _softplus(x):
    return jnp.maximum(x, 0.0) + jnp.log1p(jnp.exp(-jnp.abs(x)))


def _lru_gates(u, wg_ref, vec_ref):
    g = _dot(u.astype(BF16), wg_ref[...])
    r = jax.nn.sigmoid(g[:, :D_LRU] + vec_ref[5:6, :])
    ig = jax.nn.sigmoid(g[:, D_LRU:] + vec_ref[6:7, :])
    log_a = (-LRU_C * _softplus(-vec_ref[7:8, :])) * r
    a = jnp.exp(log_a)
    bx = jnp.sqrt(jnp.tanh(-log_a) * (1.0 + a * a)) * ig * u
    return a, bx


def _proj_lru_kernel(x_ref, w_ref, wg_ref, vec_ref,
                     q_ref, k_ref, v_ref, hl_ref, sga_ref, conv_ref, hlast_ref,
                     xc_ref, a_ref, b_ref, carry_ref):
    j = pl.program_id(1)
    rows = x_ref.shape[0]

    @pl.when(j == 0)
    def _():
        xc_ref[0:8, :] = jnp.zeros((8, D_LRU), F32)
        carry_ref[...] = jnp.zeros((1, D_LRU), F32)

    xb = x_ref[...].astype(BF16)
    q_ref[...] = _dot(xb, w_ref[:, C_Q:C_K])
    k_ref[...] = _dot(xb, w_ref[:, C_K:C_V])
    v_ref[...] = _dot(xb, w_ref[:, C_V:C_GA])
    sga_ref[...] = _silu(_dot(xb, w_ref[:, C_GA:C_XL])).astype(BF16)

    xl = _dot(xb, w_ref[:, C_XL:C_GL])
    xc_ref[8:8 + rows, :] = xl
    u = (vec_ref[4:5, :]
         + xc_ref[5:5 + rows, :] * vec_ref[0:1, :]
         + xc_ref[6:6 + rows, :] * vec_ref[1:2, :]
         + xc_ref[7:7 + rows, :] * vec_ref[2:3, :]
         + xl * vec_ref[3:4, :])
    xc_ref[0:8, :] = xc_ref[rows:rows + 8, :]

    a, bx = _lru_gates(u, wg_ref, vec_ref)
    a_ref[...] = a
    b_ref[...] = bx

    row = lax.broadcasted_iota(jnp.int32, (8, D_LRU), 0)

    def group(g, carry):
        r0 = pl.multiple_of(g * 8, 8)
        av = a_ref[pl.ds(r0, 8), :]
        bv = b_ref[pl.ds(r0, 8), :]
        for s in (1, 2, 4):
            keep = row >= s
            bv = jnp.where(keep, av * pltpu.roll(bv, s, axis=0) + bv, bv)
            av = jnp.where(keep, av * pltpu.roll(av, s, axis=0), av)
        h8 = av * carry + bv
        b_ref[pl.ds(r0, 8), :] = h8
        return h8[7:8, :]

    carry = lax.fori_loop(0, rows // 8, group, carry_ref[...], unroll=4)
    carry_ref[...] = carry

    gl = _dot(xb, w_ref[:, C_GL:D_IN])
    hl_ref[...] = (b_ref[...] * _silu(gl)).astype(BF16)

    @pl.when(j == pl.num_programs(1) - 1)
    def _():
        conv_ref[...] = xc_ref[5:8, :]
        hlast_ref[...] = carry


def _proj_lru(x, w_bf, w_gate, vecs):
    b, s, _ = x.shape
    nt = s // PROJ_ROWS
    row_spec = lambda width: pl.BlockSpec((None, PROJ_ROWS, width), lambda i, j: (i, j, 0))
    const = lambda shape: pl.BlockSpec(shape, lambda i, j: (0,) * len(shape),
                                       pipeline_mode=pl.Buffered(1))
    return pl.pallas_call(
        _proj_lru_kernel,
        grid=(b, nt),
        in_specs=[row_spec(D_MODEL), const((D_MODEL, D_IN)),
                  const((D_LRU, 2 * D_LRU)), const((8, D_LRU))],
        out_specs=[row_spec(D_ATT), row_spec(D_ATT), row_spec(D_ATT),
                   row_spec(D_LRU), row_spec(D_ATT),
                   pl.BlockSpec((None, CONV_WIDTH - 1, D_LRU), lambda i, j: (i, 0, 0)),
                   pl.BlockSpec((None, 1, D_LRU), lambda i, j: (i, 0, 0))],
        out_shape=[jax.ShapeDtypeStruct((b, s, D_ATT), F32),
                   jax.ShapeDtypeStruct((b, s, D_ATT), F32),
                   jax.ShapeDtypeStruct((b, s, D_ATT), F32),
                   jax.ShapeDtypeStruct((b, s, D_LRU), BF16),
                   jax.ShapeDtypeStruct((b, s, D_ATT), BF16),
                   jax.ShapeDtypeStruct((b, CONV_WIDTH - 1, D_LRU), F32),
                   jax.ShapeDtypeStruct((b, 1, D_LRU), F32)],
        scratch_shapes=[pltpu.VMEM((PROJ_ROWS + 8, D_LRU), F32),
                        pltpu.VMEM((PROJ_ROWS, D_LRU), F32),
                        pltpu.VMEM((PROJ_ROWS, D_LRU), F32),
                        pltpu.VMEM((1, D_LRU), F32)],
        compiler_params=pltpu.CompilerParams(
            dimension_semantics=("arbitrary", "arbitrary"),
            vmem_limit_bytes=VMEM_LIMIT),
        name="prompt_proj_lru",
    )(x, w_bf, w_gate, vecs)


def _prompt_attn_kernel(q_ref, k_ref, v_ref, sga_ref, o_ref, bias_ref, *stats):
    seq = q_ref.shape[0]
    accs, ms, ls = stats[0:3], stats[3:6], stats[6:9]

    row = lax.broadcasted_iota(jnp.int32, (BAND, 2 * BAND), 0)
    col = lax.broadcasted_iota(jnp.int32, (BAND, 2 * BAND), 1)
    band = (col >= row) & (col <= row + N_DIL_KEYS)
    bias_ref[0] = jnp.where(band & (col >= BAND), 0.0, NEG_INF)
    bias_ref[1] = jnp.where(band, 0.0, NEG_INF)
    head_a = lax.broadcasted_iota(jnp.int32, (BAND, HEAD_PAIR), 1) < HEAD_DIM
    one = jnp.ones((), BF16)

    for pi, d in enumerate(DILATIONS):
        span = BAND * d
        n_blk = seq // span
        acc_ref, m_ref, l_ref = accs[pi], ms[pi], ls[pi]

        def group(g, carry, d=d, span=span, n_blk=n_blk,
                  acc_ref=acc_ref, m_ref=m_ref, l_ref=l_ref):
            k_prev, v_prev = carry
            blk = g % n_blk
            start = blk * span + g // n_blk
            rows = pl.ds(start, BAND) if d == 1 else pl.ds(start, BAND, stride=d)

            qf = q_ref[rows, :]
            k_cur = k_ref[rows, :].astype(BF16)
            v_bf = v_ref[rows, :].astype(BF16)
            v_cur = jnp.concatenate([jnp.where(head_a, v_bf, one),
                                     jnp.where(head_a, one, v_bf)], axis=1)
            k2 = jnp.concatenate([k_prev, k_cur], axis=0)
            v4 = jnp.concatenate([v_prev, v_cur], axis=0)
            q2 = jnp.concatenate([jnp.where(head_a, qf, 0.0),
                                  jnp.where(head_a, 0.0, qf)], axis=0).astype(BF16)

            s = _dot_nt(q2, k2).reshape(2, BAND, 2 * BAND)
            s = s + bias_ref[jnp.minimum(blk, 1)][None]
            m = jnp.max(s, axis=-1, keepdims=True)
            p = jnp.exp(s - m).reshape(2 * BAND, 2 * BAND).astype(BF16)
            o = _dot(p, v4)

            acc_ref[rows, :] = jnp.where(head_a, o[:BAND, :HEAD_PAIR], o[BAND:, HEAD_PAIR:])
            l_ref[rows, :] = jnp.where(head_a, o[:BAND, HEAD_PAIR:], o[BAND:, :HEAD_PAIR])
            m_ref[rows, :] = jnp.where(head_a, m[0], m[1])
            return k_cur, v_cur

        init = (jnp.zeros((BAND, HEAD_PAIR), BF16), jnp.zeros((BAND, 2 * HEAD_PAIR), BF16))
        lax.fori_loop(0, seq // BAND, group, init, unroll=ATTN_UNROLL)

    chunk = 256

    def merge(c, carry):
        rows = pl.ds(pl.multiple_of(c * chunk, chunk), chunk)
        m = [r[rows, :] for r in ms]
        m_max = jnp.maximum(jnp.maximum(m[0], m[1]), m[2])
        e = [jnp.exp(mi - m_max) for mi in m]
        num = e[0] * accs[0][rows, :] + e[1] * accs[1][rows, :] + e[2] * accs[2][rows, :]
        den = e[0] * ls[0][rows, :] + e[1] * ls[1][rows, :] + e[2] * ls[2][rows, :]
        o_ref[rows, :] = (num / den * sga_ref[rows, :].astype(F32)).astype(BF16)
        return carry

    lax.fori_loop(0, seq // chunk, merge, 0)


def _prompt_attn(q, k, v, sga):
    b, s, _ = q.shape
    spec = pl.BlockSpec((None, s, HEAD_PAIR), lambda i, j: (i, 0, j))
    return pl.pallas_call(
        _prompt_attn_kernel,
        grid=(b, D_ATT // HEAD_PAIR),
        in_specs=[spec, spec, spec, spec],
        out_specs=spec,
        out_shape=jax.ShapeDtypeStruct((b, s, D_ATT), BF16),
        scratch_shapes=[pltpu.VMEM((2, BAND, 2 * BAND), F32)]
                       + [pltpu.VMEM((s, HEAD_PAIR), F32) for _ in range(9)],
        compiler_params=pltpu.CompilerParams(
            dimension_semantics=("arbitrary", "arbitrary"),
            vmem_limit_bytes=VMEM_LIMIT),
        name="prompt_attn",
    )(q, k, v, sga)


def _finish_kernel(ya_ref, hl_ref, x_ref, wo_ref, ln_ref, o_ref):
    sub = (_dot(ya_ref[...], wo_ref[0:D_ATT, :])
           + _dot(hl_ref[...], wo_ref[D_ATT:D_ATT + D_LRU, :]))
    z = ALPHA * x_ref[...] + sub
    mu = jnp.mean(z, axis=-1, keepdims=True)
    zc = z - mu
    var = jnp.mean(zc * zc, axis=-1, keepdims=True)
    o_ref[...] = zc * lax.rsqrt(var + LN_EPS) * ln_ref[0:1, :] + ln_ref[1:2, :]


def _finish(ya, hl, x, w_o, ln, name):
    n = x.shape[0]
    row_spec = lambda width: pl.BlockSpec((PROJ_ROWS, width), lambda i: (i, 0))
    const = lambda shape: pl.BlockSpec(shape, lambda i: (0, 0),
                                       pipeline_mode=pl.Buffered(1))
    return pl.pallas_call(
        _finish_kernel,
        grid=(n // PROJ_ROWS,),
        in_specs=[row_spec(D_ATT), row_spec(D_LRU), row_spec(D_MODEL),
                  const((D_ATT + D_LRU, D_MODEL)), const((2, D_MODEL))],
        out_specs=row_spec(D_MODEL),
        out_shape=jax.ShapeDtypeStruct((n, D_MODEL), F32),
        compiler_params=pltpu.CompilerParams(
            dimension_semantics=("arbitrary",), vmem_limit_bytes=VMEM_LIMIT),
        name=name,
    )(ya, hl, x, w_o, ln)


def _sample_proj_kernel(x_ref, sc_ref, sh_ref, w_ref, wg_ref, vec_ref,
                        q_ref, k_ref, v_ref, hl_ref, sga_ref, conv_ref, hlast_ref):
    t_steps, nb, _ = x_ref.shape
    xb = x_ref[...].reshape(t_steps * nb, D_MODEL).astype(BF16)
    split = lambda y: y.reshape(t_steps, nb, y.shape[-1])
    q_ref[...] = split(_dot(xb, w_ref[:, C_Q:C_K]))
    k_ref[...] = split(_dot(xb, w_ref[:, C_K:C_V]))
    v_ref[...] = split(_dot(xb, w_ref[:, C_V:C_GA]))
    sga_ref[...] = split(_silu(_dot(xb, w_ref[:, C_GA:C_XL])))

    xl = _dot(xb, w_ref[:, C_XL:C_GL])
    xc = [sc_ref[i] for i in range(CONV_WIDTH - 1)]
    xc += [xl[t * nb:(t + 1) * nb, :] for t in range(t_steps)]
    u = jnp.concatenate(
        [vec_ref[4:5, :] + sum(xc[t + w] * vec_ref[w:w + 1, :] for w in range(CONV_WIDTH))
         for t in range(t_steps)], axis=0)
    for i in range(CONV_WIDTH - 1):
        conv_ref[i] = xc[t_steps + i]

    a, bx = _lru_gates(u, wg_ref, vec_ref)
    sgl = _silu(_dot(xb, w_ref[:, C_GL:D_IN]))
    h = sh_ref[...]
    for t in range(t_steps):
        rows = slice(t * nb, (t + 1) * nb)
        h = a[rows, :] * h + bx[rows, :]
        hl_ref[t] = (h * sgl[rows, :]).astype(BF16)
    hlast_ref[...] = h


def _sample_proj(xs_t, sc_t, sh, w_bf, w_gate, vecs):
    t_steps, nb, _ = xs_t.shape
    full = lambda shape: pl.BlockSpec(shape, lambda i: (0,) * len(shape))
    tok = lambda dt: jax.ShapeDtypeStruct((t_steps, nb, D_ATT), dt)
    return pl.pallas_call(
        _sample_proj_kernel,
        grid=(1,),
        in_specs=[full(xs_t.shape), full(sc_t.shape), full(sh.shape),
                  full(w_bf.shape), full(w_gate.shape), full(vecs.shape)],
        out_specs=[full((t_steps, nb, D_ATT))] * 5
                  + [full(sc_t.shape), full(sh.shape)],
        out_shape=[tok(F32), tok(F32), tok(F32), tok(BF16), tok(F32),
                   jax.ShapeDtypeStruct(sc_t.shape, F32),
                   jax.ShapeDtypeStruct(sh.shape, F32)],
        compiler_params=pltpu.CompilerParams(
            dimension_semantics=("arbitrary",), vmem_limit_bytes=VMEM_LIMIT),
        name="sample_proj_lru",
    )(xs_t, sc_t, sh, w_bf, w_gate, vecs)


def _sample_attn_kernel(q_ref, kn_ref, vn_ref, sga_ref, cnt_ref, kt_ref, vt_ref, o_ref):
    t_steps = q_ref.shape[0]
    n_rows = t_steps * N_HEADS
    sub = lax.broadcasted_iota(jnp.int32, (N_HEADS, D_ATT), 0)
    lane_head = lax.broadcasted_iota(jnp.int32, (N_HEADS, D_ATT), 1) // HEAD_DIM
    own = sub == lane_head
    qbd = jnp.concatenate(
        [jnp.where(own, jnp.broadcast_to(q_ref[t], (N_HEADS, D_ATT)), 0.0)
         for t in range(t_steps)], axis=0)

    cnt = cnt_ref[...]
    s = jnp.where(cnt > 0.0, _dot(qbd.astype(BF16), kt_ref[...].astype(BF16)), NEG_INF)
    m = jnp.max(s, axis=-1, keepdims=True)

    row_t = lax.broadcasted_iota(jnp.int32, (n_rows, 1), 0) // N_HEADS
    s_new, c_new = [], []
    for t2 in range(t_steps):
        c = jnp.where(row_t == t2, float(len(DILATIONS)),
                      jnp.where(row_t > t2, 1.0, 0.0))
        sn = jnp.sum(qbd * kn_ref[t2], axis=-1, keepdims=True)
        sn = jnp.where(c > 0.0, sn, NEG_INF)
        m = jnp.maximum(m, sn)
        s_new.append(sn)
        c_new.append(c)

    p = jnp.exp(s - m) * cnt
    den = jnp.sum(p, axis=-1, keepdims=True)
    acc = _dot_nt(p.astype(BF16), vt_ref[...].astype(BF16))
    for t2 in range(t_steps):
        pn = jnp.exp(s_new[t2] - m) * c_new[t2]
        den = den + pn
        acc = acc + pn * vn_ref[t2]
    y = acc / den
    for t in range(t_steps):
        yt = jnp.sum(jnp.where(own, y[t * N_HEADS:(t + 1) * N_HEADS, :], 0.0),
                     axis=0, keepdims=True)
        o_ref[t] = yt * sga_ref[t]


def _pattern_counts(t_steps, w_buf):
    delta = w_buf + np.arange(t_steps)[:, None] - np.arange(w_buf)[None, :]
    cnt = np.zeros((t_steps, w_buf), np.float32)
    for d in DILATIONS:
        cnt += (delta % d == 0) & (delta // d <= N_DIL_KEYS)
    return np.repeat(cnt, N_HEADS, axis=0)


def _sample_attn(q4, kn4, vn4, sga4, kt, vt):
    t_steps, nb = q4.shape[:2]
    w_buf = kt.shape[-1]
    cnt = jnp.asarray(_pattern_counts(t_steps, w_buf))
    tok = pl.BlockSpec((t_steps, None, 1, D_ATT), lambda i: (0, i, 0, 0))
    cache = pl.BlockSpec((None, D_ATT, w_buf), lambda i: (i, 0, 0))
    return pl.pallas_call(
        _sample_attn_kernel,
        grid=(nb,),
        in_specs=[tok, tok, tok, tok,
                  pl.BlockSpec(cnt.shape, lambda i: (0, 0)), cache, cache],
        out_specs=tok,
        out_shape=jax.ShapeDtypeStruct((t_steps, nb, 1, D_ATT), F32),
        compiler_params=pltpu.CompilerParams(
            dimension_semantics=("arbitrary",), vmem_limit_bytes=VMEM_LIMIT),
        name="sample_attn",
    )(q4, kn4, vn4, sga4, cnt, kt, vt)


def kernel(x_prompt, x_sample, cache_k, cache_v, state_conv, state_h, w_in, conv_w, conv_b, w_ra, b_ra, w_ri, b_ri, lru_lambda, w_out, ln_g, ln_b):
    assert w_in.shape[0] == 1, "single layer"
    b, s, _ = x_prompt.shape
    nb, t_steps, _ = x_sample.shape
    w_buf = cache_k.shape[2]

    col_scale = jnp.concatenate([jnp.full((D_ATT,), HEAD_DIM ** -0.5, F32),
                                 jnp.ones((D_IN - D_ATT,), F32)])
    w_bf = (w_in[0] * col_scale).astype(BF16)
    eye = jnp.eye(LRU_BLOCKS, dtype=F32)
    block_diag = lambda wb: (eye[:, None, :, None] * wb[:, :, None, :]).reshape(D_LRU, D_LRU)
    w_gate = jnp.concatenate([block_diag(w_ra[0]), block_diag(w_ri[0])], axis=1).astype(BF16)
    vecs = jnp.concatenate([conv_w[0], conv_b, b_ra, b_ri, lru_lambda], axis=0)
    w_o = w_out[0].astype(BF16)
    ln = jnp.concatenate([ln_g, ln_b], axis=0)

    q, k, v, hl, sga, conv_p, h_p = _proj_lru(x_prompt, w_bf, w_gate, vecs)
    ya = _prompt_attn(q, k, v, sga)
    y_p = _finish(ya.reshape(b * s, D_ATT), hl.reshape(b * s, D_LRU),
                  x_prompt.reshape(b * s, D_MODEL), w_o, ln, "prompt_finish")
    n_keep = min(w_buf, s)
    heads = lambda a: a.reshape(a.shape[:-1] + (N_HEADS, HEAD_DIM))
    k_p = heads(k[:, s - n_keep:])[None]
    v_p = heads(v[:, s - n_keep:])[None]

    xs_t = jnp.transpose(x_sample, (1, 0, 2))
    sc_t = jnp.transpose(state_conv[0], (1, 0, 2))
    q_s, k_s, v_s, hl_s, sga_s, conv_s, h_s = _sample_proj(
        xs_t, sc_t, state_h[0], w_bf, w_gate, vecs)
    four_d = lambda a: a.reshape(t_steps, nb, 1, D_ATT)
    kt = jnp.transpose(cache_k[0], (0, 2, 3, 1)).reshape(nb, D_ATT, w_buf)
    vt = jnp.transpose(cache_v[0], (0, 2, 3, 1)).reshape(nb, D_ATT, w_buf)
    ya_s = _sample_attn(four_d(q_s), four_d(k_s), four_d(v_s), four_d(sga_s), kt, vt)
    y_s = _finish(ya_s.reshape(t_steps * nb, D_ATT).astype(BF16),
                  hl_s.reshape(t_steps * nb, D_LRU),
                  xs_t.reshape(t_steps * nb, D_MODEL), w_o, ln, "sample_finish")
    to_batch_major = lambda a: jnp.transpose(a, (1, 0, 2))

    return (y_p.reshape(b, s, D_MODEL),
            to_batch_major(y_s.reshape(t_steps, nb, D_MODEL)),
            k_p, v_p,
            conv_p[None], h_p.reshape(1, b, D_LRU),
            heads(to_batch_major(k_s))[None], heads(to_batch_major(v_s))[None],
            to_batch_major(conv_s)[None], h_s[None])
```

```python
import functools

import numpy as np
import jax
import jax.numpy as jnp
from jax import lax
from jax.experimental import pallas as pl
from jax.experimental.pallas import tpu as pltpu

F32 = jnp.float32
BF16 = jnp.bfloat16

D_MODEL = 1024
N_HEADS = 8
HEAD_DIM = 64
D_ATT = N_HEADS * HEAD_DIM
D_LRU = 512
LRU_BLOCKS = 8
D_IN = 4 * D_ATT + 2 * D_LRU
CONV_WIDTH = 4
LRU_C = 8.0
DILATIONS = (1, 4, 16)
N_DIL_KEYS = 128
BAND = 128
ALPHA = 2.0 ** 0.25
LN_EPS = 1e-5
NEG_INF = -1e30

C_Q, C_K, C_V, C_GA, C_XL, C_GL = (0, D_ATT, 2 * D_ATT, 3 * D_ATT, 4 * D_ATT,
                                   4 * D_ATT + D_LRU)

VMEM_LIMIT = 52 * 1024 * 1024
PROJ_ROWS = 512
PROJ_PHASES = 4
HEAD_PAIR = 2 * HEAD_DIM
ATTN_UNROLL = 8


def _dot(a, b):
    return jnp.dot(a, b, preferred_element_type=F32)


def _dot_nt(a, b):
    return lax.dot_general(a, b, (((1,), (1,)), ((), ())),
                           preferred_element_type=F32)


def _silu(x):
    return x * jax.nn.sigmoid(x)


def _softplus(x):
    return jnp.maximum(x, 0.0) + jnp.log1p(jnp.exp(-jnp.abs(x)))


def _lru_gates(u, wg_ref, vec_ref):
    g = _dot(u.astype(BF16), wg_ref[...])
    r = jax.nn.sigmoid(g[:, :D_LRU] + vec_ref[5:6, :])
    ig = jax.nn.sigmoid(g[:, D_LRU:] + vec_ref[6:7, :])
    log_a = (-LRU_C * _softplus(-vec_ref[7:8, :])) * r
    a = jnp.exp(log_a)
    bx = jnp.sqrt(jnp.tanh(-log_a) * (1.0 + a * a)) * ig * u
    return a, bx


def _proj_lru_sample_attn_kernel(x_ref, w_ref, wg_ref, vec_ref,
                                 sq_ref, skn_ref, svn_ref, ssga_ref, cnt_ref, kt_ref, vt_ref,
                                 q_ref, k_ref, v_ref, hl_ref, sga_ref, conv_ref, hlast_ref,
                                 sya_ref, xb_ref, xc_ref, sgl_ref, a_ref, b_ref, carry_ref):
    j = pl.program_id(1)
    phase = pl.program_id(2)
    rows = x_ref.shape[0]

    _sample_attn_kernel(sq_ref, skn_ref, svn_ref, ssga_ref, cnt_ref, kt_ref, vt_ref, sya_ref)

    @pl.when(phase == 0)
    def _():
        @pl.when(j == 0)
        def _():
            xc_ref[0:8, :] = jnp.zeros((8, D_LRU), F32)
            carry_ref[...] = jnp.zeros((1, D_LRU), F32)

        xb_ref[...] = x_ref[...].astype(BF16)
        q_ref[...] = _dot(xb_ref[...], w_ref[:, C_Q:C_K])
        k_ref[...] = _dot(xb_ref[...], w_ref[:, C_K:C_V])
        xc_ref[8:8 + rows, :] = _dot(xb_ref[...], w_ref[:, C_XL:C_GL])

    @pl.when(phase == 1)
    def _():
        v_ref[...] = _dot(xb_ref[...], w_ref[:, C_V:C_GA])
        sga_ref[...] = _silu(_dot(xb_ref[...], w_ref[:, C_GA:C_XL])).astype(BF16)
        sgl_ref[...] = _silu(_dot(xb_ref[...], w_ref[:, C_GL:D_IN]))

    @pl.when(phase == 2)
    def _():
        u = (vec_ref[4:5, :]
             + xc_ref[5:5 + rows, :] * vec_ref[0:1, :]
             + xc_ref[6:6 + rows, :] * vec_ref[1:2, :]
             + xc_ref[7:7 + rows, :] * vec_ref[2:3, :]
             + xc_ref[8:8 + rows, :] * vec_ref[3:4, :])
        xc_ref[0:8, :] = xc_ref[rows:rows + 8, :]
        a, bx = _lru_gates(u, wg_ref, vec_ref)
        a_ref[...] = a
        b_ref[...] = bx

    @pl.when(phase == 3)
    def _():
        row = lax.broadcasted_iota(jnp.int32, (8, D_LRU), 0)

        def group(g, carry):
            r0 = pl.multiple_of(g * 8, 8)
            av = a_ref[pl.ds(r0, 8), :]
            bv = b_ref[pl.ds(r0, 8), :]
            for s in (1, 2, 4):
                keep = row >= s
                bv = jnp.where(keep, av * pltpu.roll(bv, s, axis=0) + bv, bv)
                av = jnp.where(keep, av * pltpu.roll(av, s, axis=0), av)
            h8 = av * carry + bv
            b_ref[pl.ds(r0, 8), :] = h8
            return h8[7:8, :]

        carry = lax.fori_loop(0, rows // 8, group, carry_ref[...], unroll=4)
        carry_ref[...] = carry

        hl_ref[...] = (b_ref[...] * sgl_ref[...]).astype(BF16)

        @pl.when(j == pl.num_programs(1) - 1)
        def _():
            conv_ref[...] = xc_ref[5:8, :]
            hlast_ref[...] = carry


def _proj_lru_sample_attn(x, w_bf, w_gate, vecs, q4, kn4, vn4, sga4, kt, vt):
    b, s, _ = x.shape
    nt = s // PROJ_ROWS
    t_steps, nb = q4.shape[:2]
    w_buf = kt.shape[-1]
    assert b * nt * PROJ_PHASES == nb, "one sample batch row per grid step"
    cnt = jnp.asarray(_pattern_counts(t_steps, w_buf))
    row_spec = lambda width: pl.BlockSpec((None, PROJ_ROWS, width), lambda i, j, c: (i, j, 0))
    const = lambda shape: pl.BlockSpec(shape, lambda i, j, c: (0,) * len(shape),
                                       pipeline_mode=pl.Buffered(1))
    batch_row = lambda i, j, c: (i * nt + j) * PROJ_PHASES + c
    tok = pl.BlockSpec((t_steps, None, 1, D_ATT), lambda i, j, c: (0, batch_row(i, j, c), 0, 0))
    cache = pl.BlockSpec((None, D_ATT, w_buf), lambda i, j, c: (batch_row(i, j, c), 0, 0))
    return pl.pallas_call(
        _proj_lru_sample_attn_kernel,
        grid=(b, nt, PROJ_PHASES),
        in_specs=[row_spec(D_MODEL), const((D_MODEL, D_IN)),
                  const((D_LRU, 2 * D_LRU)), const((8, D_LRU)),
                  tok, tok, tok, tok, const(cnt.shape), cache, cache],
        out_specs=[row_spec(D_ATT), row_spec(D_ATT), row_spec(D_ATT),
                   row_spec(D_LRU), row_spec(D_ATT),
                   pl.BlockSpec((None, CONV_WIDTH - 1, D_LRU), lambda i, j, c: (i, 0, 0)),
                   pl.BlockSpec((None, 1, D_LRU), lambda i, j, c: (i, 0, 0)),
                   tok],
        out_shape=[jax.ShapeDtypeStruct((b, s, D_ATT), F32),
                   jax.ShapeDtypeStruct((b, s, D_ATT), F32),
                   jax.ShapeDtypeStruct((b, s, D_ATT), F32),
                   jax.ShapeDtypeStruct((b, s, D_LRU), BF16),
                   jax.ShapeDtypeStruct((b, s, D_ATT), BF16),
                   jax.ShapeDtypeStruct((b, CONV_WIDTH - 1, D_LRU), F32),
                   jax.ShapeDtypeStruct((b, 1, D_LRU), F32),
                   jax.ShapeDtypeStruct((t_steps, nb, 1, D_ATT), F32)],
        scratch_shapes=[pltpu.VMEM((PROJ_ROWS, D_MODEL), BF16),
                        pltpu.VMEM((PROJ_ROWS + 8, D_LRU), F32),
                        pltpu.VMEM((PROJ_ROWS, D_LRU), F32),
                        pltpu.VMEM((PROJ_ROWS, D_LRU), F32),
                        pltpu.VMEM((PROJ_ROWS, D_LRU), F32),
                        pltpu.VMEM((1, D_LRU), F32)],
        compiler_params=pltpu.CompilerParams(
            dimension_semantics=("arbitrary", "arbitrary", "arbitrary"),
            vmem_limit_bytes=VMEM_LIMIT),
        name="prompt_proj_lru_sample_attn",
    )(x, w_bf, w_gate, vecs, q4, kn4, vn4, sga4, cnt, kt, vt)


def _prompt_attn_kernel(q_ref, k_ref, v_ref, sga_ref, o_ref, bias_ref, *stats):
    seq = q_ref.shape[0]
    accs, ms, ls = stats[0:3], stats[3:6], stats[6:9]

    row = lax.broadcasted_iota(jnp.int32, (BAND, 2 * BAND), 0)
    col = lax.broadcasted_iota(jnp.int32, (BAND, 2 * BAND), 1)
    band = (col >= row) & (col <= row + N_DIL_KEYS)
    bias_ref[0] = jnp.where(band & (col >= BAND), 0.0, NEG_INF)
    bias_ref[1] = jnp.where(band, 0.0, NEG_INF)
    head_a = lax.broadcasted_iota(jnp.int32, (BAND, HEAD_PAIR), 1) < HEAD_DIM
    one = jnp.ones((), BF16)

    for pi, d in enumerate(DILATIONS):
        span = BAND * d
        n_blk = seq // span
        acc_ref, m_ref, l_ref = accs[pi], ms[pi], ls[pi]

        def group(g, carry, d=d, span=span, n_blk=n_blk,
                  acc_ref=acc_ref, m_ref=m_ref, l_ref=l_ref):
            k_prev, v_prev = carry
            blk = g % n_blk
            start = blk * span + g // n_blk
            rows = pl.ds(start, BAND) if d == 1 else pl.ds(start, BAND, stride=d)

            qf = q_ref[rows, :]
            k_cur = k_ref[rows, :].astype(BF16)
            v_bf = v_ref[rows, :].astype(BF16)
            v_cur = jnp.concatenate([jnp.where(head_a, v_bf, one),
                                     jnp.where(head_a, one, v_bf)], axis=1)
            k2 = jnp.concatenate([k_prev, k_cur], axis=0)
            v4 = jnp.concatenate([v_prev, v_cur], axis=0)
            q2 = jnp.concatenate([jnp.where(head_a, qf, 0.0),
                                  jnp.where(head_a, 0.0, qf)], axis=0).astype(BF16)

            s = _dot_nt(q2, k2).reshape(2, BAND, 2 * BAND)
            s = s + bias_ref[jnp.minimum(blk, 1)][None]
            m = jnp.max(s, axis=-1, keepdims=True)
            p = jnp.exp(s - m).reshape(2 * BAND, 2 * BAND).astype(BF16)
            o = _dot(p, v4)

            acc_ref[rows, :] = jnp.where(head_a, o[:BAND, :HEAD_PAIR], o[BAND:, HEAD_PAIR:])
            l_ref[rows, :] = jnp.where(head_a, o[:BAND, HEAD_PAIR:], o[BAND:, :HEAD_PAIR])
            m_ref[rows, :] = jnp.where(head_a, m[0], m[1])
            return k_cur, v_cur

        init = (jnp.zeros((BAND, HEAD_PAIR), BF16), jnp.zeros((BAND, 2 * HEAD_PAIR), BF16))
        lax.fori_loop(0, seq // BAND, group, init, unroll=ATTN_UNROLL)

    chunk = 256

    def merge(c, carry):
        rows = pl.ds(pl.multiple_of(c * chunk, chunk), chunk)
        m = [r[rows, :] for r in ms]
        m_max = jnp.maximum(jnp.maximum(m[0], m[1]), m[2])
        e = [jnp.exp(mi - m_max) for mi in m]
        num = e[0] * accs[0][rows, :] + e[1] * accs[1][rows, :] + e[2] * accs[2][rows, :]
        den = e[0] * ls[0][rows, :] + e[1] * ls[1][rows, :] + e[2] * ls[2][rows, :]
        o_ref[rows, :] = (num / den * sga_ref[rows, :].astype(F32)).astype(BF16)
        return carry

    lax.fori_loop(0, seq // chunk, merge, 0)


def _prompt_attn(q, k, v, sga):
    b, s, _ = q.shape
    spec = pl.BlockSpec((None, s, HEAD_PAIR), lambda i, j: (i, 0, j))
    return pl.pallas_call(
        _prompt_attn_kernel,
        grid=(b, D_ATT // HEAD_PAIR),
        in_specs=[spec, spec, spec, spec],
        out_specs=spec,
        out_shape=jax.ShapeDtypeStruct((b, s, D_ATT), BF16),
        scratch_shapes=[pltpu.VMEM((2, BAND, 2 * BAND), F32)]
                       + [pltpu.VMEM((s, HEAD_PAIR), F32) for _ in range(9)],
        compiler_params=pltpu.CompilerParams(
            dimension_semantics=("arbitrary", "arbitrary"),
            vmem_limit_bytes=VMEM_LIMIT),
        name="prompt_attn",
    )(q, k, v, sga)


def _finish_kernel(ya_ref, hl_ref, x_ref, wo_ref, ln_ref, o_ref):
    sub = (_dot(ya_ref[...], wo_ref[0:D_ATT, :])
           + _dot(hl_ref[...], wo_ref[D_ATT:D_ATT + D_LRU, :]))
    z = ALPHA * x_ref[...] + sub
    mu = jnp.mean(z, axis=-1, keepdims=True)
    zc = z - mu
    var = jnp.mean(zc * zc, axis=-1, keepdims=True)
    o_ref[...] = zc * lax.rsqrt(var + LN_EPS) * ln_ref[0:1, :] + ln_ref[1:2, :]


def _finish(ya, hl, x, w_o, ln, name):
    n = x.shape[0]
    row_spec = lambda width: pl.BlockSpec((PROJ_ROWS, width), lambda i: (i, 0))
    const = lambda shape: pl.BlockSpec(shape, lambda i: (0, 0),
                                       pipeline_mode=pl.Buffered(1))
    return pl.pallas_call(
        _finish_kernel,
        grid=(n // PROJ_ROWS,),
        in_specs=[row_spec(D_ATT), row_spec(D_LRU), row_spec(D_MODEL),
                  const((D_ATT + D_LRU, D_MODEL)), const((2, D_MODEL))],
        out_specs=row_spec(D_MODEL),
        out_shape=jax.ShapeDtypeStruct((n, D_MODEL), F32),
        compiler_params=pltpu.CompilerParams(
            dimension_semantics=("arbitrary",), vmem_limit_bytes=VMEM_LIMIT),
        name=name,
    )(ya, hl, x, w_o, ln)


def _sample_proj_kernel(x_ref, sc_ref, sh_ref, w_ref, wg_ref, vec_ref,
                        q_ref, k_ref, v_ref, hl_ref, sga_ref, conv_ref, hlast_ref):
    t_steps, nb, _ = x_ref.shape
    xb = x_ref[...].reshape(t_steps * nb, D_MODEL).astype(BF16)
    split = lambda y: y.reshape(t_steps, nb, y.shape[-1])
    q_ref[...] = split(_dot(xb, w_ref[:, C_Q:C_K]))
    k_ref[...] = split(_dot(xb, w_ref[:, C_K:C_V]))
    v_ref[...] = split(_dot(xb, w_ref[:, C_V:C_GA]))
    sga_ref[...] = split(_silu(_dot(xb, w_ref[:, C_GA:C_XL])))

    xl = _dot(xb, w_ref[:, C_XL:C_GL])
    xc = [sc_ref[i] for i in range(CONV_WIDTH - 1)]
    xc += [xl[t * nb:(t + 1) * nb, :] for t in range(t_steps)]
    u = jnp.concatenate(
        [vec_ref[4:5, :] + sum(xc[t + w] * vec_ref[w:w + 1, :] for w in range(CONV_WIDTH))
         for t in range(t_steps)], axis=0)
    for i in range(CONV_WIDTH - 1):
        conv_ref[i] = xc[t_steps + i]

    a, bx = _lru_gates(u, wg_ref, vec_ref)
    sgl = _silu(_dot(xb, w_ref[:, C_GL:D_IN]))
    h = sh_ref[...]
    for t in range(t_steps):
        rows = slice(t * nb, (t + 1) * nb)
        h = a[rows, :] * h + bx[rows, :]
        hl_ref[t] = (h * sgl[rows, :]).astype(BF16)
    hlast_ref[...] = h


def _sample_proj(xs_t, sc_t, sh, w_bf, w_gate, vecs):
    t_steps, nb, _ = xs_t.shape
    full = lambda shape: pl.BlockSpec(shape, lambda i: (0,) * len(shape))
    tok = lambda dt: jax.ShapeDtypeStruct((t_steps, nb, D_ATT), dt)
    return pl.pallas_call(
        _sample_proj_kernel,
        grid=(1,),
        in_specs=[full(xs_t.shape), full(sc_t.shape), full(sh.shape),
                  full(w_bf.shape), full(w_gate.shape), full(vecs.shape)],
        out_specs=[full((t_steps, nb, D_ATT))] * 5
                  + [full(sc_t.shape), full(sh.shape)],
        out_shape=[tok(F32), tok(F32), tok(F32), tok(BF16), tok(F32),
                   jax.ShapeDtypeStruct(sc_t.shape, F32),
                   jax.ShapeDtypeStruct(sh.shape, F32)],
        compiler_params=pltpu.CompilerParams(
            dimension_semantics=("arbitrary",), vmem_limit_bytes=VMEM_LIMIT),
        name="sample_proj_lru",
    )(xs_t, sc_t, sh, w_bf, w_gate, vecs)


def _sample_attn_kernel(q_ref, kn_ref, vn_ref, sga_ref, cnt_ref, kt_ref, vt_ref, o_ref):
    t_steps = q_ref.shape[0]
    n_rows = t_steps * N_HEADS
    sub = lax.broadcasted_iota(jnp.int32, (N_HEADS, D_ATT), 0)
    lane_head = lax.broadcasted_iota(jnp.int32, (N_HEADS, D_ATT), 1) // HEAD_DIM
    own = sub == lane_head
    qbd = jnp.concatenate(
        [jnp.where(own, jnp.broadcast_to(q_ref[t], (N_HEADS, D_ATT)), 0.0)
         for t in range(t_steps)], axis=0)

    cnt = cnt_ref[...]
    s = jnp.where(cnt > 0.0, _dot(qbd.astype(BF16), kt_ref[...].astype(BF16)), NEG_INF)
    m = jnp.max(s, axis=-1, keepdims=True)

    row_t = lax.broadcasted_iota(jnp.int32, (n_rows, 1), 0) // N_HEADS
    s_new, c_new = [], []
    for t2 in range(t_steps):
        c = jnp.where(row_t == t2, float(len(DILATIONS)),
                      jnp.where(row_t > t2, 1.0, 0.0))
        sn = jnp.sum(qbd * kn_ref[t2], axis=-1, keepdims=True)
        sn = jnp.where(c > 0.0, sn, NEG_INF)
        m = jnp.maximum(m, sn)
        s_new.append(sn)
        c_new.append(c)

    p = jnp.exp(s - m) * cnt
    den = jnp.sum(p, axis=-1, keepdims=True)
    acc = _dot_nt(p.astype(BF16), vt_ref[...].astype(BF16))
    for t2 in range(t_steps):
        pn = jnp.exp(s_new[t2] - m) * c_new[t2]
        den = den + pn
        acc = acc + pn * vn_ref[t2]
    y = acc / den
    for t in range(t_steps):
        yt = jnp.sum(jnp.where(own, y[t * N_HEADS:(t + 1) * N_HEADS, :], 0.0),
                     axis=0, keepdims=True)
        o_ref[t] = yt * sga_ref[t]


def _pattern_counts(t_steps, w_buf):
    delta = w_buf + np.arange(t_steps)[:, None] - np.arange(w_buf)[None, :]
    cnt = np.zeros((t_steps, w_buf), np.float32)
    for d in DILATIONS:
        cnt += (delta % d == 0) & (delta // d <= N_DIL_KEYS)
    return np.repeat(cnt, N_HEADS, axis=0)


def kernel(x_prompt, x_sample, cache_k, cache_v, state_conv, state_h, w_in, conv_w, conv_b, w_ra, b_ra, w_ri, b_ri, lru_lambda, w_out, ln_g, ln_b):
    assert w_in.shape[0] == 1, "single layer"
    b, s, _ = x_prompt.shape
    nb, t_steps, _ = x_sample.shape
    w_buf = cache_k.shape[2]

    col_scale = jnp.concatenate([jnp.full((D_ATT,), HEAD_DIM ** -0.5, F32),
                                 jnp.ones((D_IN - D_ATT,), F32)])
    w_bf = (w_in[0] * col_scale).astype(BF16)
    eye = jnp.eye(LRU_BLOCKS, dtype=F32)
    block_diag = lambda wb: (eye[:, None, :, None] * wb[:, :, None, :]).reshape(D_LRU, D_LRU)
    w_gate = jnp.concatenate([block_diag(w_ra[0]), block_diag(w_ri[0])], axis=1).astype(BF16)
    vecs = jnp.concatenate([conv_w[0], conv_b, b_ra, b_ri, lru_lambda], axis=0)
    w_o = w_out[0].astype(BF16)
    ln = jnp.concatenate([ln_g, ln_b], axis=0)

    heads = lambda a: a.reshape(a.shape[:-1] + (N_HEADS, HEAD_DIM))

    xs_t = jnp.transpose(x_sample, (1, 0, 2))
    sc_t = jnp.transpose(state_conv[0], (1, 0, 2))
    q_s, k_s, v_s, hl_s, sga_s, conv_s, h_s = _sample_proj(
        xs_t, sc_t, state_h[0], w_bf, w_gate, vecs)
    four_d = lambda a: a.reshape(t_steps, nb, 1, D_ATT)
    kt = jnp.transpose(cache_k[0], (0, 2, 3, 1)).reshape(nb, D_ATT, w_buf)
    vt = jnp.transpose(cache_v[0], (0, 2, 3, 1)).reshape(nb, D_ATT, w_buf)

    q, k, v, hl, sga, conv_p, h_p, ya_s = _proj_lru_sample_attn(
        x_prompt, w_bf, w_gate, vecs,
        four_d(q_s), four_d(k_s), four_d(v_s), four_d(sga_s), kt, vt)
    ya = _prompt_attn(q, k, v, sga)
    y_p = _finish(ya.reshape(b * s, D_ATT), hl.reshape(b * s, D_LRU),
                  x_prompt.reshape(b * s, D_MODEL), w_o, ln, "prompt_finish")
    n_keep = min(w_buf, s)
    k_p = heads(k[:, s - n_keep:])[None]
    v_p = heads(v[:, s - n_keep:])[None]
    y_s = _finish(ya_s.reshape(t_steps * nb, D_ATT).astype(BF16),
                  hl_s.reshape(t_steps * nb, D_LRU),
                  xs_t.reshape(t_steps * nb, D_MODEL), w_o, ln, "sample_finish")
    to_batch_major = lambda a: jnp.transpose(a, (1, 0, 2))

    return (y_p.reshape(b, s, D_MODEL),
            to_batch_major(y_s.reshape(t_steps, nb, D_MODEL)),
            k_p, v_p,
            conv_p[None], h_p.reshape(1, b, D_LRU),
            heads(to_batch_major(k_s))[None], heads(to_batch_major(v_s))[None],
            to_batch_major(conv_s)[None], h_s[None])
```

```python
import numpy as np
import jax
import jax.numpy as jnp
from jax import lax
from jax.experimental import pallas as pl
from jax.experimental.pallas import tpu as pltpu

F32 = jnp.float32
BF16 = jnp.bfloat16

D_MODEL = 1024
N_HEADS = 8
HEAD_DIM = 64
D_ATT = N_HEADS * HEAD_DIM
D_LRU = 512
LRU_BLOCKS = 8
D_IN = 4 * D_ATT + 2 * D_LRU
CONV_WIDTH = 4
LRU_C = 8.0
DILATIONS = (1, 4, 16)
N_DIL_KEYS = 128
BAND = 128
ALPHA = 2.0 ** 0.25
LN_EPS = 1e-5
NEG_INF = -1e30

C_Q, C_K, C_V, C_GA, C_XL, C_GL = (0, D_ATT, 2 * D_ATT, 3 * D_ATT, 4 * D_ATT,
                                   4 * D_ATT + D_LRU)

VMEM_LIMIT = 56 * 1024 * 1024
PROJ_ROWS = 512
PROJ_PHASES = 2
HEAD_PAIR = 2 * HEAD_DIM
ATTN_UNROLL = 8
ATTN_PHASES = 4


def _dot(a, b):
    return jnp.dot(a, b, preferred_element_type=F32)


def _dot_nt(a, b):
    return lax.dot_general(a, b, (((1,), (1,)), ((), ())),
                           preferred_element_type=F32)


def _silu(x):
    return x * jax.nn.sigmoid(x)


def _softplus(x):
    return jnp.maximum(x, 0.0) + jnp.log1p(jnp.exp(-jnp.abs(x)))


def _lru_gates(u, wg_ref, vec_ref):
    g = _dot(u.astype(BF16), wg_ref[...])
    r = jax.nn.sigmoid(g[:, :D_LRU] + vec_ref[5:6, :])
    ig = jax.nn.sigmoid(g[:, D_LRU:] + vec_ref[6:7, :])
    log_a = (-LRU_C * _softplus(-vec_ref[7:8, :])) * r
    a = jnp.exp(log_a)
    bx = jnp.sqrt(jnp.tanh(-log_a) * (1.0 + a * a)) * ig * u
    return a, bx


def _sample_attn_step(q_ref, kn_ref, vn_ref, sga_ref, cnt_ref, kt_ref, vt_ref, o_ref):
    t_steps = q_ref.shape[0]
    n_rows = t_steps * N_HEADS
    sub = lax.broadcasted_iota(jnp.int32, (N_HEADS, D_ATT), 0)
    lane_head = lax.broadcasted_iota(jnp.int32, (N_HEADS, D_ATT), 1) // HEAD_DIM
    own = sub == lane_head
    qbd = jnp.concatenate(
        [jnp.where(own, jnp.broadcast_to(q_ref[t], (N_HEADS, D_ATT)), 0.0)
         for t in range(t_steps)], axis=0)

    cnt = cnt_ref[...]
    s = jnp.where(cnt > 0.0, _dot(qbd.astype(BF16), kt_ref[...].astype(BF16)), NEG_INF)
    m = jnp.max(s, axis=-1, keepdims=True)

    row_t = lax.broadcasted_iota(jnp.int32, (n_rows, 1), 0) // N_HEADS
    s_new, c_new = [], []
    for t2 in range(t_steps):
        c = jnp.where(row_t == t2, float(len(DILATIONS)),
                      jnp.where(row_t > t2, 1.0, 0.0))
        sn = jnp.sum(qbd * kn_ref[t2], axis=-1, keepdims=True)
        sn = jnp.where(c > 0.0, sn, NEG_INF)
        m = jnp.maximum(m, sn)
        s_new.append(sn)
        c_new.append(c)

    p = jnp.exp(s - m) * cnt
    den = jnp.sum(p, axis=-1, keepdims=True)
    acc = _dot_nt(p.astype(BF16), vt_ref[...].astype(BF16))
    for t2 in range(t_steps):
        pn = jnp.exp(s_new[t2] - m) * c_new[t2]
        den = den + pn
        acc = acc + pn * vn_ref[t2]
    y = acc / den
    for t in range(t_steps):
        yt = jnp.sum(jnp.where(own, y[t * N_HEADS:(t + 1) * N_HEADS, :], 0.0),
                     axis=0, keepdims=True)
        o_ref[t] = yt * sga_ref[t]


def _pattern_counts(t_steps, w_buf):
    delta = w_buf + np.arange(t_steps)[:, None] - np.arange(w_buf)[None, :]
    cnt = np.zeros((t_steps, w_buf), np.float32)
    for d in DILATIONS:
        cnt += (delta % d == 0) & (delta // d <= N_DIL_KEYS)
    return np.repeat(cnt, N_HEADS, axis=0)


def _sample_attn_operands(sample, batch_of, n_local):
    q4, kn4, vn4, sga4, kt, vt, first = sample
    t_steps = q4.shape[0]
    w_buf = kt.shape[-1]
    cnt = jnp.asarray(_pattern_counts(t_steps, w_buf))
    tok_block = (t_steps, None, 1, D_ATT)
    tok_in = pl.BlockSpec(tok_block, lambda *g: (0, first + batch_of(*g), 0, 0))
    cache = pl.BlockSpec((None, D_ATT, w_buf), lambda *g: (first + batch_of(*g), 0, 0))
    cnt_spec = pl.BlockSpec(cnt.shape, lambda *g: (0, 0), pipeline_mode=pl.Buffered(1))
    operands = (q4, kn4, vn4, sga4, cnt, kt, vt)
    in_specs = [tok_in, tok_in, tok_in, tok_in, cnt_spec, cache, cache]
    out_spec = pl.BlockSpec(tok_block, lambda *g: (0, batch_of(*g), 0, 0))
    out_shape = jax.ShapeDtypeStruct((t_steps, n_local, 1, D_ATT), F32)
    return operands, in_specs, out_spec, out_shape


def _proj_lru_kernel(x_ref, w_ref, wg_ref, vec_ref,
                     sq_ref, skn_ref, svn_ref, ssga_ref, cnt_ref, kt_ref, vt_ref,
                     q_ref, k_ref, v_ref, hl_ref, sga_ref, conv_ref, hlast_ref, sya_ref,
                     xb_ref, xc_ref, a_ref, b_ref, carry_ref):
    j = pl.program_id(1)
    phase = pl.program_id(2)
    rows = x_ref.shape[0]

    _sample_attn_step(sq_ref, skn_ref, svn_ref, ssga_ref, cnt_ref, kt_ref, vt_ref, sya_ref)

    @pl.when(phase == 0)
    def _():
        @pl.when(j == 0)
        def _():
            xc_ref[0:8, :] = jnp.zeros((8, D_LRU), F32)
            carry_ref[...] = jnp.zeros((1, D_LRU), F32)

        xb_ref[...] = x_ref[...].astype(BF16)
        xc_ref[8:8 + rows, :] = _dot(xb_ref[...], w_ref[:, C_XL:C_GL])
        u = (vec_ref[4:5, :]
             + xc_ref[5:5 + rows, :] * vec_ref[0:1, :]
             + xc_ref[6:6 + rows, :] * vec_ref[1:2, :]
             + xc_ref[7:7 + rows, :] * vec_ref[2:3, :]
             + xc_ref[8:8 + rows, :] * vec_ref[3:4, :])
        xc_ref[0:8, :] = xc_ref[rows:rows + 8, :]
        a, bx = _lru_gates(u, wg_ref, vec_ref)
        a_ref[...] = a
        b_ref[...] = bx
        q_ref[...] = _dot(xb_ref[...], w_ref[:, C_Q:C_K])
        k_ref[...] = _dot(xb_ref[...], w_ref[:, C_K:C_V])

    @pl.when(phase == 1)
    def _():
        v_ref[...] = _dot(xb_ref[...], w_ref[:, C_V:C_GA])
        sga_ref[...] = _silu(_dot(xb_ref[...], w_ref[:, C_GA:C_XL])).astype(BF16)
        sgl = _silu(_dot(xb_ref[...], w_ref[:, C_GL:D_IN]))

        row = lax.broadcasted_iota(jnp.int32, (8, D_LRU), 0)

        def group(g, carry):
            r0 = pl.multiple_of(g * 8, 8)
            av = a_ref[pl.ds(r0, 8), :]
            bv = b_ref[pl.ds(r0, 8), :]
            for s in (1, 2, 4):
                keep = row >= s
                bv = jnp.where(keep, av * pltpu.roll(bv, s, axis=0) + bv, bv)
                av = jnp.where(keep, av * pltpu.roll(av, s, axis=0), av)
            h8 = av * carry + bv
            b_ref[pl.ds(r0, 8), :] = h8
            return h8[7:8, :]

        carry = lax.fori_loop(0, rows // 8, group, carry_ref[...], unroll=8)
        carry_ref[...] = carry
        hl_ref[...] = (b_ref[...] * sgl).astype(BF16)

        @pl.when(j == pl.num_programs(1) - 1)
        def _():
            conv_ref[...] = xc_ref[5:8, :]
            hlast_ref[...] = carry


def _proj_lru(x, w_bf, w_gate, vecs, sample, n_local):
    b, s, _ = x.shape
    nt = s // PROJ_ROWS
    assert b * nt * PROJ_PHASES == n_local, "one sample batch row per grid step"
    s_ops, s_in, s_out, s_shape = _sample_attn_operands(
        sample, lambda i, j, c: (i * nt + j) * PROJ_PHASES + c, n_local)
    row_spec = lambda width: pl.BlockSpec((None, PROJ_ROWS, width), lambda i, j, c: (i, j, 0))
    const = lambda shape: pl.BlockSpec(shape, lambda i, j, c: (0,) * len(shape),
                                       pipeline_mode=pl.Buffered(1))
    per_batch = lambda n: pl.BlockSpec((None, n, D_LRU), lambda i, j, c: (i, 0, 0))
    tile = lambda dt: jax.ShapeDtypeStruct((b, s, D_ATT), dt)
    return pl.pallas_call(
        _proj_lru_kernel,
        grid=(b, nt, PROJ_PHASES),
        in_specs=[row_spec(D_MODEL), const((D_MODEL, D_IN)),
                  const((D_LRU, 2 * D_LRU)), const((8, D_LRU))] + s_in,
        out_specs=[row_spec(D_ATT), row_spec(D_ATT), row_spec(D_ATT),
                   row_spec(D_LRU), row_spec(D_ATT),
                   per_batch(CONV_WIDTH - 1), per_batch(1), s_out],
        out_shape=[tile(F32), tile(F32), tile(F32), tile(BF16), tile(BF16),
                   jax.ShapeDtypeStruct((b, CONV_WIDTH - 1, D_LRU), F32),
                   jax.ShapeDtypeStruct((b, 1, D_LRU), F32), s_shape],
        scratch_shapes=[pltpu.VMEM((PROJ_ROWS, D_MODEL), BF16),
                        pltpu.VMEM((PROJ_ROWS + 8, D_LRU), F32),
                        pltpu.VMEM((PROJ_ROWS, D_LRU), F32),
                        pltpu.VMEM((PROJ_ROWS, D_LRU), F32),
                        pltpu.VMEM((1, D_LRU), F32)],
        compiler_params=pltpu.CompilerParams(
            dimension_semantics=("arbitrary", "arbitrary", "arbitrary"),
            vmem_limit_bytes=VMEM_LIMIT),
        name="prompt_proj_lru",
    )(x, w_bf, w_gate, vecs, *s_ops)


def _prompt_attn_kernel(q_ref, k_ref, v_ref, sga_ref,
                        sq_ref, skn_ref, svn_ref, ssga_ref, cnt_ref, kt_ref, vt_ref,
                        o_ref, sya_ref, bias_ref, *stats):
    seq = q_ref.shape[0]
    phase = pl.program_id(2)
    n_groups = seq // BAND
    head_a = lax.broadcasted_iota(jnp.int32, (BAND, HEAD_PAIR), 1) < HEAD_DIM
    one = jnp.ones((), BF16)

    _sample_attn_step(sq_ref, skn_ref, svn_ref, ssga_ref, cnt_ref, kt_ref, vt_ref, sya_ref)

    def key_value_block(rows):
        v_bf = v_ref[rows, :].astype(BF16)
        v_ext = jnp.concatenate([jnp.where(head_a, v_bf, one),
                                 jnp.where(head_a, one, v_bf)], axis=1)
        return k_ref[rows, :].astype(BF16), v_ext

    def group(rows, blk, carry):
        k_prev, v_prev = carry
        k_cur, v_cur = key_value_block(rows)
        k2 = jnp.concatenate([k_prev, k_cur], axis=0)
        v4 = jnp.concatenate([v_prev, v_cur], axis=0)
        qf = q_ref[rows, :]
        q2 = jnp.concatenate([jnp.where(head_a, qf, 0.0),
                              jnp.where(head_a, 0.0, qf)], axis=0).astype(BF16)
        s = _dot_nt(q2, k2).reshape(2, BAND, 2 * BAND)
        s = s + bias_ref[jnp.minimum(blk, 1)][None]
        m = jnp.max(s, axis=-1, keepdims=True)
        p = jnp.exp(s - m).reshape(2 * BAND, 2 * BAND).astype(BF16)
        o = _dot(p, v4)
        acc = jnp.where(head_a, o[:BAND, :HEAD_PAIR], o[BAND:, HEAD_PAIR:])
        den = jnp.where(head_a, o[:BAND, HEAD_PAIR:], o[BAND:, :HEAD_PAIR])
        return acc, den, jnp.where(head_a, m[0], m[1]), (k_cur, v_cur)

    def strided_pattern(d, acc_ref, l_ref, m_ref):
        span = BAND * d
        n_blk = seq // span

        def body(g, carry):
            blk = g % n_blk
            rows = pl.ds(blk * span + g // n_blk, BAND, stride=d)
            acc, den, m, carry = group(rows, blk, carry)
            acc_ref[rows, :] = acc
            l_ref[rows, :] = den
            m_ref[rows, :] = m
            return carry

        init = (jnp.zeros((BAND, HEAD_PAIR), BF16), jnp.zeros((BAND, 2 * HEAD_PAIR), BF16))
        lax.fori_loop(0, n_groups, body, init, unroll=ATTN_UNROLL)

    @pl.when(phase == 0)
    def _():
        row = lax.broadcasted_iota(jnp.int32, (BAND, 2 * BAND), 0)
        col = lax.broadcasted_iota(jnp.int32, (BAND, 2 * BAND), 1)
        band = (col >= row) & (col <= row + N_DIL_KEYS)
        bias_ref[0] = jnp.where(band & (col >= BAND), 0.0, NEG_INF)
        bias_ref[1] = jnp.where(band, 0.0, NEG_INF)
        strided_pattern(DILATIONS[2], *stats[3:6])

    @pl.when(phase == 1)
    def _():
        strided_pattern(DILATIONS[1], *stats[0:3])

    @pl.when(phase >= 2)
    def _():
        half = n_groups // 2
        first = (phase - 2) * half

        def body(i, carry):
            g = first + i
            rows = pl.ds(pl.multiple_of(g * BAND, BAND), BAND)
            acc, den, m, carry = group(rows, g, carry)
            parts = [(acc, den, m)] + [(stats[3 * t][rows, :], stats[3 * t + 1][rows, :],
                                        stats[3 * t + 2][rows, :]) for t in range(2)]
            m_max = jnp.maximum(jnp.maximum(parts[0][2], parts[1][2]), parts[2][2])
            e = [jnp.exp(pt[2] - m_max) for pt in parts]
            num = sum(ei * pt[0] for ei, pt in zip(e, parts))
            tot = sum(ei * pt[1] for ei, pt in zip(e, parts))
            o_ref[rows, :] = (num / tot * sga_ref[rows, :].astype(F32)).astype(BF16)
            return carry

        before = pl.ds(pl.multiple_of(jnp.maximum(first - 1, 0) * BAND, BAND), BAND)
        lax.fori_loop(0, half, body, key_value_block(before), unroll=ATTN_UNROLL)


def _prompt_attn(q, k, v, sga, sample, n_local):
    b, s, _ = q.shape
    n_pairs = D_ATT // HEAD_PAIR
    assert b * n_pairs * ATTN_PHASES == n_local, "one sample batch row per grid step"
    s_ops, s_in, s_out, s_shape = _sample_attn_operands(
        sample, lambda i, j, c: (i * n_pairs + j) * ATTN_PHASES + c, n_local)
    spec = pl.BlockSpec((None, s, HEAD_PAIR), lambda i, j, c: (i, 0, j))
    return pl.pallas_call(
        _prompt_attn_kernel,
        grid=(b, n_pairs, ATTN_PHASES),
        in_specs=[spec, spec, spec, spec] + s_in,
        out_specs=[spec, s_out],
        out_shape=[jax.ShapeDtypeStruct((b, s, D_ATT), BF16), s_shape],
        scratch_shapes=[pltpu.VMEM((2, BAND, 2 * BAND), F32)]
                       + [pltpu.VMEM((s, HEAD_PAIR), F32) for _ in range(6)],
        compiler_params=pltpu.CompilerParams(
            dimension_semantics=("arbitrary", "arbitrary", "arbitrary"),
            vmem_limit_bytes=VMEM_LIMIT),
        name="prompt_attn",
    )(q, k, v, sga, *s_ops)


def _finish_kernel(ya_ref, hl_ref, x_ref, wo_ref, ln_ref, o_ref):
    sub = (_dot(ya_ref[...], wo_ref[0:D_ATT, :])
           + _dot(hl_ref[...], wo_ref[D_ATT:D_ATT + D_LRU, :]))
    z = ALPHA * x_ref[...] + sub
    mu = jnp.mean(z, axis=-1, keepdims=True)
    zc = z - mu
    var = jnp.mean(zc * zc, axis=-1, keepdims=True)
    o_ref[...] = zc * lax.rsqrt(var + LN_EPS) * ln_ref[0:1, :] + ln_ref[1:2, :]


def _finish(ya, hl, x, w_o, ln, name):
    n = x.shape[0]
    row_spec = lambda width: pl.BlockSpec((PROJ_ROWS, width), lambda i: (i, 0))
    const = lambda shape: pl.BlockSpec(shape, lambda i: (0, 0),
                                       pipeline_mode=pl.Buffered(1))
    return pl.pallas_call(
        _finish_kernel,
        grid=(n // PROJ_ROWS,),
        in_specs=[row_spec(D_ATT), row_spec(D_LRU), row_spec(D_MODEL),
                  const((D_ATT + D_LRU, D_MODEL)), const((2, D_MODEL))],
        out_specs=row_spec(D_MODEL),
        out_shape=jax.ShapeDtypeStruct((n, D_MODEL), F32),
        compiler_params=pltpu.CompilerParams(
            dimension_semantics=("arbitrary",), vmem_limit_bytes=VMEM_LIMIT),
        name=name,
    )(ya, hl, x, w_o, ln)


def _sample_proj_kernel(x_ref, sc_ref, sh_ref, w_ref, wg_ref, vec_ref,
                        q_ref, k_ref, v_ref, hl_ref, sga_ref, conv_ref, hlast_ref):
    t_steps, nb, _ = x_ref.shape
    xb = x_ref[...].reshape(t_steps * nb, D_MODEL).astype(BF16)
    split = lambda y: y.reshape(t_steps, nb, y.shape[-1])
    q_ref[...] = split(_dot(xb, w_ref[:, C_Q:C_K]))
    k_ref[...] = split(_dot(xb, w_ref[:, C_K:C_V]))
    v_ref[...] = split(_dot(xb, w_ref[:, C_V:C_GA]))
    sga_ref[...] = split(_silu(_dot(xb, w_ref[:, C_GA:C_XL])))

    xl = _dot(xb, w_ref[:, C_XL:C_GL])
    xc = [sc_ref[i] for i in range(CONV_WIDTH - 1)]
    xc += [xl[t * nb:(t + 1) * nb, :] for t in range(t_steps)]
    u = jnp.concatenate(
        [vec_ref[4:5, :] + sum(xc[t + w] * vec_ref[w:w + 1, :] for w in range(CONV_WIDTH))
         for t in range(t_steps)], axis=0)
    for i in range(CONV_WIDTH - 1):
        conv_ref[i] = xc[t_steps + i]

    a, bx = _lru_gates(u, wg_ref, vec_ref)
    sgl = _silu(_dot(xb, w_ref[:, C_GL:D_IN]))
    h = sh_ref[...]
    for t in range(t_steps):
        rows = slice(t * nb, (t + 1) * nb)
        h = a[rows, :] * h + bx[rows, :]
        hl_ref[t] = (h * sgl[rows, :]).astype(BF16)
    hlast_ref[...] = h


def _sample_proj(xs_t, sc_t, sh, w_bf, w_gate, vecs):
    t_steps, nb, _ = xs_t.shape
    full = lambda shape: pl.BlockSpec(shape, lambda i: (0,) * len(shape))
    tok = lambda dt: jax.ShapeDtypeStruct((t_steps, nb, D_ATT), dt)
    return pl.pallas_call(
        _sample_proj_kernel,
        grid=(1,),
        in_specs=[full(xs_t.shape), full(sc_t.shape), full(sh.shape),
                  full(w_bf.shape), full(w_gate.shape), full(vecs.shape)],
        out_specs=[full((t_steps, nb, D_ATT))] * 5
                  + [full(sc_t.shape), full(sh.shape)],
        out_shape=[tok(F32), tok(F32), tok(F32), tok(BF16), tok(F32),
                   jax.ShapeDtypeStruct(sc_t.shape, F32),
                   jax.ShapeDtypeStruct(sh.shape, F32)],
        compiler_params=pltpu.CompilerParams(
            dimension_semantics=("arbitrary",), vmem_limit_bytes=VMEM_LIMIT),
        name="sample_proj_lru",
    )(xs_t, sc_t, sh, w_bf, w_gate, vecs)


def kernel(x_prompt, x_sample, cache_k, cache_v, state_conv, state_h, w_in, conv_w, conv_b, w_ra, b_ra, w_ri, b_ri, lru_lambda, w_out, ln_g, ln_b):
    assert w_in.shape[0] == 1, "single layer"
    b, s, _ = x_prompt.shape
    nb, t_steps, _ = x_sample.shape
    w_buf = cache_k.shape[2]

    col_scale = jnp.concatenate([jnp.full((D_ATT,), HEAD_DIM ** -0.5, F32),
                                 jnp.ones((D_IN - D_ATT,), F32)])
    w_bf = (w_in[0] * col_scale).astype(BF16)
    eye = jnp.eye(LRU_BLOCKS, dtype=F32)
    block_diag = lambda wb: (eye[:, None, :, None] * wb[:, :, None, :]).reshape(D_LRU, D_LRU)
    w_gate = jnp.concatenate([block_diag(w_ra[0]), block_diag(w_ri[0])], axis=1).astype(BF16)
    vecs = jnp.concatenate([conv_w[0], conv_b, b_ra, b_ri, lru_lambda], axis=0)
    w_o = w_out[0].astype(BF16)
    ln = jnp.concatenate([ln_g, ln_b], axis=0)
    heads = lambda a: a.reshape(a.shape[:-1] + (N_HEADS, HEAD_DIM))

    xs_t = jnp.transpose(x_sample, (1, 0, 2))
    sc_t = jnp.transpose(state_conv[0], (1, 0, 2))
    q_s, k_s, v_s, hl_s, sga_s, conv_s, h_s = _sample_proj(
        xs_t, sc_t, state_h[0], w_bf, w_gate, vecs)
    four_d = lambda a: a.reshape(t_steps, nb, 1, D_ATT)
    kt = jnp.transpose(cache_k[0], (0, 2, 3, 1)).reshape(nb, D_ATT, w_buf)
    vt = jnp.transpose(cache_v[0], (0, 2, 3, 1)).reshape(nb, D_ATT, w_buf)
    sample = (four_d(q_s), four_d(k_s), four_d(v_s), four_d(sga_s), kt, vt)

    n_proj = b * (s // PROJ_ROWS) * PROJ_PHASES
    n_attn = b * (D_ATT // HEAD_PAIR) * ATTN_PHASES
    assert n_proj + n_attn == nb
    q, k, v, hl, sga, conv_p, h_p, ya_s0 = _proj_lru(
        x_prompt, w_bf, w_gate, vecs, sample + (0,), n_proj)
    ya, ya_s1 = _prompt_attn(q, k, v, sga, sample + (n_proj,), n_attn)
    y_p = _finish(ya.reshape(b * s, D_ATT), hl.reshape(b * s, D_LRU),
                  x_prompt.reshape(b * s, D_MODEL), w_o, ln, "prompt_finish")
    n_keep = min(w_buf, s)
    k_p = heads(k[:, s - n_keep:])[None]
    v_p = heads(v[:, s - n_keep:])[None]
    ya_s = jnp.concatenate([ya_s0, ya_s1], axis=1)
    y_s = _finish(ya_s.reshape(t_steps * nb, D_ATT).astype(BF16),
                  hl_s.reshape(t_steps * nb, D_LRU),
                  xs_t.reshape(t_steps * nb, D_MODEL), w_o, ln, "sample_finish")
    to_batch_major = lambda a: jnp.transpose(a, (1, 0, 2))

    return (y_p.reshape(b, s, D_MODEL),
            to_batch_major(y_s.reshape(t_steps, nb, D_MODEL)),
            k_p, v_p,
            conv_p[None], h_p.reshape(1, b, D_LRU),
            heads(to_batch_major(k_s))[None], heads(to_batch_major(v_s))[None],
            to_batch_major(conv_s)[None], h_s[None])
```

```python
import functools

import numpy as np
import jax
import jax.numpy as jnp
from jax import lax
from jax.experimental import pallas as pl
from jax.experimental.pallas import tpu as pltpu

F32 = jnp.float32
BF16 = jnp.bfloat16

D_MODEL = 1024
N_HEADS = 8
HEAD_DIM = 64
D_ATT = N_HEADS * HEAD_DIM
D_LRU = 512
LRU_BLOCKS = 8
D_IN = 4 * D_ATT + 2 * D_LRU
CONV_WIDTH = 4
LRU_C = 8.0
DILATIONS = (1, 4, 16)
N_DIL_KEYS = 128
BAND = 128
ALPHA = 2.0 ** 0.25
LN_EPS = 1e-5
NEG_INF = -1e30
LOG2_E = 1.4426950408889634

C_Q, C_K, C_V, C_GA, C_XL, C_GL = (0, D_ATT, 2 * D_ATT, 3 * D_ATT, 4 * D_ATT,
                                   4 * D_ATT + D_LRU)

VMEM_LIMIT = 56 * 1024 * 1024
PROJ_ROWS = 512
PROJ_PHASES = 2
HEAD_PAIR = 2 * HEAD_DIM
SUBLANES = 8
ATTN_UNROLL = 8
ATTN_PHASES = 4


def _dot(a, b):
    return jnp.dot(a, b, preferred_element_type=F32)


def _dot_nt(a, b):
    return lax.dot_general(a, b, (((1,), (1,)), ((), ())),
                           preferred_element_type=F32)


def _sigmoid(x):
    return 0.5 * jnp.tanh(0.5 * x) + 0.5


def _silu(x):
    return x * _sigmoid(x)


def _softplus(x):
    return jnp.maximum(x, 0.0) + jnp.log1p(jnp.exp(-jnp.abs(x)))


def _lru_gates(u, wg_ref, vec_ref):
    g = _dot(u.astype(BF16), wg_ref[...])
    r = _sigmoid(g[:, :D_LRU] + vec_ref[5:6, :])
    ig = _sigmoid(g[:, D_LRU:] + vec_ref[6:7, :])
    log_a = (-LRU_C * _softplus(-vec_ref[7:8, :])) * r
    a = jnp.exp(log_a)
    bx = jnp.sqrt(jnp.tanh(-log_a) * (1.0 + a * a)) * ig * u
    return a, bx


def _sample_attn_step(q_ref, kn_ref, vn_ref, sga_ref, cnt_ref, kt_ref, vt_ref, o_ref):
    t_steps = q_ref.shape[0]
    step = ((pl.program_id(0) * pl.num_programs(1) + pl.program_id(1)) * pl.num_programs(2)
            + pl.program_id(2))
    mine = pl.ds(step % SUBLANES, 1)
    n_rows = t_steps * N_HEADS
    sub = lax.broadcasted_iota(jnp.int32, (N_HEADS, D_ATT), 0)
    lane_head = lax.broadcasted_iota(jnp.int32, (N_HEADS, D_ATT), 1) // HEAD_DIM
    own = sub == lane_head
    qbd = jnp.concatenate(
        [jnp.where(own, jnp.broadcast_to(q_ref[t, mine, :], (N_HEADS, D_ATT)), 0.0)
         for t in range(t_steps)], axis=0)

    cnt = cnt_ref[...]
    s = jnp.where(cnt > 0.0, _dot(qbd.astype(BF16), kt_ref[...].astype(BF16)), NEG_INF)
    m = jnp.max(s, axis=-1, keepdims=True)

    row_t = lax.broadcasted_iota(jnp.int32, (n_rows, 1), 0) // N_HEADS
    s_new, c_new = [], []
    for t2 in range(t_steps):
        c = jnp.where(row_t == t2, float(len(DILATIONS)),
                      jnp.where(row_t > t2, 1.0, 0.0))
        sn = jnp.sum(qbd * kn_ref[t2, mine, :], axis=-1, keepdims=True)
        sn = jnp.where(c > 0.0, sn, NEG_INF)
        m = jnp.maximum(m, sn)
        s_new.append(sn)
        c_new.append(c)

    p = jnp.exp(s - m) * cnt
    den = jnp.sum(p, axis=-1, keepdims=True)
    acc = _dot_nt(p.astype(BF16), vt_ref[...].astype(BF16))
    for t2 in range(t_steps):
        pn = jnp.exp(s_new[t2] - m) * c_new[t2]
        den = den + pn
        acc = acc + pn * vn_ref[t2, mine, :]
    y = acc / den
    for t in range(t_steps):
        yt = jnp.sum(jnp.where(own, y[t * N_HEADS:(t + 1) * N_HEADS, :], 0.0),
                     axis=0, keepdims=True)
        o_ref[t, mine, :] = yt * sga_ref[t, mine, :]


def _pattern_counts(t_steps, w_buf):
    delta = w_buf + np.arange(t_steps)[:, None] - np.arange(w_buf)[None, :]
    cnt = np.zeros((t_steps, w_buf), np.float32)
    for d in DILATIONS:
        cnt += (delta % d == 0) & (delta // d <= N_DIL_KEYS)
    return np.repeat(cnt, N_HEADS, axis=0)


def _sample_attn_operands(sample, grid):
    q_s, kn_s, vn_s, sga_s, kt, vt, first = sample
    t_steps = q_s.shape[0]
    w_buf = kt.shape[-1]
    n_local = grid[0] * grid[1] * grid[2]
    assert first % SUBLANES == 0 and n_local % SUBLANES == 0
    cnt = jnp.asarray(_pattern_counts(t_steps, w_buf))
    step = lambda i, j, c: (i * grid[1] + j) * grid[2] + c
    tok_block = (t_steps, SUBLANES, D_ATT)
    tok_in = pl.BlockSpec(tok_block, lambda *g: (0, (first + step(*g)) // SUBLANES, 0))
    cache = pl.BlockSpec((None, D_ATT, w_buf), lambda *g: (first + step(*g), 0, 0))
    cnt_spec = pl.BlockSpec(cnt.shape, lambda *g: (0, 0), pipeline_mode=pl.Buffered(1))
    operands = (q_s, kn_s, vn_s, sga_s, cnt, kt, vt)
    in_specs = [tok_in, tok_in, tok_in, tok_in, cnt_spec, cache, cache]
    out_spec = pl.BlockSpec(tok_block, lambda *g: (0, step(*g) // SUBLANES, 0))
    out_shape = jax.ShapeDtypeStruct((t_steps, n_local, D_ATT), F32)
    return operands, in_specs, out_spec, out_shape


def _proj_lru_kernel(first_kept, x_ref, w_ref, wg_ref, vec_ref,
                     sq_ref, skn_ref, svn_ref, ssga_ref, cnt_ref, kt_ref, vt_ref,
                     q_ref, k_ref, v_ref, hl_ref, sga_ref, conv_ref, hlast_ref,
                     kt_keep_ref, vt_keep_ref, sya_ref,
                     xb_ref, xc_ref, a_ref, b_ref, carry_ref):
    j = pl.program_id(1)
    phase = pl.program_id(2)
    rows = x_ref.shape[0]
    kept = j >= first_kept

    _sample_attn_step(sq_ref, skn_ref, svn_ref, ssga_ref, cnt_ref, kt_ref, vt_ref, sya_ref)

    @pl.when(phase == 0)
    def _():
        @pl.when(j == 0)
        def _():
            xc_ref[0:8, :] = jnp.zeros((8, D_LRU), F32)
            carry_ref[...] = jnp.zeros((1, D_LRU), F32)

        xb_ref[...] = x_ref[...].astype(BF16)
        xc_ref[8:8 + rows, :] = _dot(xb_ref[...], w_ref[:, C_XL:C_GL])
        u = (vec_ref[4:5, :]
             + xc_ref[5:5 + rows, :] * vec_ref[0:1, :]
             + xc_ref[6:6 + rows, :] * vec_ref[1:2, :]
             + xc_ref[7:7 + rows, :] * vec_ref[2:3, :]
             + xc_ref[8:8 + rows, :] * vec_ref[3:4, :])
        xc_ref[0:8, :] = xc_ref[rows:rows + 8, :]
        a, bx = _lru_gates(u, wg_ref, vec_ref)
        a_ref[...] = a
        b_ref[...] = bx
        q_ref[...] = _dot(xb_ref[...], w_ref[:, C_Q:C_K]) * LOG2_E
        k_new = _dot(xb_ref[...], w_ref[:, C_K:C_V])
        k_ref[...] = k_new

        @pl.when(kept)
        def _():
            kt_keep_ref[...] = k_new.T

    @pl.when(phase == 1)
    def _():
        v_new = _dot(xb_ref[...], w_ref[:, C_V:C_GA])
        v_ref[...] = v_new
        sga_ref[...] = _silu(_dot(xb_ref[...], w_ref[:, C_GA:C_XL])).astype(BF16)
        sgl = _silu(_dot(xb_ref[...], w_ref[:, C_GL:D_IN]))

        row = lax.broadcasted_iota(jnp.int32, (8, D_LRU), 0)

        def group(g, carry):
            r0 = pl.multiple_of(g * 8, 8)
            av = a_ref[pl.ds(r0, 8), :]
            bv = b_ref[pl.ds(r0, 8), :]
            for s in (1, 2, 4):
                keep = row >= s
                bv = jnp.where(keep, av * pltpu.roll(bv, s, axis=0) + bv, bv)
                av = jnp.where(keep, av * pltpu.roll(av, s, axis=0), av)
            h8 = av * carry + bv
            b_ref[pl.ds(r0, 8), :] = h8
            return h8[7:8, :]

        carry = lax.fori_loop(0, rows // 8, group, carry_ref[...], unroll=True)
        carry_ref[...] = carry
        hl_ref[...] = (b_ref[...] * sgl).astype(BF16)

        @pl.when(kept)
        def _():
            vt_keep_ref[...] = v_new.T

        @pl.when(j == pl.num_programs(1) - 1)
        def _():
            conv_ref[...] = xc_ref[5:8, :]
            hlast_ref[...] = carry


def _proj_lru(x, w_bf, w_gate, vecs, sample, n_keep):
    b, s, _ = x.shape
    nt = s // PROJ_ROWS
    assert n_keep % PROJ_ROWS == 0
    first_kept = nt - n_keep // PROJ_ROWS
    keep_spec = pl.BlockSpec((None, D_ATT, PROJ_ROWS),
                             lambda i, j, c: (i, 0, jnp.maximum(j - first_kept, 0)))
    keep_shape = jax.ShapeDtypeStruct((b, D_ATT, n_keep), F32)
    grid = (b, nt, PROJ_PHASES)
    s_ops, s_in, s_out, s_shape = _sample_attn_operands(sample, grid)
    row_spec = lambda width: pl.BlockSpec((None, PROJ_ROWS, width), lambda i, j, c: (i, j, 0))
    const = lambda shape: pl.BlockSpec(shape, lambda i, j, c: (0,) * len(shape),
                                       pipeline_mode=pl.Buffered(1))
    per_batch = lambda n: pl.BlockSpec((None, n, D_LRU), lambda i, j, c: (i, 0, 0))
    tile = lambda dt: jax.ShapeDtypeStruct((b, s, D_ATT), dt)
    return pl.pallas_call(
        functools.partial(_proj_lru_kernel, first_kept),
        grid=grid,
        in_specs=[row_spec(D_MODEL), const((D_MODEL, D_IN)),
                  const((D_LRU, 2 * D_LRU)), const((8, D_LRU))] + s_in,
        out_specs=[row_spec(D_ATT), row_spec(D_ATT), row_spec(D_ATT),
                   row_spec(D_LRU), row_spec(D_ATT),
                   per_batch(CONV_WIDTH - 1), per_batch(1), keep_spec, keep_spec, s_out],
        out_shape=[tile(F32), tile(F32), tile(F32), tile(BF16), tile(BF16),
                   jax.ShapeDtypeStruct((b, CONV_WIDTH - 1, D_LRU), F32),
                   jax.ShapeDtypeStruct((b, 1, D_LRU), F32), keep_shape, keep_shape, s_shape],
        scratch_shapes=[pltpu.VMEM((PROJ_ROWS, D_MODEL), BF16),
                        pltpu.VMEM((PROJ_ROWS + 8, D_LRU), F32),
                        pltpu.VMEM((PROJ_ROWS, D_LRU), F32),
                        pltpu.VMEM((PROJ_ROWS, D_LRU), F32),
                        pltpu.VMEM((1, D_LRU), F32)],
        compiler_params=pltpu.CompilerParams(
            dimension_semantics=("arbitrary", "arbitrary", "arbitrary"),
            vmem_limit_bytes=VMEM_LIMIT),
        name="prompt_proj_lru",
    )(x, w_bf, w_gate, vecs, *s_ops)


def _prompt_attn_kernel(q_ref, k_ref, v_ref, sga_ref,
                        sq_ref, skn_ref, svn_ref, ssga_ref, cnt_ref, kt_ref, vt_ref,
                        o_ref, sya_ref, bias_ref, *stats):
    seq = q_ref.shape[0]
    phase = pl.program_id(2)
    n_groups = seq // BAND
    head_a = lax.broadcasted_iota(jnp.int32, (BAND, HEAD_PAIR), 1) < HEAD_DIM
    one = jnp.ones((), BF16)

    _sample_attn_step(sq_ref, skn_ref, svn_ref, ssga_ref, cnt_ref, kt_ref, vt_ref, sya_ref)

    def key_value_block(rows):
        v_bf = v_ref[rows, :].astype(BF16)
        v_ext = jnp.concatenate([jnp.where(head_a, v_bf, one),
                                 jnp.where(head_a, one, v_bf)], axis=1)
        return k_ref[rows, :].astype(BF16), v_ext

    def group(rows, blk, carry):
        k_prev, v_prev = carry
        k_cur, v_cur = key_value_block(rows)
        k2 = jnp.concatenate([k_prev, k_cur], axis=0)
        v4 = jnp.concatenate([v_prev, v_cur], axis=0)
        qf = q_ref[rows, :]
        q2 = jnp.concatenate([jnp.where(head_a, qf, 0.0),
                              jnp.where(head_a, 0.0, qf)], axis=0).astype(BF16)
        s = _dot_nt(q2, k2).reshape(2, BAND, 2 * BAND)
        s = s + bias_ref[jnp.minimum(blk, 1)][None]
        m = jnp.max(s, axis=-1, keepdims=True)
        p = jnp.exp2(s - m).reshape(2 * BAND, 2 * BAND).astype(BF16)
        o = _dot(p, v4)
        acc = jnp.where(head_a, o[:BAND, :HEAD_PAIR], o[BAND:, HEAD_PAIR:])
        den = jnp.where(head_a, o[:BAND, HEAD_PAIR:], o[BAND:, :HEAD_PAIR])
        return acc, den, jnp.where(head_a, m[0], m[1]), (k_cur, v_cur)

    def strided_pattern(d, acc_ref, l_ref, m_ref):
        span = BAND * d
        n_blk = seq // span

        def body(g, carry):
            blk = g % n_blk
            rows = pl.ds(blk * span + g // n_blk, BAND, stride=d)
            acc, den, m, carry = group(rows, blk, carry)
            acc_ref[rows, :] = acc
            l_ref[rows, :] = den
            m_ref[rows, :] = m
            return carry

        init = (jnp.zeros((BAND, HEAD_PAIR), BF16), jnp.zeros((BAND, 2 * HEAD_PAIR), BF16))
        lax.fori_loop(0, n_groups, body, init, unroll=ATTN_UNROLL)

    @pl.when(phase == 0)
    def _():
        row = lax.broadcasted_iota(jnp.int32, (BAND, 2 * BAND), 0)
        col = lax.broadcasted_iota(jnp.int32, (BAND, 2 * BAND), 1)
        band = (col >= row) & (col <= row + N_DIL_KEYS)
        bias_ref[0] = jnp.where(band & (col >= BAND), 0.0, NEG_INF)
        bias_ref[1] = jnp.where(band, 0.0, NEG_INF)
        strided_pattern(DILATIONS[2], *stats[3:6])

    @pl.when(phase == 1)
    def _():
        strided_pattern(DILATIONS[1], *stats[0:3])

    @pl.when(phase >= 2)
    def _():
        half = n_groups // 2
        first = (phase - 2) * half

        def body(i, carry):
            g = first + i
            rows = pl.ds(pl.multiple_of(g * BAND, BAND), BAND)
            acc, den, m, carry = group(rows, g, carry)
            parts = [(acc, den, m)] + [(stats[3 * t][rows, :], stats[3 * t + 1][rows, :],
                                        stats[3 * t + 2][rows, :]) for t in range(2)]
            m_max = jnp.maximum(jnp.maximum(parts[0][2], parts[1][2]), parts[2][2])
            e = [jnp.exp2(pt[2] - m_max) for pt in parts]
            num = sum(ei * pt[0] for ei, pt in zip(e, parts))
            tot = sum(ei * pt[1] for ei, pt in zip(e, parts))
            o_ref[rows, :] = (num / tot * sga_ref[rows, :].astype(F32)).astype(BF16)
            return carry

        before = pl.ds(pl.multiple_of(jnp.maximum(first - 1, 0) * BAND, BAND), BAND)
        lax.fori_loop(0, half, body, key_value_block(before), unroll=ATTN_UNROLL)


def _prompt_attn(q, k, v, sga, sample):
    b, s, _ = q.shape
    grid = (b, D_ATT // HEAD_PAIR, ATTN_PHASES)
    s_ops, s_in, s_out, s_shape = _sample_attn_operands(sample, grid)
    spec = pl.BlockSpec((None, s, HEAD_PAIR), lambda i, j, c: (i, 0, j))
    return pl.pallas_call(
        _prompt_attn_kernel,
        grid=grid,
        in_specs=[spec, spec, spec, spec] + s_in,
        out_specs=[spec, s_out],
        out_shape=[jax.ShapeDtypeStruct((b, s, D_ATT), BF16), s_shape],
        scratch_shapes=[pltpu.VMEM((2, BAND, 2 * BAND), F32)]
                       + [pltpu.VMEM((s, HEAD_PAIR), F32) for _ in range(6)],
        compiler_params=pltpu.CompilerParams(
            dimension_semantics=("arbitrary", "arbitrary", "arbitrary"),
            vmem_limit_bytes=VMEM_LIMIT),
        name="prompt_attn",
    )(q, k, v, sga, *s_ops)


def _finish_kernel(ya_ref, hl_ref, x_ref, wo_ref, ln_ref, o_ref):
    sub = (_dot(ya_ref[...], wo_ref[0:D_ATT, :])
           + _dot(hl_ref[...], wo_ref[D_ATT:D_ATT + D_LRU, :]))
    z = ALPHA * x_ref[...] + sub
    mu = jnp.mean(z, axis=-1, keepdims=True)
    zc = z - mu
    var = jnp.mean(zc * zc, axis=-1, keepdims=True)
    o_ref[...] = zc * lax.rsqrt(var + LN_EPS) * ln_ref[0:1, :] + ln_ref[1:2, :]


def _finish(ya, hl, x, w_o, ln, name):
    n = x.shape[0]
    row_spec = lambda width: pl.BlockSpec((PROJ_ROWS, width), lambda i: (i, 0))
    const = lambda shape: pl.BlockSpec(shape, lambda i: (0, 0),
                                       pipeline_mode=pl.Buffered(1))
    return pl.pallas_call(
        _finish_kernel,
        grid=(n // PROJ_ROWS,),
        in_specs=[row_spec(D_ATT), row_spec(D_LRU), row_spec(D_MODEL),
                  const((D_ATT + D_LRU, D_MODEL)), const((2, D_MODEL))],
        out_specs=row_spec(D_MODEL),
        out_shape=jax.ShapeDtypeStruct((n, D_MODEL), F32),
        compiler_params=pltpu.CompilerParams(
            dimension_semantics=("arbitrary",), vmem_limit_bytes=VMEM_LIMIT),
        name=name,
    )(ya, hl, x, w_o, ln)


def _sample_proj_kernel(x_ref, sc_ref, sh_ref, w_ref, wg_ref, vec_ref,
                        q_ref, k_ref, v_ref, hl_ref, sga_ref, conv_ref, hlast_ref):
    t_steps, nb, _ = x_ref.shape
    xb = x_ref[...].reshape(t_steps * nb, D_MODEL).astype(BF16)
    split = lambda y: y.reshape(t_steps, nb, y.shape[-1])
    q_ref[...] = split(_dot(xb, w_ref[:, C_Q:C_K]))
    k_ref[...] = split(_dot(xb, w_ref[:, C_K:C_V]))
    v_ref[...] = split(_dot(xb, w_ref[:, C_V:C_GA]))
    sga_ref[...] = split(_silu(_dot(xb, w_ref[:, C_GA:C_XL])))

    xl = _dot(xb, w_ref[:, C_XL:C_GL])
    xc = [sc_ref[i] for i in range(CONV_WIDTH - 1)]
    xc += [xl[t * nb:(t + 1) * nb, :] for t in range(t_steps)]
    u = jnp.concatenate(
        [vec_ref[4:5, :] + sum(xc[t + w] * vec_ref[w:w + 1, :] for w in range(CONV_WIDTH))
         for t in range(t_steps)], axis=0)
    for i in range(CONV_WIDTH - 1):
        conv_ref[i] = xc[t_steps + i]

    a, bx = _lru_gates(u, wg_ref, vec_ref)
    sgl = _silu(_dot(xb, w_ref[:, C_GL:D_IN]))
    h = sh_ref[...]
    for t in range(t_steps):
        rows = slice(t * nb, (t + 1) * nb)
        h = a[rows, :] * h + bx[rows, :]
        hl_ref[t] = (h * sgl[rows, :]).astype(BF16)
    hlast_ref[...] = h


def _sample_proj(xs_t, sc_t, sh, w_bf, w_gate, vecs):
    t_steps, nb, _ = xs_t.shape
    full = lambda shape: pl.BlockSpec(shape, lambda i: (0,) * len(shape))
    tok = lambda dt: jax.ShapeDtypeStruct((t_steps, nb, D_ATT), dt)
    return pl.pallas_call(
        _sample_proj_kernel,
        grid=(1,),
        in_specs=[full(xs_t.shape), full(sc_t.shape), full(sh.shape),
                  full(w_bf.shape), full(w_gate.shape), full(vecs.shape)],
        out_specs=[full((t_steps, nb, D_ATT))] * 5
                  + [full(sc_t.shape), full(sh.shape)],
        out_shape=[tok(F32), tok(F32), tok(F32), tok(BF16), tok(F32),
                   jax.ShapeDtypeStruct(sc_t.shape, F32),
                   jax.ShapeDtypeStruct(sh.shape, F32)],
        compiler_params=pltpu.CompilerParams(
            dimension_semantics=("arbitrary",), vmem_limit_bytes=VMEM_LIMIT),
        name="sample_proj_lru",
    )(xs_t, sc_t, sh, w_bf, w_gate, vecs)


def kernel(x_prompt, x_sample, cache_k, cache_v, state_conv, state_h, w_in, conv_w, conv_b, w_ra, b_ra, w_ri, b_ri, lru_lambda, w_out, ln_g, ln_b):
    assert w_in.shape[0] == 1, "single layer"
    b, s, _ = x_prompt.shape
    nb, t_steps, _ = x_sample.shape
    w_buf = cache_k.shape[2]

    col_scale = jnp.concatenate([jnp.full((D_ATT,), HEAD_DIM ** -0.5, F32),
                                 jnp.ones((D_IN - D_ATT,), F32)])
    w_bf = (w_in[0] * col_scale).astype(BF16)
    eye = jnp.eye(LRU_BLOCKS, dtype=F32)
    block_diag = lambda wb: (eye[:, None, :, None] * wb[:, :, None, :]).reshape(D_LRU, D_LRU)
    w_gate = jnp.concatenate([block_diag(w_ra[0]), block_diag(w_ri[0])], axis=1).astype(BF16)
    vecs = jnp.concatenate([conv_w[0], conv_b, b_ra, b_ri, lru_lambda], axis=0)
    w_o = w_out[0].astype(BF16)
    ln = jnp.concatenate([ln_g, ln_b], axis=0)
    heads = lambda a: a.reshape(a.shape[:-1] + (N_HEADS, HEAD_DIM))

    xs_t = jnp.transpose(x_sample, (1, 0, 2))
    sc_t = jnp.transpose(state_conv[0], (1, 0, 2))
    q_s, k_s, v_s, hl_s, sga_s, conv_s, h_s = _sample_proj(
        xs_t, sc_t, state_h[0], w_bf, w_gate, vecs)
    kt = jnp.transpose(cache_k[0], (0, 2, 3, 1)).reshape(nb, D_ATT, w_buf)
    vt = jnp.transpose(cache_v[0], (0, 2, 3, 1)).reshape(nb, D_ATT, w_buf)
    sample = (q_s, k_s, v_s, sga_s, kt, vt)

    n_proj = b * (s // PROJ_ROWS) * PROJ_PHASES
    n_attn = b * (D_ATT // HEAD_PAIR) * ATTN_PHASES
    assert n_proj + n_attn == nb
    n_keep = min(w_buf, s)
    q, k, v, hl, sga, conv_p, h_p, kt_p, vt_p, ya_s0 = _proj_lru(
        x_prompt, w_bf, w_gate, vecs, sample + (0,), n_keep)
    ya, ya_s1 = _prompt_attn(q, k, v, sga, sample + (n_proj,))
    y_p = _finish(ya.reshape(b * s, D_ATT), hl.reshape(b * s, D_LRU),
                  x_prompt.reshape(b * s, D_MODEL), w_o, ln, "prompt_finish")
    window = lambda a: jnp.transpose(a.reshape(b, N_HEADS, HEAD_DIM, n_keep), (0, 3, 1, 2))[None]
    k_p, v_p = window(kt_p), window(vt_p)
    ya_s = jnp.concatenate([ya_s0, ya_s1], axis=1)
    y_s = _finish(ya_s.reshape(t_steps * nb, D_ATT).astype(BF16),
                  hl_s.reshape(t_steps * nb, D_LRU),
                  xs_t.reshape(t_steps * nb, D_MODEL), w_o, ln, "sample_finish")
    to_batch_major = lambda a: jnp.transpose(a, (1, 0, 2))

    return (y_p.reshape(b, s, D_MODEL),
            to_batch_major(y_s.reshape(t_steps, nb, D_MODEL)),
            k_p, v_p,
            conv_p[None], h_p.reshape(1, b, D_LRU),
            heads(to_batch_major(k_s))[None], heads(to_batch_major(v_s))[None],
            to_batch_major(conv_s)[None], h_s[None])
```

```python
import functools

import numpy as np
import jax
import jax.numpy as jnp
from jax import lax
from jax.experimental import pallas as pl
from jax.experimental.pallas import tpu as pltpu

F32 = jnp.float32
BF16 = jnp.bfloat16

D_MODEL = 1024
N_HEADS = 8
HEAD_DIM = 64
D_ATT = N_HEADS * HEAD_DIM
D_LRU = 512
LRU_BLOCKS = 8
D_IN = 4 * D_ATT + 2 * D_LRU
CONV_WIDTH = 4
LRU_C = 8.0
DILATIONS = (1, 4, 16)
N_DIL_KEYS = 128
BAND = 128
ALPHA = 2.0 ** 0.25
LN_EPS = 1e-5
NEG_INF = -1e30
LOG2_E = 1.4426950408889634

C_Q, C_K, C_V, C_GA, C_XL, C_GL = (0, D_ATT, 2 * D_ATT, 3 * D_ATT, 4 * D_ATT,
                                   4 * D_ATT + D_LRU)

VMEM_LIMIT = 56 * 1024 * 1024
PROJ_ROWS = 512
FINISH_ROWS = 1024
PROJ_PHASES = 2
HEAD_PAIR = 2 * HEAD_DIM
SUBLANES = 8
ATTN_UNROLL = 16
ATTN_PHASES = 4


def _dot(a, b):
    return jnp.dot(a, b, preferred_element_type=F32)


def _dot_nt(a, b):
    return lax.dot_general(a, b, (((1,), (1,)), ((), ())),
                           preferred_element_type=F32)


def _sigmoid(x):
    return 0.5 * jnp.tanh(0.5 * x) + 0.5


def _silu(x):
    return x * _sigmoid(x)


def _softplus(x):
    return jnp.maximum(x, 0.0) + jnp.log1p(jnp.exp(-jnp.abs(x)))


def _lru_gates(u, wg_ref, vec_ref):
    g = _dot(u.astype(BF16), wg_ref[...])
    r = _sigmoid(g[:, :D_LRU] + vec_ref[5:6, :])
    ig = _sigmoid(g[:, D_LRU:] + vec_ref[6:7, :])
    log_a = (-LRU_C * _softplus(-vec_ref[7:8, :])) * r
    a = jnp.exp(log_a)
    bx = jnp.sqrt(jnp.tanh(-log_a) * (1.0 + a * a)) * ig * u
    return a, bx


def _sample_attn_step(q_ref, kn_ref, vn_ref, sga_ref, cnt_ref, kt_ref, vt_ref, o_ref):
    t_steps = q_ref.shape[0]
    step = ((pl.program_id(0) * pl.num_programs(1) + pl.program_id(1)) * pl.num_programs(2)
            + pl.program_id(2))
    mine = pl.ds(step % SUBLANES, 1)
    n_rows = t_steps * N_HEADS
    sub = lax.broadcasted_iota(jnp.int32, (N_HEADS, D_ATT), 0)
    lane_head = lax.broadcasted_iota(jnp.int32, (N_HEADS, D_ATT), 1) // HEAD_DIM
    own = sub == lane_head
    qbd = jnp.concatenate(
        [jnp.where(own, jnp.broadcast_to(q_ref[t, mine, :], (N_HEADS, D_ATT)), 0.0)
         for t in range(t_steps)], axis=0)

    cnt = cnt_ref[...]
    s = jnp.where(cnt > 0.0, _dot(qbd.astype(BF16), kt_ref[...].astype(BF16)), NEG_INF)
    m = jnp.max(s, axis=-1, keepdims=True)

    row_t = lax.broadcasted_iota(jnp.int32, (n_rows, 1), 0) // N_HEADS
    s_new, c_new = [], []
    for t2 in range(t_steps):
        c = jnp.where(row_t == t2, float(len(DILATIONS)),
                      jnp.where(row_t > t2, 1.0, 0.0))
        sn = jnp.sum(qbd * kn_ref[t2, mine, :], axis=-1, keepdims=True)
        sn = jnp.where(c > 0.0, sn, NEG_INF)
        m = jnp.maximum(m, sn)
        s_new.append(sn)
        c_new.append(c)

    p = jnp.exp(s - m) * cnt
    den = jnp.sum(p, axis=-1, keepdims=True)
    acc = _dot_nt(p.astype(BF16), vt_ref[...].astype(BF16))
    for t2 in range(t_steps):
        pn = jnp.exp(s_new[t2] - m) * c_new[t2]
        den = den + pn
        acc = acc + pn * vn_ref[t2, mine, :]
    y = acc / den
    for t in range(t_steps):
        yt = jnp.sum(jnp.where(own, y[t * N_HEADS:(t + 1) * N_HEADS, :], 0.0),
                     axis=0, keepdims=True)
        o_ref[t, mine, :] = yt * sga_ref[t, mine, :]


def _pattern_counts(t_steps, w_buf):
    delta = w_buf + np.arange(t_steps)[:, None] - np.arange(w_buf)[None, :]
    cnt = np.zeros((t_steps, w_buf), np.float32)
    for d in DILATIONS:
        cnt += (delta % d == 0) & (delta // d <= N_DIL_KEYS)
    return np.repeat(cnt, N_HEADS, axis=0)


def _sample_attn_operands(sample, grid):
    q_s, kn_s, vn_s, sga_s, kt, vt, first = sample
    t_steps = q_s.shape[0]
    w_buf = kt.shape[-1]
    n_local = grid[0] * grid[1] * grid[2]
    assert first % SUBLANES == 0 and n_local % SUBLANES == 0
    cnt = jnp.asarray(_pattern_counts(t_steps, w_buf))
    step = lambda i, j, c: (i * grid[1] + j) * grid[2] + c
    tok_block = (t_steps, SUBLANES, D_ATT)
    tok_in = pl.BlockSpec(tok_block, lambda *g: (0, (first + step(*g)) // SUBLANES, 0))
    cache = pl.BlockSpec((None, D_ATT, w_buf), lambda *g: (first + step(*g), 0, 0))
    cnt_spec = pl.BlockSpec(cnt.shape, lambda *g: (0, 0), pipeline_mode=pl.Buffered(1))
    operands = (q_s, kn_s, vn_s, sga_s, cnt, kt, vt)
    in_specs = [tok_in, tok_in, tok_in, tok_in, cnt_spec, cache, cache]
    out_spec = pl.BlockSpec(tok_block, lambda *g: (0, step(*g) // SUBLANES, 0))
    out_shape = jax.ShapeDtypeStruct((t_steps, n_local, D_ATT), F32)
    return operands, in_specs, out_spec, out_shape


def _proj_lru_kernel(first_kept, x_ref, w_ref, wg_ref, vec_ref,
                     sq_ref, skn_ref, svn_ref, ssga_ref, cnt_ref, kt_ref, vt_ref,
                     q_ref, k_ref, v_ref, hl_ref, sga_ref, conv_ref, hlast_ref,
                     kt_keep_ref, vt_keep_ref, sya_ref,
                     xb_ref, xc_ref, a_ref, b_ref, carry_ref):
    j = pl.program_id(1)
    phase = pl.program_id(2)
    rows = x_ref.shape[0]
    kept = j >= first_kept

    _sample_attn_step(sq_ref, skn_ref, svn_ref, ssga_ref, cnt_ref, kt_ref, vt_ref, sya_ref)

    @pl.when(phase == 0)
    def _():
        @pl.when(j == 0)
        def _():
            xc_ref[0:8, :] = jnp.zeros((8, D_LRU), F32)
            carry_ref[...] = jnp.zeros((1, D_LRU), F32)

        xb_ref[...] = x_ref[...].astype(BF16)
        xc_ref[8:8 + rows, :] = _dot(xb_ref[...], w_ref[:, C_XL:C_GL])
        u = (vec_ref[4:5, :]
             + xc_ref[5:5 + rows, :] * vec_ref[0:1, :]
             + xc_ref[6:6 + rows, :] * vec_ref[1:2, :]
             + xc_ref[7:7 + rows, :] * vec_ref[2:3, :]
             + xc_ref[8:8 + rows, :] * vec_ref[3:4, :])
        xc_ref[0:8, :] = xc_ref[rows:rows + 8, :]
        a, bx = _lru_gates(u, wg_ref, vec_ref)
        a_ref[...] = a
        b_ref[...] = bx
        q_ref[...] = _dot(xb_ref[...], w_ref[:, C_Q:C_K]) * LOG2_E
        k_new = _dot(xb_ref[...], w_ref[:, C_K:C_V])
        k_ref[...] = k_new

        @pl.when(kept)
        def _():
            kt_keep_ref[...] = k_new.T

    @pl.when(phase == 1)
    def _():
        v_new = _dot(xb_ref[...], w_ref[:, C_V:C_GA])
        v_ref[...] = v_new
        sga_ref[...] = _silu(_dot(xb_ref[...], w_ref[:, C_GA:C_XL])).astype(BF16)
        sgl = _silu(_dot(xb_ref[...], w_ref[:, C_GL:D_IN]))

        row = lax.broadcasted_iota(jnp.int32, (8, D_LRU), 0)

        def group(g, carry):
            r0 = pl.multiple_of(g * 8, 8)
            av = a_ref[pl.ds(r0, 8), :]
            bv = b_ref[pl.ds(r0, 8), :]
            for s in (1, 2, 4):
                keep = row >= s
                bv = jnp.where(keep, av * pltpu.roll(bv, s, axis=0) + bv, bv)
                av = jnp.where(keep, av * pltpu.roll(av, s, axis=0), av)
            h8 = av * carry + bv
            b_ref[pl.ds(r0, 8), :] = h8
            return h8[7:8, :]

        carry = lax.fori_loop(0, rows // 8, group, carry_ref[...], unroll=True)
        carry_ref[...] = carry
        hl_ref[...] = (b_ref[...] * sgl).astype(BF16)

        @pl.when(kept)
        def _():
            vt_keep_ref[...] = v_new.T

        @pl.when(j == pl.num_programs(1) - 1)
        def _():
            conv_ref[...] = xc_ref[5:8, :]
            hlast_ref[...] = carry


def _proj_lru(x, w_bf, w_gate, vecs, sample, n_keep):
    b, s, _ = x.shape
    nt = s // PROJ_ROWS
    assert n_keep % PROJ_ROWS == 0
    first_kept = nt - n_keep // PROJ_ROWS
    keep_spec = pl.BlockSpec((None, D_ATT, PROJ_ROWS),
                             lambda i, j, c: (i, 0, jnp.maximum(j - first_kept, 0)))
    keep_shape = jax.ShapeDtypeStruct((b, D_ATT, n_keep), F32)
    grid = (b, nt, PROJ_PHASES)
    s_ops, s_in, s_out, s_shape = _sample_attn_operands(sample, grid)
    row_spec = lambda width: pl.BlockSpec((None, PROJ_ROWS, width), lambda i, j, c: (i, j, 0))
    const = lambda shape: pl.BlockSpec(shape, lambda i, j, c: (0,) * len(shape),
                                       pipeline_mode=pl.Buffered(1))
    per_batch = lambda n: pl.BlockSpec((None, n, D_LRU), lambda i, j, c: (i, 0, 0))
    tile = lambda dt: jax.ShapeDtypeStruct((b, s, D_ATT), dt)
    return pl.pallas_call(
        functools.partial(_proj_lru_kernel, first_kept),
        grid=grid,
        in_specs=[row_spec(D_MODEL), const((D_MODEL, D_IN)),
                  const((D_LRU, 2 * D_LRU)), const((8, D_LRU))] + s_in,
        out_specs=[row_spec(D_ATT), row_spec(D_ATT), row_spec(D_ATT),
                   row_spec(D_LRU), row_spec(D_ATT),
                   per_batch(CONV_WIDTH - 1), per_batch(1), keep_spec, keep_spec, s_out],
        out_shape=[tile(F32), tile(F32), tile(F32), tile(BF16), tile(BF16),
                   jax.ShapeDtypeStruct((b, CONV_WIDTH - 1, D_LRU), F32),
                   jax.ShapeDtypeStruct((b, 1, D_LRU), F32), keep_shape, keep_shape, s_shape],
        scratch_shapes=[pltpu.VMEM((PROJ_ROWS, D_MODEL), BF16),
                        pltpu.VMEM((PROJ_ROWS + 8, D_LRU), F32),
                        pltpu.VMEM((PROJ_ROWS, D_LRU), F32),
                        pltpu.VMEM((PROJ_ROWS, D_LRU), F32),
                        pltpu.VMEM((1, D_LRU), F32)],
        compiler_params=pltpu.CompilerParams(
            dimension_semantics=("arbitrary", "arbitrary", "arbitrary"),
            vmem_limit_bytes=VMEM_LIMIT),
        name="prompt_proj_lru",
    )(x, w_bf, w_gate, vecs, *s_ops)


def _prompt_attn_kernel(q_ref, k_ref, v_ref, sga_ref,
                        sq_ref, skn_ref, svn_ref, ssga_ref, cnt_ref, kt_ref, vt_ref,
                        o_ref, sya_ref, bias_ref, *stats):
    seq = q_ref.shape[0]
    phase = pl.program_id(2)
    n_groups = seq // BAND
    head_a = lax.broadcasted_iota(jnp.int32, (BAND, HEAD_PAIR), 1) < HEAD_DIM
    one = jnp.ones((), BF16)

    _sample_attn_step(sq_ref, skn_ref, svn_ref, ssga_ref, cnt_ref, kt_ref, vt_ref, sya_ref)

    def key_value_block(rows):
        v_bf = v_ref[rows, :].astype(BF16)
        v_ext = jnp.concatenate([jnp.where(head_a, v_bf, one),
                                 jnp.where(head_a, one, v_bf)], axis=1)
        return k_ref[rows, :].astype(BF16), v_ext

    def group(rows, blk, carry):
        k_prev, v_prev = carry
        k_cur, v_cur = key_value_block(rows)
        k2 = jnp.concatenate([k_prev, k_cur], axis=0)
        v4 = jnp.concatenate([v_prev, v_cur], axis=0)
        qf = q_ref[rows, :]
        q2 = jnp.concatenate([jnp.where(head_a, qf, 0.0),
                              jnp.where(head_a, 0.0, qf)], axis=0).astype(BF16)
        s = _dot_nt(q2, k2).reshape(2, BAND, 2 * BAND)
        s = s + bias_ref[jnp.minimum(blk, 1)][None]
        m = jnp.max(s, axis=-1, keepdims=True)
        p = jnp.exp2(s - m).reshape(2 * BAND, 2 * BAND).astype(BF16)
        o = _dot(p, v4)
        acc = jnp.where(head_a, o[:BAND, :HEAD_PAIR], o[BAND:, HEAD_PAIR:])
        den = jnp.where(head_a, o[:BAND, HEAD_PAIR:], o[BAND:, :HEAD_PAIR])
        return acc, den, jnp.where(head_a, m[0], m[1]), (k_cur, v_cur)

    zero_block = (jnp.zeros((BAND, HEAD_PAIR), BF16), jnp.zeros((BAND, 2 * HEAD_PAIR), BF16))

    def strided_pattern(d, first, count, acc_ref, l_ref, m_ref):
        span = BAND * d
        n_blk = seq // span

        def body(i, carry):
            g = first + i
            blk = g % n_blk
            rows = pl.ds(blk * span + g // n_blk, BAND, stride=d)
            acc, den, m, carry = group(rows, blk, carry)
            acc_ref[rows, :] = acc
            l_ref[rows, :] = den
            m_ref[rows, :] = m
            return carry

        lax.fori_loop(0, count, body, zero_block, unroll=True)

    half = n_groups // 2

    @pl.when(phase == 0)
    def _():
        row = lax.broadcasted_iota(jnp.int32, (BAND, 2 * BAND), 0)
        col = lax.broadcasted_iota(jnp.int32, (BAND, 2 * BAND), 1)
        band = (col >= row) & (col <= row + N_DIL_KEYS)
        bias_ref[0] = jnp.where(band & (col >= BAND), 0.0, NEG_INF)
        bias_ref[1] = jnp.where(band, 0.0, NEG_INF)
        strided_pattern(DILATIONS[1], 0, n_groups, *stats[0:3])

    @pl.when(phase == 1)
    def _():
        strided_pattern(DILATIONS[2], 0, half, *stats[3:6])

    @pl.when(phase == 2)
    def _():
        strided_pattern(DILATIONS[2], half, half, *stats[3:6])

    @pl.when(phase == 3)
    def _():
        def body(g, carry):
            rows = pl.ds(pl.multiple_of(g * BAND, BAND), BAND)
            acc, den, m, carry = group(rows, g, carry)
            parts = [(acc, den, m)] + [(stats[3 * t][rows, :], stats[3 * t + 1][rows, :],
                                        stats[3 * t + 2][rows, :]) for t in range(2)]
            m_max = jnp.maximum(jnp.maximum(parts[0][2], parts[1][2]), parts[2][2])
            e = [jnp.exp2(pt[2] - m_max) for pt in parts]
            num = sum(ei * pt[0] for ei, pt in zip(e, parts))
            tot = sum(ei * pt[1] for ei, pt in zip(e, parts))
            o_ref[rows, :] = (num / tot * sga_ref[rows, :].astype(F32)).astype(BF16)
            return carry

        lax.fori_loop(0, n_groups, body, zero_block, unroll=ATTN_UNROLL)


def _prompt_attn(q, k, v, sga, sample):
    b, s, _ = q.shape
    grid = (b, D_ATT // HEAD_PAIR, ATTN_PHASES)
    s_ops, s_in, s_out, s_shape = _sample_attn_operands(sample, grid)
    spec = pl.BlockSpec((None, s, HEAD_PAIR), lambda i, j, c: (i, 0, j))
    return pl.pallas_call(
        _prompt_attn_kernel,
        grid=grid,
        in_specs=[spec, spec, spec, spec] + s_in,
        out_specs=[spec, s_out],
        out_shape=[jax.ShapeDtypeStruct((b, s, D_ATT), BF16), s_shape],
        scratch_shapes=[pltpu.VMEM((2, BAND, 2 * BAND), F32)]
                       + [pltpu.VMEM((s, HEAD_PAIR), F32) for _ in range(6)],
        compiler_params=pltpu.CompilerParams(
            dimension_semantics=("arbitrary", "arbitrary", "arbitrary"),
            vmem_limit_bytes=VMEM_LIMIT),
        name="prompt_attn",
    )(q, k, v, sga, *s_ops)


def _finish_kernel(ya_ref, hl_ref, x_ref, wo_ref, ln_ref, o_ref):
    sub = (_dot(ya_ref[...], wo_ref[0:D_ATT, :])
           + _dot(hl_ref[...], wo_ref[D_ATT:D_ATT + D_LRU, :]))
    z = ALPHA * x_ref[...] + sub
    mu = jnp.mean(z, axis=-1, keepdims=True)
    zc = z - mu
    var = jnp.mean(zc * zc, axis=-1, keepdims=True)
    o_ref[...] = zc * lax.rsqrt(var + LN_EPS) * ln_ref[0:1, :] + ln_ref[1:2, :]


def _finish(ya, hl, x, w_o, ln, name):
    n = x.shape[0]
    rows = min(n, FINISH_ROWS)
    row_spec = lambda width: pl.BlockSpec((rows, width), lambda i: (i, 0))
    const = lambda shape: pl.BlockSpec(shape, lambda i: (0, 0),
                                       pipeline_mode=pl.Buffered(1))
    return pl.pallas_call(
        _finish_kernel,
        grid=(n // rows,),
        in_specs=[row_spec(D_ATT), row_spec(D_LRU), row_spec(D_MODEL),
                  const((D_ATT + D_LRU, D_MODEL)), const((2, D_MODEL))],
        out_specs=row_spec(D_MODEL),
        out_shape=jax.ShapeDtypeStruct((n, D_MODEL), F32),
        compiler_params=pltpu.CompilerParams(
            dimension_semantics=("arbitrary",), vmem_limit_bytes=VMEM_LIMIT),
        name=name,
    )(ya, hl, x, w_o, ln)


def _sample_proj_kernel(x_ref, sc_ref, sh_ref, w_ref, wg_ref, vec_ref,
                        q_ref, k_ref, v_ref, hl_ref, sga_ref, conv_ref, hlast_ref):
    t_steps, nb, _ = x_ref.shape
    xb = x_ref[...].reshape(t_steps * nb, D_MODEL).astype(BF16)
    split = lambda y: y.reshape(t_steps, nb, y.shape[-1])
    q_ref[...] = split(_dot(xb, w_ref[:, C_Q:C_K]))
    k_ref[...] = split(_dot(xb, w_ref[:, C_K:C_V]))
    v_ref[...] = split(_dot(xb, w_ref[:, C_V:C_GA]))
    sga_ref[...] = split(_silu(_dot(xb, w_ref[:, C_GA:C_XL])))

    xl = _dot(xb, w_ref[:, C_XL:C_GL])
    xc = [sc_ref[i] for i in range(CONV_WIDTH - 1)]
    xc += [xl[t * nb:(t + 1) * nb, :] for t in range(t_steps)]
    u = jnp.concatenate(
        [vec_ref[4:5, :] + sum(xc[t + w] * vec_ref[w:w + 1, :] for w in range(CONV_WIDTH))
         for t in range(t_steps)], axis=0)
    for i in range(CONV_WIDTH - 1):
        conv_ref[i] = xc[t_steps + i]

    a, bx = _lru_gates(u, wg_ref, vec_ref)
    sgl = _silu(_dot(xb, w_ref[:, C_GL:D_IN]))
    h = sh_ref[...]
    for t in range(t_steps):
        rows = slice(t * nb, (t + 1) * nb)
        h = a[rows, :] * h + bx[rows, :]
        hl_ref[t] = (h * sgl[rows, :]).astype(BF16)
    hlast_ref[...] = h


def _sample_proj(xs_t, sc_t, sh, w_bf, w_gate, vecs):
    t_steps, nb, _ = xs_t.shape
    full = lambda shape: pl.BlockSpec(shape, lambda i: (0,) * len(shape))
    tok = lambda dt: jax.ShapeDtypeStruct((t_steps, nb, D_ATT), dt)
    return pl.pallas_call(
        _sample_proj_kernel,
        grid=(1,),
        in_specs=[full(xs_t.shape), full(sc_t.shape), full(sh.shape),
                  full(w_bf.shape), full(w_gate.shape), full(vecs.shape)],
        out_specs=[full((t_steps, nb, D_ATT))] * 5
                  + [full(sc_t.shape), full(sh.shape)],
        out_shape=[tok(F32), tok(F32), tok(F32), tok(BF16), tok(F32),
                   jax.ShapeDtypeStruct(sc_t.shape, F32),
                   jax.ShapeDtypeStruct(sh.shape, F32)],
        compiler_params=pltpu.CompilerParams(
            dimension_semantics=("arbitrary",), vmem_limit_bytes=VMEM_LIMIT),
        name="sample_proj_lru",
    )(xs_t, sc_t, sh, w_bf, w_gate, vecs)


def kernel(x_prompt, x_sample, cache_k, cache_v, state_conv, state_h, w_in, conv_w, conv_b, w_ra, b_ra, w_ri, b_ri, lru_lambda, w_out, ln_g, ln_b):
    assert w_in.shape[0] == 1, "single layer"
    b, s, _ = x_prompt.shape
    nb, t_steps, _ = x_sample.shape
    w_buf = cache_k.shape[2]

    col_scale = jnp.concatenate([jnp.full((D_ATT,), HEAD_DIM ** -0.5, F32),
                                 jnp.ones((D_IN - D_ATT,), F32)])
    w_bf = (w_in[0] * col_scale).astype(BF16)
    eye = jnp.eye(LRU_BLOCKS, dtype=F32)
    block_diag = lambda wb: (eye[:, None, :, None] * wb[:, :, None, :]).reshape(D_LRU, D_LRU)
    w_gate = jnp.concatenate([block_diag(w_ra[0]), block_diag(w_ri[0])], axis=1).astype(BF16)
    vecs = jnp.concatenate([conv_w[0], conv_b, b_ra, b_ri, lru_lambda], axis=0)
    w_o = w_out[0].astype(BF16)
    ln = jnp.concatenate([ln_g, ln_b], axis=0)
    heads = lambda a: a.reshape(a.shape[:-1] + (N_HEADS, HEAD_DIM))

    xs_t = jnp.transpose(x_sample, (1, 0, 2))
    sc_t = jnp.transpose(state_conv[0], (1, 0, 2))
    q_s, k_s, v_s, hl_s, sga_s, conv_s, h_s = _sample_proj(
        xs_t, sc_t, state_h[0], w_bf, w_gate, vecs)
    kt = jnp.transpose(cache_k[0], (0, 2, 3, 1)).reshape(nb, D_ATT, w_buf)
    vt = jnp.transpose(cache_v[0], (0, 2, 3, 1)).reshape(nb, D_ATT, w_buf)
    sample = (q_s, k_s, v_s, sga_s, kt, vt)

    n_proj = b * (s // PROJ_ROWS) * PROJ_PHASES
    n_attn = b * (D_ATT // HEAD_PAIR) * ATTN_PHASES
    assert n_proj + n_attn == nb
    n_keep = min(w_buf, s)
    q, k, v, hl, sga, conv_p, h_p, kt_p, vt_p, ya_s0 = _proj_lru(
        x_prompt, w_bf, w_gate, vecs, sample + (0,), n_keep)
    ya, ya_s1 = _prompt_attn(q, k, v, sga, sample + (n_proj,))
    y_p = _finish(ya.reshape(b * s, D_ATT), hl.reshape(b * s, D_LRU),
                  x_prompt.reshape(b * s, D_MODEL), w_o, ln, "prompt_finish")
    window = lambda a: jnp.transpose(a.reshape(b, N_HEADS, HEAD_DIM, n_keep), (0, 3, 1, 2))[None]
    k_p, v_p = window(kt_p), window(vt_p)
    ya_s = jnp.concatenate([ya_s0, ya_s1], axis=1)
    y_s = _finish(ya_s.reshape(t_steps * nb, D_ATT).astype(BF16),
                  hl_s.reshape(t_steps * nb, D_LRU),
                  xs_t.reshape(t_steps * nb, D_MODEL), w_o, ln, "sample_finish")
    to_batch_major = lambda a: jnp.transpose(a, (1, 0, 2))

    return (y_p.reshape(b, s, D_MODEL),
            to_batch_major(y_s.reshape(t_steps, nb, D_MODEL)),
            k_p, v_p,
            conv_p[None], h_p.reshape(1, b, D_LRU),
            heads(to_batch_major(k_s))[None], heads(to_batch_major(v_s))[None],
            to_batch_major(conv_s)[None], h_s[None])
```

```python
import functools

import numpy as np
import jax
import jax.numpy as jnp
from jax import lax
from jax.experimental import pallas as pl
from jax.experimental.pallas import tpu as pltpu

F32 = jnp.float32
BF16 = jnp.bfloat16

D_MODEL = 1024
N_HEADS = 8
HEAD_DIM = 64
D_ATT = N_HEADS * HEAD_DIM
D_LRU = 512
LRU_BLOCKS = 8
D_IN = 4 * D_ATT + 2 * D_LRU
CONV_WIDTH = 4
LRU_C = 8.0
DILATIONS = (1, 4, 16)
N_DIL_KEYS = 128
BAND = 128
ALPHA = 2.0 ** 0.25
LN_EPS = 1e-5
NEG_INF = -1e30
LOG2_E = 1.4426950408889634

C_Q, C_K, C_V, C_GA, C_XL, C_GL = (0, D_ATT, 2 * D_ATT, 3 * D_ATT, 4 * D_ATT,
                                   4 * D_ATT + D_LRU)

VMEM_LIMIT = 56 * 1024 * 1024
PROJ_ROWS = 512
FINISH_ROWS = 1024
PROJ_PHASES = 2
HEAD_PAIR = 2 * HEAD_DIM
SUBLANES = 8
ATTN_UNROLL = 16
ATTN_PHASES = 4


def _dot(a, b):
    return jnp.dot(a, b, preferred_element_type=F32)


def _dot_nt(a, b):
    return lax.dot_general(a, b, (((1,), (1,)), ((), ())),
                           preferred_element_type=F32)


def _sigmoid(x):
    return 0.5 * jnp.tanh(0.5 * x) + 0.5


def _silu(x):
    return x * _sigmoid(x)


def _softplus(x):
    return jnp.maximum(x, 0.0) + jnp.log1p(jnp.exp(-jnp.abs(x)))


def _lru_gates(u, wg_ref, vec_ref):
    g = _dot(u.astype(BF16), wg_ref[...])
    r = _sigmoid(g[:, :D_LRU] + vec_ref[5:6, :])
    ig = _sigmoid(g[:, D_LRU:] + vec_ref[6:7, :])
    log_a = (-LRU_C * _softplus(-vec_ref[7:8, :])) * r
    a = jnp.exp(log_a)
    bx = jnp.sqrt(jnp.tanh(-log_a) * (1.0 + a * a)) * ig * u
    return a, bx


def _sample_attn_step(q_ref, kn_ref, vn_ref, sga_ref, cnt_ref, kt_ref, vt_ref, o_ref):
    t_steps = q_ref.shape[0]
    step = ((pl.program_id(0) * pl.num_programs(1) + pl.program_id(1)) * pl.num_programs(2)
            + pl.program_id(2))
    mine = pl.ds(step % SUBLANES, 1)
    n_rows = t_steps * N_HEADS
    sub = lax.broadcasted_iota(jnp.int32, (N_HEADS, D_ATT), 0)
    lane_head = lax.broadcasted_iota(jnp.int32, (N_HEADS, D_ATT), 1) // HEAD_DIM
    own = sub == lane_head
    qbd = jnp.concatenate(
        [jnp.where(own, jnp.broadcast_to(q_ref[t, mine, :], (N_HEADS, D_ATT)), 0.0)
         for t in range(t_steps)], axis=0)

    cnt = cnt_ref[...]
    s = jnp.where(cnt > 0.0, _dot(qbd.astype(BF16), kt_ref[...].astype(BF16)), NEG_INF)
    m = jnp.max(s, axis=-1, keepdims=True)

    row_t = lax.broadcasted_iota(jnp.int32, (n_rows, 1), 0) // N_HEADS
    s_new, c_new = [], []
    for t2 in range(t_steps):
        c = jnp.where(row_t == t2, float(len(DILATIONS)),
                      jnp.where(row_t > t2, 1.0, 0.0))
        sn = jnp.sum(qbd * kn_ref[t2, mine, :], axis=-1, keepdims=True)
        sn = jnp.where(c > 0.0, sn, NEG_INF)
        m = jnp.maximum(m, sn)
        s_new.append(sn)
        c_new.append(c)

    p = jnp.exp(s - m) * cnt
    den = jnp.sum(p, axis=-1, keepdims=True)
    acc = _dot_nt(p.astype(BF16), vt_ref[...].astype(BF16))
    for t2 in range(t_steps):
        pn = jnp.exp(s_new[t2] - m) * c_new[t2]
        den = den + pn
        acc = acc + pn * vn_ref[t2, mine, :]
    y = acc / den
    for t in range(t_steps):
        yt = jnp.sum(jnp.where(own, y[t * N_HEADS:(t + 1) * N_HEADS, :], 0.0),
                     axis=0, keepdims=True)
        o_ref[t, mine, :] = yt * sga_ref[t, mine, :]


def _pattern_counts(t_steps, w_buf):
    delta = w_buf + np.arange(t_steps)[:, None] - np.arange(w_buf)[None, :]
    cnt = np.zeros((t_steps, w_buf), np.float32)
    for d in DILATIONS:
        cnt += (delta % d == 0) & (delta // d <= N_DIL_KEYS)
    return np.repeat(cnt, N_HEADS, axis=0)


def _sample_attn_operands(sample, grid):
    q_s, kn_s, vn_s, sga_s, kt, vt, first = sample
    t_steps = q_s.shape[0]
    w_buf = kt.shape[-1]
    n_local = grid[0] * grid[1] * grid[2]
    assert first % SUBLANES == 0 and n_local % SUBLANES == 0
    cnt = jnp.asarray(_pattern_counts(t_steps, w_buf))
    step = lambda i, j, c: (i * grid[1] + j) * grid[2] + c
    tok_block = (t_steps, SUBLANES, D_ATT)
    tok_in = pl.BlockSpec(tok_block, lambda *g: (0, (first + step(*g)) // SUBLANES, 0))
    cache = pl.BlockSpec((None, D_ATT, w_buf), lambda *g: (first + step(*g), 0, 0))
    cnt_spec = pl.BlockSpec(cnt.shape, lambda *g: (0, 0), pipeline_mode=pl.Buffered(1))
    operands = (q_s, kn_s, vn_s, sga_s, cnt, kt, vt)
    in_specs = [tok_in, tok_in, tok_in, tok_in, cnt_spec, cache, cache]
    out_spec = pl.BlockSpec(tok_block, lambda *g: (0, step(*g) // SUBLANES, 0))
    out_shape = jax.ShapeDtypeStruct((t_steps, n_local, D_ATT), F32)
    return operands, in_specs, out_spec, out_shape


def _proj_lru_kernel(first_kept, x_ref, w_ref, wg_ref, vec_ref,
                     sq_ref, skn_ref, svn_ref, ssga_ref, cnt_ref, kt_ref, vt_ref,
                     q_ref, k_ref, v_ref, hl_ref, sga_ref, conv_ref, hlast_ref,
                     kt_keep_ref, vt_keep_ref, sya_ref,
                     xb_ref, xc_ref, a_ref, b_ref, carry_ref):
    j = pl.program_id(1)
    phase = pl.program_id(2)
    rows = x_ref.shape[0]
    kept = j >= first_kept
    sample_step = functools.partial(_sample_attn_step, sq_ref, skn_ref, svn_ref, ssga_ref,
                                    cnt_ref, kt_ref, vt_ref, sya_ref)

    @pl.when(phase == 0)
    def _():
        @pl.when(j == 0)
        def _():
            xc_ref[0:8, :] = jnp.zeros((8, D_LRU), F32)
            carry_ref[...] = jnp.zeros((1, D_LRU), F32)

        sample_step()
        xb_ref[...] = x_ref[...].astype(BF16)
        xc_ref[8:8 + rows, :] = _dot(xb_ref[...], w_ref[:, C_XL:C_GL])
        u = (vec_ref[4:5, :]
             + xc_ref[5:5 + rows, :] * vec_ref[0:1, :]
             + xc_ref[6:6 + rows, :] * vec_ref[1:2, :]
             + xc_ref[7:7 + rows, :] * vec_ref[2:3, :]
             + xc_ref[8:8 + rows, :] * vec_ref[3:4, :])
        xc_ref[0:8, :] = xc_ref[rows:rows + 8, :]
        a, bx = _lru_gates(u, wg_ref, vec_ref)
        a_ref[...] = a
        b_ref[...] = bx
        q_ref[...] = _dot(xb_ref[...], w_ref[:, C_Q:C_K]) * LOG2_E
        k_new = _dot(xb_ref[...], w_ref[:, C_K:C_V])
        k_ref[...] = k_new

        @pl.when(kept)
        def _():
            kt_keep_ref[...] = k_new.T

    @pl.when(phase == 1)
    def _():
        sample_step()
        v_new = _dot(xb_ref[...], w_ref[:, C_V:C_GA])
        v_ref[...] = v_new
        sga_ref[...] = _silu(_dot(xb_ref[...], w_ref[:, C_GA:C_XL])).astype(BF16)
        sgl = _silu(_dot(xb_ref[...], w_ref[:, C_GL:D_IN]))

        row = lax.broadcasted_iota(jnp.int32, (8, D_LRU), 0)

        def group(g, carry):
            r0 = pl.multiple_of(g * 8, 8)
            av = a_ref[pl.ds(r0, 8), :]
            bv = b_ref[pl.ds(r0, 8), :]
            for s in (1, 2, 4):
                keep = row >= s
                bv = jnp.where(keep, av * pltpu.roll(bv, s, axis=0) + bv, bv)
                av = jnp.where(keep, av * pltpu.roll(av, s, axis=0), av)
            h8 = av * carry + bv
            b_ref[pl.ds(r0, 8), :] = h8
            return h8[7:8, :]

        carry = lax.fori_loop(0, rows // 8, group, carry_ref[...], unroll=True)
        carry_ref[...] = carry
        hl_ref[...] = (b_ref[...] * sgl).astype(BF16)

        @pl.when(kept)
        def _():
            vt_keep_ref[...] = v_new.T

        @pl.when(j == pl.num_programs(1) - 1)
        def _():
            conv_ref[...] = xc_ref[5:8, :]
            hlast_ref[...] = carry


def _proj_lru(x, w_bf, w_gate, vecs, sample, n_keep):
    b, s, _ = x.shape
    nt = s // PROJ_ROWS
    assert n_keep % PROJ_ROWS == 0
    first_kept = nt - n_keep // PROJ_ROWS
    keep_spec = pl.BlockSpec((None, D_ATT, PROJ_ROWS),
                             lambda i, j, c: (i, 0, jnp.maximum(j - first_kept, 0)))
    keep_shape = jax.ShapeDtypeStruct((b, D_ATT, n_keep), F32)
    grid = (b, nt, PROJ_PHASES)
    s_ops, s_in, s_out, s_shape = _sample_attn_operands(sample, grid)
    row_spec = lambda width: pl.BlockSpec((None, PROJ_ROWS, width), lambda i, j, c: (i, j, 0))
    const = lambda shape: pl.BlockSpec(shape, lambda i, j, c: (0,) * len(shape),
                                       pipeline_mode=pl.Buffered(1))
    per_batch = lambda n: pl.BlockSpec((None, n, D_LRU), lambda i, j, c: (i, 0, 0))
    tile = lambda dt: jax.ShapeDtypeStruct((b, s, D_ATT), dt)
    return pl.pallas_call(
        functools.partial(_proj_lru_kernel, first_kept),
        grid=grid,
        in_specs=[row_spec(D_MODEL), const((D_MODEL, D_IN)),
                  const((D_LRU, 2 * D_LRU)), const((8, D_LRU))] + s_in,
        out_specs=[row_spec(D_ATT), row_spec(D_ATT), row_spec(D_ATT),
                   row_spec(D_LRU), row_spec(D_ATT),
                   per_batch(CONV_WIDTH - 1), per_batch(1), keep_spec, keep_spec, s_out],
        out_shape=[tile(F32), tile(F32), tile(F32), tile(BF16), tile(BF16),
                   jax.ShapeDtypeStruct((b, CONV_WIDTH - 1, D_LRU), F32),
                   jax.ShapeDtypeStruct((b, 1, D_LRU), F32), keep_shape, keep_shape, s_shape],
        scratch_shapes=[pltpu.VMEM((PROJ_ROWS, D_MODEL), BF16),
                        pltpu.VMEM((PROJ_ROWS + 8, D_LRU), F32),
                        pltpu.VMEM((PROJ_ROWS, D_LRU), F32),
                        pltpu.VMEM((PROJ_ROWS, D_LRU), F32),
                        pltpu.VMEM((1, D_LRU), F32)],
        compiler_params=pltpu.CompilerParams(
            dimension_semantics=("arbitrary", "arbitrary", "arbitrary"),
            vmem_limit_bytes=VMEM_LIMIT),
        name="prompt_proj_lru",
    )(x, w_bf, w_gate, vecs, *s_ops)


def _prompt_attn_kernel(q_ref, k_ref, v_ref, sga_ref,
                        sq_ref, skn_ref, svn_ref, ssga_ref, cnt_ref, kt_ref, vt_ref,
                        o_ref, sya_ref, bias_ref, *stats):
    seq = q_ref.shape[0]
    phase = pl.program_id(2)
    n_groups = seq // BAND
    head_a = lax.broadcasted_iota(jnp.int32, (BAND, HEAD_PAIR), 1) < HEAD_DIM
    one = jnp.ones((), BF16)

    sample_step = functools.partial(_sample_attn_step, sq_ref, skn_ref, svn_ref, ssga_ref,
                                    cnt_ref, kt_ref, vt_ref, sya_ref)

    def key_value_block(rows):
        v_bf = v_ref[rows, :].astype(BF16)
        v_ext = jnp.concatenate([jnp.where(head_a, v_bf, one),
                                 jnp.where(head_a, one, v_bf)], axis=1)
        return k_ref[rows, :].astype(BF16), v_ext

    def group(rows, blk, carry):
        k_prev, v_prev = carry
        k_cur, v_cur = key_value_block(rows)
        k2 = jnp.concatenate([k_prev, k_cur], axis=0)
        v4 = jnp.concatenate([v_prev, v_cur], axis=0)
        qf = q_ref[rows, :]
        q2 = jnp.concatenate([jnp.where(head_a, qf, 0.0),
                              jnp.where(head_a, 0.0, qf)], axis=0).astype(BF16)
        s = _dot_nt(q2, k2).reshape(2, BAND, 2 * BAND)
        s = s + bias_ref[jnp.minimum(blk, 1)][None]
        m = jnp.max(s, axis=-1, keepdims=True)
        p = jnp.exp2(s - m).reshape(2 * BAND, 2 * BAND).astype(BF16)
        o = _dot(p, v4)
        acc = jnp.where(head_a, o[:BAND, :HEAD_PAIR], o[BAND:, HEAD_PAIR:])
        den = jnp.where(head_a, o[:BAND, HEAD_PAIR:], o[BAND:, :HEAD_PAIR])
        return acc, den, jnp.where(head_a, m[0], m[1]), (k_cur, v_cur)

    zero_block = (jnp.zeros((BAND, HEAD_PAIR), BF16), jnp.zeros((BAND, 2 * HEAD_PAIR), BF16))

    def strided_pattern(d, first, count, acc_ref, l_ref, m_ref):
        span = BAND * d
        n_blk = seq // span

        def body(i, carry):
            g = first + i
            blk = g % n_blk
            rows = pl.ds(blk * span + g // n_blk, BAND, stride=d)
            acc, den, m, carry = group(rows, blk, carry)
            acc_ref[rows, :] = acc
            l_ref[rows, :] = den
            m_ref[rows, :] = m
            return carry

        lax.fori_loop(0, count, body, zero_block, unroll=True)

    half = n_groups // 2

    @pl.when(phase == 0)
    def _():
        row = lax.broadcasted_iota(jnp.int32, (BAND, 2 * BAND), 0)
        col = lax.broadcasted_iota(jnp.int32, (BAND, 2 * BAND), 1)
        band = (col >= row) & (col <= row + N_DIL_KEYS)
        bias_ref[0] = jnp.where(band & (col >= BAND), 0.0, NEG_INF)
        bias_ref[1] = jnp.where(band, 0.0, NEG_INF)
        sample_step()
        strided_pattern(DILATIONS[1], 0, n_groups, *stats[0:3])

    @pl.when(phase == 1)
    def _():
        sample_step()
        strided_pattern(DILATIONS[2], 0, half, *stats[3:6])

    @pl.when(phase == 2)
    def _():
        sample_step()
        strided_pattern(DILATIONS[2], half, half, *stats[3:6])

    @pl.when(phase == 3)
    def _():
        def body(g, carry):
            rows = pl.ds(pl.multiple_of(g * BAND, BAND), BAND)
            acc, den, m, carry = group(rows, g, carry)
            parts = [(acc, den, m)] + [(stats[3 * t][rows, :], stats[3 * t + 1][rows, :],
                                        stats[3 * t + 2][rows, :]) for t in range(2)]
            m_max = jnp.maximum(jnp.maximum(parts[0][2], parts[1][2]), parts[2][2])
            e = [jnp.exp2(pt[2] - m_max) for pt in parts]
            num = sum(ei * pt[0] for ei, pt in zip(e, parts))
            tot = sum(ei * pt[1] for ei, pt in zip(e, parts))
            o_ref[rows, :] = (num / tot * sga_ref[rows, :].astype(F32)).astype(BF16)
            return carry

        sample_step()
        lax.fori_loop(0, n_groups, body, zero_block, unroll=ATTN_UNROLL)


def _prompt_attn(q, k, v, sga, sample):
    b, s, _ = q.shape
    grid = (b, D_ATT // HEAD_PAIR, ATTN_PHASES)
    s_ops, s_in, s_out, s_shape = _sample_attn_operands(sample, grid)
    spec = pl.BlockSpec((None, s, HEAD_PAIR), lambda i, j, c: (i, 0, j))
    return pl.pallas_call(
        _prompt_attn_kernel,
        grid=grid,
        in_specs=[spec, spec, spec, spec] + s_in,
        out_specs=[spec, s_out],
        out_shape=[jax.ShapeDtypeStruct((b, s, D_ATT), BF16), s_shape],
        scratch_shapes=[pltpu.VMEM((2, BAND, 2 * BAND), F32)]
                       + [pltpu.VMEM((s, HEAD_PAIR), F32) for _ in range(6)],
        compiler_params=pltpu.CompilerParams(
            dimension_semantics=("arbitrary", "arbitrary", "arbitrary"),
            vmem_limit_bytes=VMEM_LIMIT),
        name="prompt_attn",
    )(q, k, v, sga, *s_ops)


def _finish_kernel(ya_ref, hl_ref, x_ref, wo_ref, ln_ref, o_ref):
    sub = (_dot(ya_ref[...], wo_ref[0:D_ATT, :])
           + _dot(hl_ref[...], wo_ref[D_ATT:D_ATT + D_LRU, :]))
    z = ALPHA * x_ref[...] + sub
    mu = jnp.mean(z, axis=-1, keepdims=True)
    zc = z - mu
    var = jnp.mean(zc * zc, axis=-1, keepdims=True)
    o_ref[...] = zc * lax.rsqrt(var + LN_EPS) * ln_ref[0:1, :] + ln_ref[1:2, :]


def _finish(ya, hl, x, w_o, ln, name):
    n = x.shape[0]
    rows = min(n, FINISH_ROWS)
    row_spec = lambda width: pl.BlockSpec((rows, width), lambda i: (i, 0))
    const = lambda shape: pl.BlockSpec(shape, lambda i: (0, 0),
                                       pipeline_mode=pl.Buffered(1))
    return pl.pallas_call(
        _finish_kernel,
        grid=(n // rows,),
        in_specs=[row_spec(D_ATT), row_spec(D_LRU), row_spec(D_MODEL),
                  const((D_ATT + D_LRU, D_MODEL)), const((2, D_MODEL))],
        out_specs=row_spec(D_MODEL),
        out_shape=jax.ShapeDtypeStruct((n, D_MODEL), F32),
        compiler_params=pltpu.CompilerParams(
            dimension_semantics=("arbitrary",), vmem_limit_bytes=VMEM_LIMIT),
        name=name,
    )(ya, hl, x, w_o, ln)


def _sample_proj_kernel(x_ref, sc_ref, sh_ref, w_ref, wg_ref, vec_ref,
                        q_ref, k_ref, v_ref, hl_ref, sga_ref, conv_ref, hlast_ref):
    t_steps, nb, _ = x_ref.shape
    xb = x_ref[...].reshape(t_steps * nb, D_MODEL).astype(BF16)
    split = lambda y: y.reshape(t_steps, nb, y.shape[-1])
    q_ref[...] = split(_dot(xb, w_ref[:, C_Q:C_K]))
    k_ref[...] = split(_dot(xb, w_ref[:, C_K:C_V]))
    v_ref[...] = split(_dot(xb, w_ref[:, C_V:C_GA]))
    sga_ref[...] = split(_silu(_dot(xb, w_ref[:, C_GA:C_XL])))

    xl = _dot(xb, w_ref[:, C_XL:C_GL])
    xc = [sc_ref[i] for i in range(CONV_WIDTH - 1)]
    xc += [xl[t * nb:(t + 1) * nb, :] for t in range(t_steps)]
    u = jnp.concatenate(
        [vec_ref[4:5, :] + sum(xc[t + w] * vec_ref[w:w + 1, :] for w in range(CONV_WIDTH))
         for t in range(t_steps)], axis=0)
    for i in range(CONV_WIDTH - 1):
        conv_ref[i] = xc[t_steps + i]

    a, bx = _lru_gates(u, wg_ref, vec_ref)
    sgl = _silu(_dot(xb, w_ref[:, C_GL:D_IN]))
    h = sh_ref[...]
    for t in range(t_steps):
        rows = slice(t * nb, (t + 1) * nb)
        h = a[rows, :] * h + bx[rows, :]
        hl_ref[t] = (h * sgl[rows, :]).astype(BF16)
    hlast_ref[...] = h


def _sample_proj(xs_t, sc_t, sh, w_bf, w_gate, vecs):
    t_steps, nb, _ = xs_t.shape
    full = lambda shape: pl.BlockSpec(shape, lambda i: (0,) * len(shape))
    tok = lambda dt: jax.ShapeDtypeStruct((t_steps, nb, D_ATT), dt)
    return pl.pallas_call(
        _sample_proj_kernel,
        grid=(1,),
        in_specs=[full(xs_t.shape), full(sc_t.shape), full(sh.shape),
                  full(w_bf.shape), full(w_gate.shape), full(vecs.shape)],
        out_specs=[full((t_steps, nb, D_ATT))] * 5
                  + [full(sc_t.shape), full(sh.shape)],
        out_shape=[tok(F32), tok(F32), tok(F32), tok(BF16), tok(F32),
                   jax.ShapeDtypeStruct(sc_t.shape, F32),
                   jax.ShapeDtypeStruct(sh.shape, F32)],
        compiler_params=pltpu.CompilerParams(
            dimension_semantics=("arbitrary",), vmem_limit_bytes=VMEM_LIMIT),
        name="sample_proj_lru",
    )(xs_t, sc_t, sh, w_bf, w_gate, vecs)


def kernel(x_prompt, x_sample, cache_k, cache_v, state_conv, state_h, w_in, conv_w, conv_b, w_ra, b_ra, w_ri, b_ri, lru_lambda, w_out, ln_g, ln_b):
    assert w_in.shape[0] == 1, "single layer"
    b, s, _ = x_prompt.shape
    nb, t_steps, _ = x_sample.shape
    w_buf = cache_k.shape[2]

    col_scale = jnp.concatenate([jnp.full((D_ATT,), HEAD_DIM ** -0.5, F32),
                                 jnp.ones((D_IN - D_ATT,), F32)])
    w_bf = (w_in[0] * col_scale).astype(BF16)
    eye = jnp.eye(LRU_BLOCKS, dtype=F32)
    block_diag = lambda wb: (eye[:, None, :, None] * wb[:, :, None, :]).reshape(D_LRU, D_LRU)
    w_gate = jnp.concatenate([block_diag(w_ra[0]), block_diag(w_ri[0])], axis=1).astype(BF16)
    vecs = jnp.concatenate([conv_w[0], conv_b, b_ra, b_ri, lru_lambda], axis=0)
    w_o = w_out[0].astype(BF16)
    ln = jnp.concatenate([ln_g, ln_b], axis=0)
    heads = lambda a: a.reshape(a.shape[:-1] + (N_HEADS, HEAD_DIM))

    xs_t = jnp.transpose(x_sample, (1, 0, 2))
    sc_t = jnp.transpose(state_conv[0], (1, 0, 2))
    q_s, k_s, v_s, hl_s, sga_s, conv_s, h_s = _sample_proj(
        xs_t, sc_t, state_h[0], w_bf, w_gate, vecs)
    kt = jnp.transpose(cache_k[0], (0, 2, 3, 1)).reshape(nb, D_ATT, w_buf)
    vt = jnp.transpose(cache_v[0], (0, 2, 3, 1)).reshape(nb, D_ATT, w_buf)
    sample = (q_s, k_s, v_s, sga_s, kt, vt)

    n_proj = b * (s // PROJ_ROWS) * PROJ_PHASES
    n_attn = b * (D_ATT // HEAD_PAIR) * ATTN_PHASES
    assert n_proj + n_attn == nb
    n_keep = min(w_buf, s)
    q, k, v, hl, sga, conv_p, h_p, kt_p, vt_p, ya_s0 = _proj_lru(
        x_prompt, w_bf, w_gate, vecs, sample + (0,), n_keep)
    ya, ya_s1 = _prompt_attn(q, k, v, sga, sample + (n_proj,))
    y_p = _finish(ya.reshape(b * s, D_ATT), hl.reshape(b * s, D_LRU),
                  x_prompt.reshape(b * s, D_MODEL), w_o, ln, "prompt_finish")
    window = lambda a: jnp.transpose(a.reshape(b, N_HEADS, HEAD_DIM, n_keep), (0, 3, 1, 2))[None]
    k_p, v_p = window(kt_p), window(vt_p)
    ya_s = jnp.concatenate([ya_s0, ya_s1], axis=1)
    y_s = _finish(ya_s.reshape(t_steps * nb, D_ATT).astype(BF16),
                  hl_s.reshape(t_steps * nb, D_LRU),
                  xs_t.reshape(t_steps * nb, D_MODEL), w_o, ln, "sample_finish")
    to_batch_major = lambda a: jnp.transpose(a, (1, 0, 2))

    return (y_p.reshape(b, s, D_MODEL),
            to_batch_major(y_s.reshape(t_steps, nb, D_MODEL)),
            k_p, v_p,
            conv_p[None], h_p.reshape(1, b, D_LRU),
            heads(to_batch_major(k_s))[None], heads(to_batch_major(v_s))[None],
            to_batch_major(conv_s)[None], h_s[None])
```

```python
import functools

import numpy as np
import jax
import jax.numpy as jnp
from jax import lax
from jax.experimental import pallas as pl
from jax.experimental.pallas import tpu as pltpu

F32 = jnp.float32
BF16 = jnp.bfloat16

D_MODEL = 1024
N_HEADS = 8
HEAD_DIM = 64
D_ATT = N_HEADS * HEAD_DIM
D_LRU = 512
LRU_BLOCKS = 8
D_IN = 4 * D_ATT + 2 * D_LRU
CONV_WIDTH = 4
LRU_C = 8.0
DILATIONS = (1, 4, 16)
N_DIL_KEYS = 128
BAND = 128
ALPHA = 2.0 ** 0.25
LN_EPS = 1e-5
NEG_INF = -1e30
LOG2_E = 1.4426950408889634

C_Q, C_K, C_V, C_GA, C_XL, C_GL = (0, D_ATT, 2 * D_ATT, 3 * D_ATT, 4 * D_ATT,
                                   4 * D_ATT + D_LRU)

VMEM_LIMIT = 56 * 1024 * 1024
PROJ_ROWS = 512
FINISH_ROWS = 1024
PROJ_PHASES = 2
HEAD_PAIR = 2 * HEAD_DIM
SUBLANES = 8
ATTN_UNROLL = 16
ATTN_PHASES = 4


def _dot(a, b):
    return jnp.dot(a, b, preferred_element_type=F32)


def _dot_nt(a, b):
    return lax.dot_general(a, b, (((1,), (1,)), ((), ())),
                           preferred_element_type=F32)


def _sigmoid(x):
    return 0.5 * jnp.tanh(0.5 * x) + 0.5


def _silu(x):
    return x * _sigmoid(x)


def _softplus(x):
    return jnp.maximum(x, 0.0) + jnp.log1p(jnp.exp(-jnp.abs(x)))


def _lru_gates(u, wg_ref, vec_ref):
    g = _dot(u.astype(BF16), wg_ref[...])
    tr = jnp.tanh(g[:, :D_LRU] + 0.5 * vec_ref[5:6, :])
    ig = 0.5 * jnp.tanh(g[:, D_LRU:] + 0.5 * vec_ref[6:7, :]) + 0.5
    half_c = (-0.5 * LRU_C) * _softplus(-vec_ref[7:8, :])
    log_a = half_c * tr + half_c
    a = jnp.exp(log_a)
    x = jnp.tanh(-log_a) * (1.0 + a * a)
    bx = jnp.where(x > 0.0, x * lax.rsqrt(x), 0.0) * ig * u
    return a, bx


def _sample_attn_step(q_ref, kn_ref, vn_ref, sga_ref, cnt_ref, kt_ref, vt_ref, o_ref):
    t_steps = q_ref.shape[0]
    step = ((pl.program_id(0) * pl.num_programs(1) + pl.program_id(1)) * pl.num_programs(2)
            + pl.program_id(2))
    mine = pl.ds(step % SUBLANES, 1)
    n_rows = t_steps * N_HEADS
    sub = lax.broadcasted_iota(jnp.int32, (N_HEADS, D_ATT), 0)
    lane_head = lax.broadcasted_iota(jnp.int32, (N_HEADS, D_ATT), 1) // HEAD_DIM
    own = sub == lane_head
    qbd = jnp.concatenate(
        [jnp.where(own, jnp.broadcast_to(q_ref[t, mine, :], (N_HEADS, D_ATT)), 0.0)
         for t in range(t_steps)], axis=0)

    cnt = cnt_ref[...]
    s = jnp.where(cnt > 0.0, _dot(qbd.astype(BF16), kt_ref[...].astype(BF16)), NEG_INF)
    m = jnp.max(s, axis=-1, keepdims=True)

    row_t = lax.broadcasted_iota(jnp.int32, (n_rows, 1), 0) // N_HEADS
    s_new, c_new = [], []
    for t2 in range(t_steps):
        c = jnp.where(row_t == t2, float(len(DILATIONS)),
                      jnp.where(row_t > t2, 1.0, 0.0))
        sn = jnp.sum(qbd * kn_ref[t2, mine, :], axis=-1, keepdims=True)
        sn = jnp.where(c > 0.0, sn, NEG_INF)
        m = jnp.maximum(m, sn)
        s_new.append(sn)
        c_new.append(c)

    p = jnp.exp(s - m) * cnt
    den = jnp.sum(p, axis=-1, keepdims=True)
    acc = _dot_nt(p.astype(BF16), vt_ref[...].astype(BF16))
    for t2 in range(t_steps):
        pn = jnp.exp(s_new[t2] - m) * c_new[t2]
        den = den + pn
        acc = acc + pn * vn_ref[t2, mine, :]
    y = acc / den
    for t in range(t_steps):
        yt = jnp.sum(jnp.where(own, y[t * N_HEADS:(t + 1) * N_HEADS, :], 0.0),
                     axis=0, keepdims=True)
        o_ref[t, mine, :] = yt * sga_ref[t, mine, :]


def _pattern_counts(t_steps, w_buf):
    delta = w_buf + np.arange(t_steps)[:, None] - np.arange(w_buf)[None, :]
    cnt = np.zeros((t_steps, w_buf), np.float32)
    for d in DILATIONS:
        cnt += (delta % d == 0) & (delta // d <= N_DIL_KEYS)
    return np.repeat(cnt, N_HEADS, axis=0)


def _sample_attn_operands(sample, grid):
    q_s, kn_s, vn_s, sga_s, kt, vt, first = sample
    t_steps = q_s.shape[0]
    w_buf = kt.shape[-1]
    n_local = grid[0] * grid[1] * grid[2]
    assert first % SUBLANES == 0 and n_local % SUBLANES == 0
    cnt = jnp.asarray(_pattern_counts(t_steps, w_buf))
    step = lambda i, j, c: (i * grid[1] + j) * grid[2] + c
    tok_block = (t_steps, SUBLANES, D_ATT)
    tok_in = pl.BlockSpec(tok_block, lambda *g: (0, (first + step(*g)) // SUBLANES, 0))
    cache = pl.BlockSpec((None, D_ATT, w_buf), lambda *g: (first + step(*g), 0, 0))
    cnt_spec = pl.BlockSpec(cnt.shape, lambda *g: (0, 0), pipeline_mode=pl.Buffered(1))
    operands = (q_s, kn_s, vn_s, sga_s, cnt, kt, vt)
    in_specs = [tok_in, tok_in, tok_in, tok_in, cnt_spec, cache, cache]
    out_spec = pl.BlockSpec(tok_block, lambda *g: (0, step(*g) // SUBLANES, 0))
    out_shape = jax.ShapeDtypeStruct((t_steps, n_local, D_ATT), F32)
    return operands, in_specs, out_spec, out_shape


def _proj_lru_kernel(first_kept, x_ref, w_ref, wg_ref, vec_ref,
                     sq_ref, skn_ref, svn_ref, ssga_ref, cnt_ref, kt_ref, vt_ref,
                     q_ref, k_ref, v_ref, hl_ref, sga_ref, conv_ref, hlast_ref,
                     kt_keep_ref, vt_keep_ref, sya_ref,
                     xb_ref, xc_ref, a_ref, b_ref, carry_ref):
    j = pl.program_id(1)
    phase = pl.program_id(2)
    rows = x_ref.shape[0]
    kept = j >= first_kept
    sample_step = functools.partial(_sample_attn_step, sq_ref, skn_ref, svn_ref, ssga_ref,
                                    cnt_ref, kt_ref, vt_ref, sya_ref)

    @pl.when(phase == 0)
    def _():
        @pl.when(j == 0)
        def _():
            xc_ref[...] = jnp.zeros((8, D_LRU), F32)
            carry_ref[...] = jnp.zeros((1, D_LRU), F32)

        sample_step()
        xb_ref[...] = x_ref[...].astype(BF16)
        xl = _dot(xb_ref[...], w_ref[:, C_XL:C_GL])
        tail = xc_ref[...]
        xc_ref[...] = xl[rows - 8:, :]
        row8 = lax.broadcasted_iota(jnp.int32, (8, D_LRU), 0)
        u = vec_ref[4:5, :] + xl * vec_ref[3:4, :]
        for s in range(1, CONV_WIDTH):
            rolled = pltpu.roll(xl, s, axis=0)
            head = jnp.where(row8 < s, pltpu.roll(tail, s, axis=0), rolled[0:8, :])
            u = u + jnp.concatenate([head, rolled[8:, :]], axis=0) * vec_ref[3 - s:4 - s, :]
        a, bx = _lru_gates(u, wg_ref, vec_ref)
        a_ref[...] = a
        b_ref[...] = bx
        q_ref[...] = _dot(xb_ref[...], w_ref[:, C_Q:C_K]) * LOG2_E
        k_new = _dot(xb_ref[...], w_ref[:, C_K:C_V])
        k_ref[...] = k_new

        @pl.when(kept)
        def _():
            kt_keep_ref[...] = k_new.T

    @pl.when(phase == 1)
    def _():
        sample_step()
        v_new = _dot(xb_ref[...], w_ref[:, C_V:C_GA])
        v_ref[...] = v_new
        sga_ref[...] = _silu(_dot(xb_ref[...], w_ref[:, C_GA:C_XL])).astype(BF16)
        sgl = _silu(_dot(xb_ref[...], w_ref[:, C_GL:D_IN]))

        row = lax.broadcasted_iota(jnp.int32, (8, D_LRU), 0)

        def group(g, carry):
            r0 = pl.multiple_of(g * 8, 8)
            av = a_ref[pl.ds(r0, 8), :]
            bv = b_ref[pl.ds(r0, 8), :]
            for s in (1, 2, 4):
                keep = row >= s
                bv = jnp.where(keep, av * pltpu.roll(bv, s, axis=0) + bv, bv)
                av = jnp.where(keep, av * pltpu.roll(av, s, axis=0), av)
            h8 = av * carry + bv
            b_ref[pl.ds(r0, 8), :] = h8
            return h8[7:8, :]

        carry = lax.fori_loop(0, rows // 8, group, carry_ref[...], unroll=True)
        carry_ref[...] = carry
        hl_ref[...] = (b_ref[...] * sgl).astype(BF16)

        @pl.when(kept)
        def _():
            vt_keep_ref[...] = v_new.T

        @pl.when(j == pl.num_programs(1) - 1)
        def _():
            conv_ref[...] = xc_ref[5:8, :]
            hlast_ref[...] = carry


def _proj_lru(x, w_bf, w_gate, vecs, sample, n_keep):
    b, s, _ = x.shape
    nt = s // PROJ_ROWS
    assert n_keep % PROJ_ROWS == 0
    first_kept = nt - n_keep // PROJ_ROWS
    keep_spec = pl.BlockSpec((None, D_ATT, PROJ_ROWS),
                             lambda i, j, c: (i, 0, jnp.maximum(j - first_kept, 0)))
    keep_shape = jax.ShapeDtypeStruct((b, D_ATT, n_keep), F32)
    grid = (b, nt, PROJ_PHASES)
    s_ops, s_in, s_out, s_shape = _sample_attn_operands(sample, grid)
    row_spec = lambda width: pl.BlockSpec((None, PROJ_ROWS, width), lambda i, j, c: (i, j, 0))
    const = lambda shape: pl.BlockSpec(shape, lambda i, j, c: (0,) * len(shape),
                                       pipeline_mode=pl.Buffered(1))
    per_batch = lambda n: pl.BlockSpec((None, n, D_LRU), lambda i, j, c: (i, 0, 0))
    tile = lambda dt: jax.ShapeDtypeStruct((b, s, D_ATT), dt)
    return pl.pallas_call(
        functools.partial(_proj_lru_kernel, first_kept),
        grid=grid,
        in_specs=[row_spec(D_MODEL), const((D_MODEL, D_IN)),
                  const((D_LRU, 2 * D_LRU)), const((8, D_LRU))] + s_in,
        out_specs=[row_spec(D_ATT), row_spec(D_ATT), row_spec(D_ATT),
                   row_spec(D_LRU), row_spec(D_ATT),
                   per_batch(CONV_WIDTH - 1), per_batch(1), keep_spec, keep_spec, s_out],
        out_shape=[tile(F32), tile(F32), tile(F32), tile(BF16), tile(BF16),
                   jax.ShapeDtypeStruct((b, CONV_WIDTH - 1, D_LRU), F32),
                   jax.ShapeDtypeStruct((b, 1, D_LRU), F32), keep_shape, keep_shape, s_shape],
        scratch_shapes=[pltpu.VMEM((PROJ_ROWS, D_MODEL), BF16),
                        pltpu.VMEM((8, D_LRU), F32),
                        pltpu.VMEM((PROJ_ROWS, D_LRU), F32),
                        pltpu.VMEM((PROJ_ROWS, D_LRU), F32),
                        pltpu.VMEM((1, D_LRU), F32)],
        compiler_params=pltpu.CompilerParams(
            dimension_semantics=("arbitrary", "arbitrary", "arbitrary"),
            vmem_limit_bytes=VMEM_LIMIT),
        name="prompt_proj_lru",
    )(x, w_bf, w_gate, vecs, *s_ops)


def _prompt_attn_kernel(q_ref, k_ref, v_ref, sga_ref,
                        sq_ref, skn_ref, svn_ref, ssga_ref, cnt_ref, kt_ref, vt_ref,
                        o_ref, sya_ref, bias_ref, *stats):
    seq = q_ref.shape[0]
    phase = pl.program_id(2)
    n_groups = seq // BAND
    head_a = lax.broadcasted_iota(jnp.int32, (BAND, HEAD_PAIR), 1) < HEAD_DIM
    one = jnp.ones((), BF16)

    sample_step = functools.partial(_sample_attn_step, sq_ref, skn_ref, svn_ref, ssga_ref,
                                    cnt_ref, kt_ref, vt_ref, sya_ref)

    def key_value_block(rows):
        v_bf = v_ref[rows, :].astype(BF16)
        v_ext = jnp.concatenate([jnp.where(head_a, v_bf, one),
                                 jnp.where(head_a, one, v_bf)], axis=1)
        return k_ref[rows, :].astype(BF16), v_ext

    def group(rows, blk, carry):
        k_prev, v_prev = carry
        k_cur, v_cur = key_value_block(rows)
        k2 = jnp.concatenate([k_prev, k_cur], axis=0)
        v4 = jnp.concatenate([v_prev, v_cur], axis=0)
        qf = q_ref[rows, :]
        q2 = jnp.concatenate([jnp.where(head_a, qf, 0.0),
                              jnp.where(head_a, 0.0, qf)], axis=0).astype(BF16)
        s = _dot_nt(q2, k2).reshape(2, BAND, 2 * BAND)
        s = s + bias_ref[jnp.minimum(blk, 1)][None]
        m = jnp.max(s, axis=-1, keepdims=True)
        p = jnp.exp2(s - m).reshape(2 * BAND, 2 * BAND).astype(BF16)
        o = _dot(p, v4)
        acc = jnp.where(head_a, o[:BAND, :HEAD_PAIR], o[BAND:, HEAD_PAIR:])
        den = jnp.where(head_a, o[:BAND, HEAD_PAIR:], o[BAND:, :HEAD_PAIR])
        return acc, den, jnp.where(head_a, m[0], m[1]), (k_cur, v_cur)

    zero_block = (jnp.zeros((BAND, HEAD_PAIR), BF16), jnp.zeros((BAND, 2 * HEAD_PAIR), BF16))

    def strided_pattern(d, first, count, acc_ref, l_ref, m_ref):
        span = BAND * d
        n_blk = seq // span

        def body(i, carry):
            g = first + i
            blk = g % n_blk
            rows = pl.ds(blk * span + g // n_blk, BAND, stride=d)
            acc, den, m, carry = group(rows, blk, carry)
            acc_ref[rows, :] = acc
            l_ref[rows, :] = den
            m_ref[rows, :] = m
            return carry

        lax.fori_loop(0, count, body, zero_block, unroll=True)

    half = n_groups // 2

    @pl.when(phase == 0)
    def _():
        row = lax.broadcasted_iota(jnp.int32, (BAND, 2 * BAND), 0)
        col = lax.broadcasted_iota(jnp.int32, (BAND, 2 * BAND), 1)
        band = (col >= row) & (col <= row + N_DIL_KEYS)
        bias_ref[0] = jnp.where(band & (col >= BAND), 0.0, NEG_INF)
        bias_ref[1] = jnp.where(band, 0.0, NEG_INF)
        sample_step()
        strided_pattern(DILATIONS[1], 0, n_groups, *stats[0:3])

    @pl.when(phase == 1)
    def _():
        sample_step()
        strided_pattern(DILATIONS[2], 0, half, *stats[3:6])

    @pl.when(phase == 2)
    def _():
        sample_step()
        strided_pattern(DILATIONS[2], half, half, *stats[3:6])

    @pl.when(phase == 3)
    def _():
        def body(g, carry):
            rows = pl.ds(pl.multiple_of(g * BAND, BAND), BAND)
            acc, den, m, carry = group(rows, g, carry)
            parts = [(acc, den, m)] + [(stats[3 * t][rows, :], stats[3 * t + 1][rows, :],
                                        stats[3 * t + 2][rows, :]) for t in range(2)]
            m_max = jnp.maximum(jnp.maximum(parts[0][2], parts[1][2]), parts[2][2])
            e = [jnp.exp2(pt[2] - m_max) for pt in parts]
            num = sum(ei * pt[0] for ei, pt in zip(e, parts))
            tot = sum(ei * pt[1] for ei, pt in zip(e, parts))
            o_ref[rows, :] = (num / tot * sga_ref[rows, :].astype(F32)).astype(BF16)
            return carry

        sample_step()
        lax.fori_loop(0, n_groups, body, zero_block, unroll=ATTN_UNROLL)


def _prompt_attn(q, k, v, sga, sample):
    b, s, _ = q.shape
    grid = (b, D_ATT // HEAD_PAIR, ATTN_PHASES)
    s_ops, s_in, s_out, s_shape = _sample_attn_operands(sample, grid)
    spec = pl.BlockSpec((None, s, HEAD_PAIR), lambda i, j, c: (i, 0, j))
    return pl.pallas_call(
        _prompt_attn_kernel,
        grid=grid,
        in_specs=[spec, spec, spec, spec] + s_in,
        out_specs=[spec, s_out],
        out_shape=[jax.ShapeDtypeStruct((b, s, D_ATT), BF16), s_shape],
        scratch_shapes=[pltpu.VMEM((2, BAND, 2 * BAND), F32)]
                       + [pltpu.VMEM((s, HEAD_PAIR), F32) for _ in range(6)],
        compiler_params=pltpu.CompilerParams(
            dimension_semantics=("arbitrary", "arbitrary", "arbitrary"),
            vmem_limit_bytes=VMEM_LIMIT),
        name="prompt_attn",
    )(q, k, v, sga, *s_ops)


def _finish_kernel(ya_ref, hl_ref, x_ref, wo_ref, ln_ref, o_ref):
    sub = (_dot(ya_ref[...], wo_ref[0:D_ATT, :])
           + _dot(hl_ref[...], wo_ref[D_ATT:D_ATT + D_LRU, :]))
    z = ALPHA * x_ref[...] + sub
    mu = jnp.mean(z, axis=-1, keepdims=True)
    zc = z - mu
    var = jnp.mean(zc * zc, axis=-1, keepdims=True)
    o_ref[...] = zc * lax.rsqrt(var + LN_EPS) * ln_ref[0:1, :] + ln_ref[1:2, :]


def _finish(ya, hl, x, w_o, ln, name):
    n = x.shape[0]
    rows = min(n, FINISH_ROWS)
    row_spec = lambda width: pl.BlockSpec((rows, width), lambda i: (i, 0))
    const = lambda shape: pl.BlockSpec(shape, lambda i: (0, 0),
                                       pipeline_mode=pl.Buffered(1))
    return pl.pallas_call(
        _finish_kernel,
        grid=(n // rows,),
        in_specs=[row_spec(D_ATT), row_spec(D_LRU), row_spec(D_MODEL),
                  const((D_ATT + D_LRU, D_MODEL)), const((2, D_MODEL))],
        out_specs=row_spec(D_MODEL),
        out_shape=jax.ShapeDtypeStruct((n, D_MODEL), F32),
        compiler_params=pltpu.CompilerParams(
            dimension_semantics=("arbitrary",), vmem_limit_bytes=VMEM_LIMIT),
        name=name,
    )(ya, hl, x, w_o, ln)


def _sample_proj_kernel(x_ref, sc_ref, sh_ref, w_ref, wg_ref, vec_ref,
                        q_ref, k_ref, v_ref, hl_ref, sga_ref, conv_ref, hlast_ref):
    t_steps, nb, _ = x_ref.shape
    xb = x_ref[...].reshape(t_steps * nb, D_MODEL).astype(BF16)
    split = lambda y: y.reshape(t_steps, nb, y.shape[-1])
    q_ref[...] = split(_dot(xb, w_ref[:, C_Q:C_K]))
    k_ref[...] = split(_dot(xb, w_ref[:, C_K:C_V]))
    v_ref[...] = split(_dot(xb, w_ref[:, C_V:C_GA]))
    sga_ref[...] = split(_silu(_dot(xb, w_ref[:, C_GA:C_XL])))

    xl = _dot(xb, w_ref[:, C_XL:C_GL])
    xc = [sc_ref[i] for i in range(CONV_WIDTH - 1)]
    xc += [xl[t * nb:(t + 1) * nb, :] for t in range(t_steps)]
    u = jnp.concatenate(
        [vec_ref[4:5, :] + sum(xc[t + w] * vec_ref[w:w + 1, :] for w in range(CONV_WIDTH))
         for t in range(t_steps)], axis=0)
    for i in range(CONV_WIDTH - 1):
        conv_ref[i] = xc[t_steps + i]

    a, bx = _lru_gates(u, wg_ref, vec_ref)
    sgl = _silu(_dot(xb, w_ref[:, C_GL:D_IN]))
    h = sh_ref[...]
    for t in range(t_steps):
        rows = slice(t * nb, (t + 1) * nb)
        h = a[rows, :] * h + bx[rows, :]
        hl_ref[t] = (h * sgl[rows, :]).astype(BF16)
    hlast_ref[...] = h


def _sample_proj(xs_t, sc_t, sh, w_bf, w_gate, vecs):
    t_steps, nb, _ = xs_t.shape
    full = lambda shape: pl.BlockSpec(shape, lambda i: (0,) * len(shape))
    tok = lambda dt: jax.ShapeDtypeStruct((t_steps, nb, D_ATT), dt)
    return pl.pallas_call(
        _sample_proj_kernel,
        grid=(1,),
        in_specs=[full(xs_t.shape), full(sc_t.shape), full(sh.shape),
                  full(w_bf.shape), full(w_gate.shape), full(vecs.shape)],
        out_specs=[full((t_steps, nb, D_ATT))] * 5
                  + [full(sc_t.shape), full(sh.shape)],
        out_shape=[tok(F32), tok(F32), tok(F32), tok(BF16), tok(F32),
                   jax.ShapeDtypeStruct(sc_t.shape, F32),
                   jax.ShapeDtypeStruct(sh.shape, F32)],
        compiler_params=pltpu.CompilerParams(
            dimension_semantics=("arbitrary",), vmem_limit_bytes=VMEM_LIMIT),
        name="sample_proj_lru",
    )(xs_t, sc_t, sh, w_bf, w_gate, vecs)


def kernel(x_prompt, x_sample, cache_k, cache_v, state_conv, state_h, w_in, conv_w, conv_b, w_ra, b_ra, w_ri, b_ri, lru_lambda, w_out, ln_g, ln_b):
    assert w_in.shape[0] == 1, "single layer"
    b, s, _ = x_prompt.shape
    nb, t_steps, _ = x_sample.shape
    w_buf = cache_k.shape[2]

    col_scale = jnp.concatenate([jnp.full((D_ATT,), HEAD_DIM ** -0.5, F32),
                                 jnp.ones((D_IN - D_ATT,), F32)])
    w_bf = (w_in[0] * col_scale).astype(BF16)
    eye = jnp.eye(LRU_BLOCKS, dtype=F32)
    block_diag = lambda wb: (eye[:, None, :, None] * wb[:, :, None, :]).reshape(D_LRU, D_LRU)
    w_gate = (0.5 * jnp.concatenate([block_diag(w_ra[0]), block_diag(w_ri[0])], axis=1)).astype(BF16)
    vecs = jnp.concatenate([conv_w[0], conv_b, b_ra, b_ri, lru_lambda], axis=0)
    w_o = w_out[0].astype(BF16)
    ln = jnp.concatenate([ln_g, ln_b], axis=0)
    heads = lambda a: a.reshape(a.shape[:-1] + (N_HEADS, HEAD_DIM))

    xs_t = jnp.transpose(x_sample, (1, 0, 2))
    sc_t = jnp.transpose(state_conv[0], (1, 0, 2))
    q_s, k_s, v_s, hl_s, sga_s, conv_s, h_s = _sample_proj(
        xs_t, sc_t, state_h[0], w_bf, w_gate, vecs)
    kt = jnp.transpose(cache_k[0], (0, 2, 3, 1)).reshape(nb, D_ATT, w_buf)
    vt = jnp.transpose(cache_v[0], (0, 2, 3, 1)).reshape(nb, D_ATT, w_buf)
    sample = (q_s, k_s, v_s, sga_s, kt, vt)

    n_proj = b * (s // PROJ_ROWS) * PROJ_PHASES
    n_attn = b * (D_ATT // HEAD_PAIR) * ATTN_PHASES
    assert n_proj + n_attn == nb
    n_keep = min(w_buf, s)
    q, k, v, hl, sga, conv_p, h_p, kt_p, vt_p, ya_s0 = _proj_lru(
        x_prompt, w_bf, w_gate, vecs, sample + (0,), n_keep)
    ya, ya_s1 = _prompt_attn(q, k, v, sga, sample + (n_proj,))
    y_p = _finish(ya.reshape(b * s, D_ATT), hl.reshape(b * s, D_LRU),
                  x_prompt.reshape(b * s, D_MODEL), w_o, ln, "prompt_finish")
    window = lambda a: jnp.transpose(a.reshape(b, N_HEADS, HEAD_DIM, n_keep), (0, 3, 1, 2))[None]
    k_p, v_p = window(kt_p), window(vt_p)
    ya_s = jnp.concatenate([ya_s0, ya_s1], axis=1)
    y_s = _finish(ya_s.reshape(t_steps * nb, D_ATT).astype(BF16),
                  hl_s.reshape(t_steps * nb, D_LRU),
                  xs_t.reshape(t_steps * nb, D_MODEL), w_o, ln, "sample_finish")
    to_batch_major = lambda a: jnp.transpose(a, (1, 0, 2))

    return (y_p.reshape(b, s, D_MODEL),
            to_batch_major(y_s.reshape(t_steps, nb, D_MODEL)),
            k_p, v_p,
            conv_p[None], h_p.reshape(1, b, D_LRU),
            heads(to_batch_major(k_s))[None], heads(to_batch_major(v_s))[None],
            to_batch_major(conv_s)[None], h_s[None])
```

```python
import functools

import numpy as np
import jax
import jax.numpy as jnp
from jax import lax
from jax.experimental import pallas as pl
from jax.experimental.pallas import tpu as pltpu

F32 = jnp.float32
BF16 = jnp.bfloat16

D_MODEL = 1024
N_HEADS = 8
HEAD_DIM = 64
D_ATT = N_HEADS * HEAD_DIM
D_LRU = 512
LRU_BLOCKS = 8
D_IN = 4 * D_ATT + 2 * D_LRU
CONV_WIDTH = 4
LRU_C = 8.0
DILATIONS = (1, 4, 16)
N_DIL_KEYS = 128
BAND = 128
ALPHA = 2.0 ** 0.25
LN_EPS = 1e-5
NEG_INF = -1e30
LOG2_E = 1.4426950408889634

C_Q, C_K, C_V, C_GA, C_XL, C_GL = (0, D_ATT, 2 * D_ATT, 3 * D_ATT, 4 * D_ATT,
                                   4 * D_ATT + D_LRU)

VMEM_LIMIT = 60 * 1024 * 1024
PROJ_ROWS = 512
FINISH_ROWS = 1024
PROJ_PHASES = 2
HEAD_PAIR = 2 * HEAD_DIM
SUBLANES = 8
CACHE_SLOTS = 3
ATTN_UNROLL = 16
ATTN_PHASES = 4


def _dot(a, b):
    return jnp.dot(a, b, preferred_element_type=F32)


def _dot_nt(a, b):
    return lax.dot_general(a, b, (((1,), (1,)), ((), ())),
                           preferred_element_type=F32)


def _sigmoid(x):
    return 0.5 * jnp.tanh(0.5 * x) + 0.5


def _silu(x):
    return x * _sigmoid(x)


def _softplus(x):
    return jnp.maximum(x, 0.0) + jnp.log1p(jnp.exp(-jnp.abs(x)))


def _lru_gates(u, wg_ref, vec_ref):
    g = _dot(u.astype(BF16), wg_ref[...])
    tr = jnp.tanh(g[:, :D_LRU] + 0.5 * vec_ref[5:6, :])
    ig = 0.5 * jnp.tanh(g[:, D_LRU:] + 0.5 * vec_ref[6:7, :]) + 0.5
    half_c = (-0.5 * LRU_C) * _softplus(-vec_ref[7:8, :])
    log_a = half_c * tr + half_c
    a = jnp.exp(log_a)
    x = jnp.tanh(-log_a) * (1.0 + a * a)
    bx = jnp.where(x > 0.0, x * lax.rsqrt(x), 0.0) * ig * u
    return a, bx


def _grid_step():
    return ((pl.program_id(0) * pl.num_programs(1) + pl.program_id(1)) * pl.num_programs(2)
            + pl.program_id(2))


def _cache_copies(first, step, cache_hbm, ring, sems):
    slot = step % CACHE_SLOTS
    return [pltpu.make_async_copy(src.at[first + step], dst.at[slot], sems.at[which, slot])
            for which, (src, dst) in enumerate(zip(cache_hbm, ring))]


def _cache_ring_advance(first, n_steps, cache_hbm, ring, sems):
    step = _grid_step()
    lookahead = CACHE_SLOTS - 1

    @pl.when(step == 0)
    def _():
        for early in range(lookahead):
            for copy in _cache_copies(first, early, cache_hbm, ring, sems):
                copy.start()

    @pl.when(step + lookahead < n_steps)
    def _():
        for copy in _cache_copies(first, step + lookahead, cache_hbm, ring, sems):
            copy.start()


def _sample_attn_step(first, q_ref, kn_ref, vn_ref, sga_ref, cnt_ref, cache_hbm, ring, sems, o_ref):
    t_steps = q_ref.shape[0]
    step = _grid_step()
    mine = pl.ds(step % SUBLANES, 1)
    for copy in _cache_copies(first, step, cache_hbm, ring, sems):
        copy.wait()
    kt_ref, vt_ref = (buf.at[step % CACHE_SLOTS] for buf in ring)
    n_rows = t_steps * N_HEADS
    sub = lax.broadcasted_iota(jnp.int32, (N_HEADS, D_ATT), 0)
    lane_head = lax.broadcasted_iota(jnp.int32, (N_HEADS, D_ATT), 1) // HEAD_DIM
    own = sub == lane_head
    qbd = jnp.concatenate(
        [jnp.where(own, jnp.broadcast_to(q_ref[t, mine, :], (N_HEADS, D_ATT)), 0.0)
         for t in range(t_steps)], axis=0)

    cnt = cnt_ref[...]
    s = jnp.where(cnt > 0.0, _dot(qbd.astype(BF16), kt_ref[...].astype(BF16)), NEG_INF)
    m = jnp.max(s, axis=-1, keepdims=True)

    row_t = lax.broadcasted_iota(jnp.int32, (n_rows, 1), 0) // N_HEADS
    s_new, c_new = [], []
    for t2 in range(t_steps):
        c = jnp.where(row_t == t2, float(len(DILATIONS)),
                      jnp.where(row_t > t2, 1.0, 0.0))
        sn = jnp.sum(qbd * kn_ref[t2, mine, :], axis=-1, keepdims=True)
        sn = jnp.where(c > 0.0, sn, NEG_INF)
        m = jnp.maximum(m, sn)
        s_new.append(sn)
        c_new.append(c)

    p = jnp.exp(s - m) * cnt
    den = jnp.sum(p, axis=-1, keepdims=True)
    acc = _dot_nt(p.astype(BF16), vt_ref[...].astype(BF16))
    for t2 in range(t_steps):
        pn = jnp.exp(s_new[t2] - m) * c_new[t2]
        den = den + pn
        acc = acc + pn * vn_ref[t2, mine, :]
    y = acc / den
    for t in range(t_steps):
        yt = jnp.sum(jnp.where(own, y[t * N_HEADS:(t + 1) * N_HEADS, :], 0.0),
                     axis=0, keepdims=True)
        o_ref[t, mine, :] = yt * sga_ref[t, mine, :]


def _pattern_counts(t_steps, w_buf):
    delta = w_buf + np.arange(t_steps)[:, None] - np.arange(w_buf)[None, :]
    cnt = np.zeros((t_steps, w_buf), np.float32)
    for d in DILATIONS:
        cnt += (delta % d == 0) & (delta // d <= N_DIL_KEYS)
    return np.repeat(cnt, N_HEADS, axis=0)


def _sample_attn_operands(sample, grid):
    q_s, kn_s, vn_s, sga_s, kt, vt, first = sample
    t_steps = q_s.shape[0]
    w_buf = kt.shape[-1]
    n_local = grid[0] * grid[1] * grid[2]
    assert first % SUBLANES == 0 and n_local % SUBLANES == 0 and n_local >= CACHE_SLOTS
    cnt = jnp.asarray(_pattern_counts(t_steps, w_buf))
    step = lambda i, j, c: (i * grid[1] + j) * grid[2] + c
    tok_block = (t_steps, SUBLANES, D_ATT)
    tok_in = pl.BlockSpec(tok_block, lambda *g: (0, (first + step(*g)) // SUBLANES, 0))
    in_hbm = pl.BlockSpec(memory_space=pl.ANY)
    cnt_spec = pl.BlockSpec(cnt.shape, lambda *g: (0, 0), pipeline_mode=pl.Buffered(1))
    operands = (q_s, kn_s, vn_s, sga_s, cnt, kt, vt)
    in_specs = [tok_in, tok_in, tok_in, tok_in, cnt_spec, in_hbm, in_hbm]
    out_spec = pl.BlockSpec(tok_block, lambda *g: (0, step(*g) // SUBLANES, 0))
    out_shape = jax.ShapeDtypeStruct((t_steps, n_local, D_ATT), F32)
    scratch = [pltpu.VMEM((CACHE_SLOTS, D_ATT, w_buf), F32),
               pltpu.VMEM((CACHE_SLOTS, D_ATT, w_buf), F32),
               pltpu.SemaphoreType.DMA((2, CACHE_SLOTS))]
    return operands, in_specs, out_spec, out_shape, scratch


def _proj_lru_kernel(first_kept, first_sample, n_steps, x_ref, w_ref, wg_ref, vec_ref,
                     sq_ref, skn_ref, svn_ref, ssga_ref, cnt_ref, kt_hbm, vt_hbm,
                     q_ref, k_ref, v_ref, hl_ref, sga_ref, conv_ref, hlast_ref,
                     kt_keep_ref, vt_keep_ref, sya_ref,
                     xb_ref, xc_ref, a_ref, b_ref, carry_ref, kt_ring, vt_ring, cache_sems):
    j = pl.program_id(1)
    phase = pl.program_id(2)
    rows = x_ref.shape[0]
    kept = j >= first_kept
    cache = ((kt_hbm, vt_hbm), (kt_ring, vt_ring), cache_sems)
    _cache_ring_advance(first_sample, n_steps, *cache)
    sample_step = functools.partial(_sample_attn_step, first_sample, sq_ref, skn_ref, svn_ref,
                                    ssga_ref, cnt_ref, *cache, sya_ref)

    @pl.when(phase == 0)
    def _():
        @pl.when(j == 0)
        def _():
            xc_ref[...] = jnp.zeros((8, D_LRU), F32)
            carry_ref[...] = jnp.zeros((1, D_LRU), F32)

        sample_step()
        xb_ref[...] = x_ref[...].astype(BF16)
        xl = _dot(xb_ref[...], w_ref[:, C_XL:C_GL])
        tail = xc_ref[...]
        xc_ref[...] = xl[rows - 8:, :]
        row8 = lax.broadcasted_iota(jnp.int32, (8, D_LRU), 0)
        u = vec_ref[4:5, :] + xl * vec_ref[3:4, :]
        for s in range(1, CONV_WIDTH):
            rolled = pltpu.roll(xl, s, axis=0)
            head = jnp.where(row8 < s, pltpu.roll(tail, s, axis=0), rolled[0:8, :])
            u = u + jnp.concatenate([head, rolled[8:, :]], axis=0) * vec_ref[3 - s:4 - s, :]
        a, bx = _lru_gates(u, wg_ref, vec_ref)
        a_ref[...] = a
        b_ref[...] = bx
        q_ref[...] = _dot(xb_ref[...], w_ref[:, C_Q:C_K]) * LOG2_E
        k_new = _dot(xb_ref[...], w_ref[:, C_K:C_V])
        k_ref[...] = k_new

        @pl.when(kept)
        def _():
            kt_keep_ref[...] = k_new.T

    @pl.when(phase == 1)
    def _():
        sample_step()
        v_new = _dot(xb_ref[...], w_ref[:, C_V:C_GA])
        v_ref[...] = v_new
        sga_ref[...] = _silu(_dot(xb_ref[...], w_ref[:, C_GA:C_XL])).astype(BF16)
        sgl = _silu(_dot(xb_ref[...], w_ref[:, C_GL:D_IN]))

        row = lax.broadcasted_iota(jnp.int32, (8, D_LRU), 0)

        def group(g, carry):
            r0 = pl.multiple_of(g * 8, 8)
            av = a_ref[pl.ds(r0, 8), :]
            bv = b_ref[pl.ds(r0, 8), :]
            for s in (1, 2, 4):
                keep = row >= s
                bv = jnp.where(keep, av * pltpu.roll(bv, s, axis=0) + bv, bv)
                av = jnp.where(keep, av * pltpu.roll(av, s, axis=0), av)
            h8 = av * carry + bv
            b_ref[pl.ds(r0, 8), :] = h8
            return h8[7:8, :]

        carry = lax.fori_loop(0, rows // 8, group, carry_ref[...], unroll=True)
        carry_ref[...] = carry
        hl_ref[...] = (b_ref[...] * sgl).astype(BF16)

        @pl.when(kept)
        def _():
            vt_keep_ref[...] = v_new.T

        @pl.when(j == pl.num_programs(1) - 1)
        def _():
            conv_ref[...] = xc_ref[5:8, :]
            hlast_ref[...] = carry


def _proj_lru(x, w_bf, w_gate, vecs, sample, n_keep):
    b, s, _ = x.shape
    nt = s // PROJ_ROWS
    assert n_keep % PROJ_ROWS == 0
    first_kept = nt - n_keep // PROJ_ROWS
    keep_spec = pl.BlockSpec((None, D_ATT, PROJ_ROWS),
                             lambda i, j, c: (i, 0, jnp.maximum(j - first_kept, 0)))
    keep_shape = jax.ShapeDtypeStruct((b, D_ATT, n_keep), F32)
    grid = (b, nt, PROJ_PHASES)
    s_ops, s_in, s_out, s_shape, s_scratch = _sample_attn_operands(sample, grid)
    row_spec = lambda width: pl.BlockSpec((None, PROJ_ROWS, width), lambda i, j, c: (i, j, 0))
    const = lambda shape: pl.BlockSpec(shape, lambda i, j, c: (0,) * len(shape),
                                       pipeline_mode=pl.Buffered(1))
    per_batch = lambda n: pl.BlockSpec((None, n, D_LRU), lambda i, j, c: (i, 0, 0))
    tile = lambda dt: jax.ShapeDtypeStruct((b, s, D_ATT), dt)
    return pl.pallas_call(
        functools.partial(_proj_lru_kernel, first_kept, sample[-1], b * nt * PROJ_PHASES),
        grid=grid,
        in_specs=[row_spec(D_MODEL), const((D_MODEL, D_IN)),
                  const((D_LRU, 2 * D_LRU)), const((8, D_LRU))] + s_in,
        out_specs=[row_spec(D_ATT), row_spec(D_ATT), row_spec(D_ATT),
                   row_spec(D_LRU), row_spec(D_ATT),
                   per_batch(CONV_WIDTH - 1), per_batch(1), keep_spec, keep_spec, s_out],
        out_shape=[tile(F32), tile(F32), tile(F32), tile(BF16), tile(BF16),
                   jax.ShapeDtypeStruct((b, CONV_WIDTH - 1, D_LRU), F32),
                   jax.ShapeDtypeStruct((b, 1, D_LRU), F32), keep_shape, keep_shape, s_shape],
        scratch_shapes=[pltpu.VMEM((PROJ_ROWS, D_MODEL), BF16),
                        pltpu.VMEM((8, D_LRU), F32),
                        pltpu.VMEM((PROJ_ROWS, D_LRU), F32),
                        pltpu.VMEM((PROJ_ROWS, D_LRU), F32),
                        pltpu.VMEM((1, D_LRU), F32)] + s_scratch,
        compiler_params=pltpu.CompilerParams(
            dimension_semantics=("arbitrary", "arbitrary", "arbitrary"),
            vmem_limit_bytes=VMEM_LIMIT),
        name="prompt_proj_lru",
    )(x, w_bf, w_gate, vecs, *s_ops)


def _prompt_attn_kernel(first_sample, n_steps, q_ref, k_ref, v_ref, sga_ref,
                        sq_ref, skn_ref, svn_ref, ssga_ref, cnt_ref, kt_hbm, vt_hbm,
                        o_ref, sya_ref, bias_ref, *scratch):
    seq = q_ref.shape[0]
    phase = pl.program_id(2)
    n_groups = seq // BAND
    head_a = lax.broadcasted_iota(jnp.int32, (BAND, HEAD_PAIR), 1) < HEAD_DIM
    one = jnp.ones((), BF16)

    stats, cache = scratch[:6], ((kt_hbm, vt_hbm), scratch[6:8], scratch[8])
    _cache_ring_advance(first_sample, n_steps, *cache)
    sample_step = functools.partial(_sample_attn_step, first_sample, sq_ref, skn_ref, svn_ref,
                                    ssga_ref, cnt_ref, *cache, sya_ref)

    def key_value_block(rows):
        v_bf = v_ref[rows, :].astype(BF16)
        v_ext = jnp.concatenate([jnp.where(head_a, v_bf, one),
                                 jnp.where(head_a, one, v_bf)], axis=1)
        return k_ref[rows, :].astype(BF16), v_ext

    def group(rows, blk, carry):
        k_prev, v_prev = carry
        k_cur, v_cur = key_value_block(rows)
        k2 = jnp.concatenate([k_prev, k_cur], axis=0)
        v4 = jnp.concatenate([v_prev, v_cur], axis=0)
        qf = q_ref[rows, :]
        q2 = jnp.concatenate([jnp.where(head_a, qf, 0.0),
                              jnp.where(head_a, 0.0, qf)], axis=0).astype(BF16)
        s = _dot_nt(q2, k2).reshape(2, BAND, 2 * BAND)
        s = s + bias_ref[jnp.minimum(blk, 1)][None]
        m = jnp.max(s, axis=-1, keepdims=True)
        p = jnp.exp2(s - m).reshape(2 * BAND, 2 * BAND).astype(BF16)
        o = _dot(p, v4)
        acc = jnp.where(head_a, o[:BAND, :HEAD_PAIR], o[BAND:, HEAD_PAIR:])
        den = jnp.where(head_a, o[:BAND, HEAD_PAIR:], o[BAND:, :HEAD_PAIR])
        return acc, den, jnp.where(head_a, m[0], m[1]), (k_cur, v_cur)

    zero_block = (jnp.zeros((BAND, HEAD_PAIR), BF16), jnp.zeros((BAND, 2 * HEAD_PAIR), BF16))

    def strided_pattern(d, first, count, acc_ref, l_ref, m_ref):
        span = BAND * d
        n_blk = seq // span

        def body(i, carry):
            g = first + i
            blk = g % n_blk
            rows = pl.ds(blk * span + g // n_blk, BAND, stride=d)
            acc, den, m, carry = group(rows, blk, carry)
            acc_ref[rows, :] = acc
            l_ref[rows, :] = den
            m_ref[rows, :] = m
            return carry

        lax.fori_loop(0, count, body, zero_block, unroll=True)

    half = n_groups // 2

    @pl.when(phase == 0)
    def _():
        row = lax.broadcasted_iota(jnp.int32, (BAND, 2 * BAND), 0)
        col = lax.broadcasted_iota(jnp.int32, (BAND, 2 * BAND), 1)
        band = (col >= row) & (col <= row + N_DIL_KEYS)
        bias_ref[0] = jnp.where(band & (col >= BAND), 0.0, NEG_INF)
        bias_ref[1] = jnp.where(band, 0.0, NEG_INF)
        sample_step()
        strided_pattern(DILATIONS[1], 0, n_groups, *stats[0:3])

    @pl.when(phase == 1)
    def _():
        sample_step()
        strided_pattern(DILATIONS[2], 0, half, *stats[3:6])

    @pl.when(phase == 2)
    def _():
        sample_step()
        strided_pattern(DILATIONS[2], half, half, *stats[3:6])

    @pl.when(phase == 3)
    def _():
        def body(g, carry):
            rows = pl.ds(pl.multiple_of(g * BAND, BAND), BAND)
            acc, den, m, carry = group(rows, g, carry)
            parts = [(acc, den, m)] + [(stats[3 * t][rows, :], stats[3 * t + 1][rows, :],
                                        stats[3 * t + 2][rows, :]) for t in range(2)]
            m_max = jnp.maximum(jnp.maximum(parts[0][2], parts[1][2]), parts[2][2])
            e = [jnp.exp2(pt[2] - m_max) for pt in parts]
            num = sum(ei * pt[0] for ei, pt in zip(e, parts))
            tot = sum(ei * pt[1] for ei, pt in zip(e, parts))
            o_ref[rows, :] = (num / tot * sga_ref[rows, :].astype(F32)).astype(BF16)
            return carry

        sample_step()
        lax.fori_loop(0, n_groups, body, zero_block, unroll=ATTN_UNROLL)


def _prompt_attn(q, k, v, sga, sample):
    b, s, _ = q.shape
    grid = (b, D_ATT // HEAD_PAIR, ATTN_PHASES)
    s_ops, s_in, s_out, s_shape, s_scratch = _sample_attn_operands(sample, grid)
    spec = pl.BlockSpec((None, s, HEAD_PAIR), lambda i, j, c: (i, 0, j))
    return pl.pallas_call(
        functools.partial(_prompt_attn_kernel, sample[-1], grid[0] * grid[1] * grid[2]),
        grid=grid,
        in_specs=[spec, spec, spec, spec] + s_in,
        out_specs=[spec, s_out],
        out_shape=[jax.ShapeDtypeStruct((b, s, D_ATT), BF16), s_shape],
        scratch_shapes=[pltpu.VMEM((2, BAND, 2 * BAND), F32)]
                       + [pltpu.VMEM((s, HEAD_PAIR), F32) for _ in range(6)] + s_scratch,
        compiler_params=pltpu.CompilerParams(
            dimension_semantics=("arbitrary", "arbitrary", "arbitrary"),
            vmem_limit_bytes=VMEM_LIMIT),
        name="prompt_attn",
    )(q, k, v, sga, *s_ops)


def _finish_kernel(ya_ref, hl_ref, x_ref, wo_ref, ln_ref, o_ref):
    sub = (_dot(ya_ref[...], wo_ref[0:D_ATT, :])
           + _dot(hl_ref[...], wo_ref[D_ATT:D_ATT + D_LRU, :]))
    z = ALPHA * x_ref[...] + sub
    mu = jnp.mean(z, axis=-1, keepdims=True)
    zc = z - mu
    var = jnp.mean(zc * zc, axis=-1, keepdims=True)
    o_ref[...] = zc * lax.rsqrt(var + LN_EPS) * ln_ref[0:1, :] + ln_ref[1:2, :]


def _finish(ya, hl, x, w_o, ln, name):
    n = x.shape[0]
    rows = min(n, FINISH_ROWS)
    row_spec = lambda width: pl.BlockSpec((rows, width), lambda i: (i, 0))
    const = lambda shape: pl.BlockSpec(shape, lambda i: (0, 0),
                                       pipeline_mode=pl.Buffered(1))
    return pl.pallas_call(
        _finish_kernel,
        grid=(n // rows,),
        in_specs=[row_spec(D_ATT), row_spec(D_LRU), row_spec(D_MODEL),
                  const((D_ATT + D_LRU, D_MODEL)), const((2, D_MODEL))],
        out_specs=row_spec(D_MODEL),
        out_shape=jax.ShapeDtypeStruct((n, D_MODEL), F32),
        compiler_params=pltpu.CompilerParams(
            dimension_semantics=("arbitrary",), vmem_limit_bytes=VMEM_LIMIT),
        name=name,
    )(ya, hl, x, w_o, ln)


def _sample_proj_kernel(x_ref, sc_ref, sh_ref, w_ref, wg_ref, vec_ref,
                        q_ref, k_ref, v_ref, hl_ref, sga_ref, conv_ref, hlast_ref):
    t_steps, nb, _ = x_ref.shape
    xb = x_ref[...].reshape(t_steps * nb, D_MODEL).astype(BF16)
    split = lambda y: y.reshape(t_steps, nb, y.shape[-1])
    q_ref[...] = split(_dot(xb, w_ref[:, C_Q:C_K]))
    k_ref[...] = split(_dot(xb, w_ref[:, C_K:C_V]))
    v_ref[...] = split(_dot(xb, w_ref[:, C_V:C_GA]))
    sga_ref[...] = split(_silu(_dot(xb, w_ref[:, C_GA:C_XL])))

    xl = _dot(xb, w_ref[:, C_XL:C_GL])
    xc = [sc_ref[i] for i in range(CONV_WIDTH - 1)]
    xc += [xl[t * nb:(t + 1) * nb, :] for t in range(t_steps)]
    u = jnp.concatenate(
        [vec_ref[4:5, :] + sum(xc[t + w] * vec_ref[w:w + 1, :] for w in range(CONV_WIDTH))
         for t in range(t_steps)], axis=0)
    for i in range(CONV_WIDTH - 1):
        conv_ref[i] = xc[t_steps + i]

    a, bx = _lru_gates(u, wg_ref, vec_ref)
    sgl = _silu(_dot(xb, w_ref[:, C_GL:D_IN]))
    h = sh_ref[...]
    for t in range(t_steps):
        rows = slice(t * nb, (t + 1) * nb)
        h = a[rows, :] * h + bx[rows, :]
        hl_ref[t] = (h * sgl[rows, :]).astype(BF16)
    hlast_ref[...] = h


def _sample_proj(xs_t, sc_t, sh, w_bf, w_gate, vecs):
    t_steps, nb, _ = xs_t.shape
    full = lambda shape: pl.BlockSpec(shape, lambda i: (0,) * len(shape))
    tok = lambda dt: jax.ShapeDtypeStruct((t_steps, nb, D_ATT), dt)
    return pl.pallas_call(
        _sample_proj_kernel,
        grid=(1,),
        in_specs=[full(xs_t.shape), full(sc_t.shape), full(sh.shape),
                  full(w_bf.shape), full(w_gate.shape), full(vecs.shape)],
        out_specs=[full((t_steps, nb, D_ATT))] * 5
                  + [full(sc_t.shape), full(sh.shape)],
        out_shape=[tok(F32), tok(F32), tok(F32), tok(BF16), tok(F32),
                   jax.ShapeDtypeStruct(sc_t.shape, F32),
                   jax.ShapeDtypeStruct(sh.shape, F32)],
        compiler_params=pltpu.CompilerParams(
            dimension_semantics=("arbitrary",), vmem_limit_bytes=VMEM_LIMIT),
        name="sample_proj_lru",
    )(xs_t, sc_t, sh, w_bf, w_gate, vecs)


def kernel(x_prompt, x_sample, cache_k, cache_v, state_conv, state_h, w_in, conv_w, conv_b, w_ra, b_ra, w_ri, b_ri, lru_lambda, w_out, ln_g, ln_b):
    assert w_in.shape[0] == 1, "single layer"
    b, s, _ = x_prompt.shape
    nb, t_steps, _ = x_sample.shape
    w_buf = cache_k.shape[2]

    col_scale = jnp.concatenate([jnp.full((D_ATT,), HEAD_DIM ** -0.5, F32),
                                 jnp.ones((D_IN - D_ATT,), F32)])
    w_bf = (w_in[0] * col_scale).astype(BF16)
    eye = jnp.eye(LRU_BLOCKS, dtype=F32)
    block_diag = lambda wb: (eye[:, None, :, None] * wb[:, :, None, :]).reshape(D_LRU, D_LRU)
    w_gate = (0.5 * jnp.concatenate([block_diag(w_ra[0]), block_diag(w_ri[0])], axis=1)).astype(BF16)
    vecs = jnp.concatenate([conv_w[0], conv_b, b_ra, b_ri, lru_lambda], axis=0)
    w_o = w_out[0].astype(BF16)
    ln = jnp.concatenate([ln_g, ln_b], axis=0)
    heads = lambda a: a.reshape(a.shape[:-1] + (N_HEADS, HEAD_DIM))

    xs_t = jnp.transpose(x_sample, (1, 0, 2))
    sc_t = jnp.transpose(state_conv[0], (1, 0, 2))
    q_s, k_s, v_s, hl_s, sga_s, conv_s, h_s = _sample_proj(
        xs_t, sc_t, state_h[0], w_bf, w_gate, vecs)
    kt = jnp.transpose(cache_k[0], (0, 2, 3, 1)).reshape(nb, D_ATT, w_buf)
    vt = jnp.transpose(cache_v[0], (0, 2, 3, 1)).reshape(nb, D_ATT, w_buf)
    sample = (q_s, k_s, v_s, sga_s, kt, vt)

    n_proj = b * (s // PROJ_ROWS) * PROJ_PHASES
    n_attn = b * (D_ATT // HEAD_PAIR) * ATTN_PHASES
    assert n_proj + n_attn == nb
    n_keep = min(w_buf, s)
    q, k, v, hl, sga, conv_p, h_p, kt_p, vt_p, ya_s0 = _proj_lru(
        x_prompt, w_bf, w_gate, vecs, sample + (0,), n_keep)
    ya, ya_s1 = _prompt_attn(q, k, v, sga, sample + (n_proj,))
    y_p = _finish(ya.reshape(b * s, D_ATT), hl.reshape(b * s, D_LRU),
                  x_prompt.reshape(b * s, D_MODEL), w_o, ln, "prompt_finish")
    window = lambda a: jnp.transpose(a.reshape(b, N_HEADS, HEAD_DIM, n_keep), (0, 3, 1, 2))[None]
    k_p, v_p = window(kt_p), window(vt_p)
    ya_s = jnp.concatenate([ya_s0, ya_s1], axis=1)
    y_s = _finish(ya_s.reshape(t_steps * nb, D_ATT).astype(BF16),
                  hl_s.reshape(t_steps * nb, D_LRU),
                  xs_t.reshape(t_steps * nb, D_MODEL), w_o, ln, "sample_finish")
    to_batch_major = lambda a: jnp.transpose(a, (1, 0, 2))

    return (y_p.reshape(b, s, D_MODEL),
            to_batch_major(y_s.reshape(t_steps, nb, D_MODEL)),
            k_p, v_p,
            conv_p[None], h_p.reshape(1, b, D_LRU),
            heads(to_batch_major(k_s))[None], heads(to_batch_major(v_s))[None],
            to_batch_major(conv_s)[None], h_s[None])
```

```python
import functools

import numpy as np
import jax
import jax.numpy as jnp
from jax import lax
from jax.experimental import pallas as pl
from jax.experimental.pallas import tpu as pltpu

F32 = jnp.float32
BF16 = jnp.bfloat16

D_MODEL = 1024
N_HEADS = 8
HEAD_DIM = 64
D_ATT = N_HEADS * HEAD_DIM
D_LRU = 512
LRU_BLOCKS = 8
D_IN = 4 * D_ATT + 2 * D_LRU
CONV_WIDTH = 4
LRU_C = 8.0
DILATIONS = (1, 4, 16)
N_DIL_KEYS = 128
BAND = 128
ALPHA = 2.0 ** 0.25
LN_EPS = 1e-5
NEG_INF = -1e30
LOG2_E = 1.4426950408889634

C_Q, C_K, C_V, C_GA, C_XL, C_GL = (0, D_ATT, 2 * D_ATT, 3 * D_ATT, 4 * D_ATT,
                                   4 * D_ATT + D_LRU)

VMEM_LIMIT = 60 * 1024 * 1024
PROJ_ROWS = 512
FINISH_ROWS = 1024
PROJ_PHASES = 2
HEAD_PAIR = 2 * HEAD_DIM
SUBLANES = 8
CACHE_SLOTS = 3
ATTN_PHASES = 4


def _dot(a, b):
    return jnp.dot(a, b, preferred_element_type=F32)


def _dot_nt(a, b):
    return lax.dot_general(a, b, (((1,), (1,)), ((), ())),
                           preferred_element_type=F32)


def _sigmoid(x):
    return 0.5 * jnp.tanh(0.5 * x) + 0.5


def _silu(x):
    return x * _sigmoid(x)


def _softplus(x):
    return jnp.maximum(x, 0.0) + jnp.log1p(jnp.exp(-jnp.abs(x)))


def _lru_gates(u, wg_ref, vec_ref):
    g = _dot(u.astype(BF16), wg_ref[...])
    tr = jnp.tanh(g[:, :D_LRU] + 0.5 * vec_ref[5:6, :])
    ig = 0.5 * jnp.tanh(g[:, D_LRU:] + 0.5 * vec_ref[6:7, :]) + 0.5
    half_c = (-0.5 * LRU_C) * _softplus(-vec_ref[7:8, :])
    log_a = half_c * tr + half_c
    a = jnp.exp(log_a)
    x = jnp.tanh(-log_a) * (1.0 + a * a)
    bx = jnp.where(x > 0.0, x * lax.rsqrt(x), 0.0) * ig * u
    return a, bx


def _grid_step():
    return ((pl.program_id(0) * pl.num_programs(1) + pl.program_id(1)) * pl.num_programs(2)
            + pl.program_id(2))


def _cache_copies(first, step, cache_hbm, ring, sems):
    slot = step % CACHE_SLOTS
    return [pltpu.make_async_copy(src.at[first + step], dst.at[slot], sems.at[which, slot])
            for which, (src, dst) in enumerate(zip(cache_hbm, ring))]


def _cache_ring_advance(first, n_steps, cache_hbm, ring, sems):
    step = _grid_step()
    lookahead = CACHE_SLOTS - 1

    @pl.when(step == 0)
    def _():
        for early in range(lookahead):
            for copy in _cache_copies(first, early, cache_hbm, ring, sems):
                copy.start()

    @pl.when(step + lookahead < n_steps)
    def _():
        for copy in _cache_copies(first, step + lookahead, cache_hbm, ring, sems):
            copy.start()


def _sample_attn_step(first, q_ref, kn_ref, vn_ref, sga_ref, cnt_ref, cache_hbm, ring, sems, o_ref):
    t_steps = q_ref.shape[0]
    step = _grid_step()
    mine = pl.ds(step % SUBLANES, 1)
    for copy in _cache_copies(first, step, cache_hbm, ring, sems):
        copy.wait()
    kt_ref, vt_ref = (buf.at[step % CACHE_SLOTS] for buf in ring)
    n_rows = t_steps * N_HEADS
    sub = lax.broadcasted_iota(jnp.int32, (N_HEADS, D_ATT), 0)
    lane_head = lax.broadcasted_iota(jnp.int32, (N_HEADS, D_ATT), 1) // HEAD_DIM
    own = sub == lane_head
    qbd = jnp.concatenate(
        [jnp.where(own, jnp.broadcast_to(q_ref[t, mine, :], (N_HEADS, D_ATT)), 0.0)
         for t in range(t_steps)], axis=0)

    cnt = cnt_ref[...]
    s = jnp.where(cnt > 0.0, _dot(qbd.astype(BF16), kt_ref[...].astype(BF16)), NEG_INF)
    m = jnp.max(s, axis=-1, keepdims=True)

    row_t = lax.broadcasted_iota(jnp.int32, (n_rows, 1), 0) // N_HEADS
    s_new, c_new = [], []
    for t2 in range(t_steps):
        c = jnp.where(row_t == t2, float(len(DILATIONS)),
                      jnp.where(row_t > t2, 1.0, 0.0))
        sn = jnp.sum(qbd * kn_ref[t2, mine, :], axis=-1, keepdims=True)
        sn = jnp.where(c > 0.0, sn, NEG_INF)
        m = jnp.maximum(m, sn)
        s_new.append(sn)
        c_new.append(c)

    p = jnp.exp(s - m) * cnt
    den = jnp.sum(p, axis=-1, keepdims=True)
    acc = _dot_nt(p.astype(BF16), vt_ref[...].astype(BF16))
    for t2 in range(t_steps):
        pn = jnp.exp(s_new[t2] - m) * c_new[t2]
        den = den + pn
        acc = acc + pn * vn_ref[t2, mine, :]
    y = acc / den
    for t in range(t_steps):
        yt = jnp.sum(jnp.where(own, y[t * N_HEADS:(t + 1) * N_HEADS, :], 0.0),
                     axis=0, keepdims=True)
        o_ref[t, mine, :] = yt * sga_ref[t, mine, :]


def _pattern_counts(t_steps, w_buf):
    delta = w_buf + np.arange(t_steps)[:, None] - np.arange(w_buf)[None, :]
    cnt = np.zeros((t_steps, w_buf), np.float32)
    for d in DILATIONS:
        cnt += (delta % d == 0) & (delta // d <= N_DIL_KEYS)
    return np.repeat(cnt, N_HEADS, axis=0)


def _sample_attn_operands(sample, grid):
    q_s, kn_s, vn_s, sga_s, kt, vt, first = sample
    t_steps = q_s.shape[0]
    w_buf = kt.shape[-1]
    n_local = grid[0] * grid[1] * grid[2]
    assert first % SUBLANES == 0 and n_local % SUBLANES == 0 and n_local >= CACHE_SLOTS
    cnt = jnp.asarray(_pattern_counts(t_steps, w_buf))
    step = lambda i, j, c: (i * grid[1] + j) * grid[2] + c
    tok_block = (t_steps, SUBLANES, D_ATT)
    tok_in = pl.BlockSpec(tok_block, lambda *g: (0, (first + step(*g)) // SUBLANES, 0))
    in_hbm = pl.BlockSpec(memory_space=pl.ANY)
    cnt_spec = pl.BlockSpec(cnt.shape, lambda *g: (0, 0), pipeline_mode=pl.Buffered(1))
    operands = (q_s, kn_s, vn_s, sga_s, cnt, kt, vt)
    in_specs = [tok_in, tok_in, tok_in, tok_in, cnt_spec, in_hbm, in_hbm]
    out_spec = pl.BlockSpec(tok_block, lambda *g: (0, step(*g) // SUBLANES, 0))
    out_shape = jax.ShapeDtypeStruct((t_steps, n_local, D_ATT), F32)
    scratch = [pltpu.VMEM((CACHE_SLOTS, D_ATT, w_buf), F32),
               pltpu.VMEM((CACHE_SLOTS, D_ATT, w_buf), F32),
               pltpu.SemaphoreType.DMA((2, CACHE_SLOTS))]
    return operands, in_specs, out_spec, out_shape, scratch


def _proj_lru_kernel(first_kept, first_sample, n_steps, x_ref, w_ref, wg_ref, vec_ref,
                     sq_ref, skn_ref, svn_ref, ssga_ref, cnt_ref, kt_hbm, vt_hbm,
                     q_ref, k_ref, v_ref, hl_ref, sga_ref, conv_ref, hlast_ref,
                     kt_keep_ref, vt_keep_ref, sya_ref,
                     xb_ref, xc_ref, a_ref, b_ref, carry_ref, kt_ring, vt_ring, cache_sems):
    j = pl.program_id(1)
    phase = pl.program_id(2)
    rows = x_ref.shape[0]
    kept = j >= first_kept
    cache = ((kt_hbm, vt_hbm), (kt_ring, vt_ring), cache_sems)
    _cache_ring_advance(first_sample, n_steps, *cache)
    sample_step = functools.partial(_sample_attn_step, first_sample, sq_ref, skn_ref, svn_ref,
                                    ssga_ref, cnt_ref, *cache, sya_ref)

    @pl.when(phase == 0)
    def _():
        @pl.when(j == 0)
        def _():
            xc_ref[...] = jnp.zeros((8, D_LRU), F32)
            carry_ref[...] = jnp.zeros((1, D_LRU), F32)

        sample_step()
        xb_ref[...] = x_ref[...].astype(BF16)
        xl = _dot(xb_ref[...], w_ref[:, C_XL:C_GL])
        tail = xc_ref[...]
        xc_ref[...] = xl[rows - 8:, :]
        row8 = lax.broadcasted_iota(jnp.int32, (8, D_LRU), 0)
        u = vec_ref[4:5, :] + xl * vec_ref[3:4, :]
        for s in range(1, CONV_WIDTH):
            rolled = pltpu.roll(xl, s, axis=0)
            head = jnp.where(row8 < s, pltpu.roll(tail, s, axis=0), rolled[0:8, :])
            u = u + jnp.concatenate([head, rolled[8:, :]], axis=0) * vec_ref[3 - s:4 - s, :]
        a, bx = _lru_gates(u, wg_ref, vec_ref)
        a_ref[...] = a
        b_ref[...] = bx
        q_ref[...] = _dot(xb_ref[...], w_ref[:, C_Q:C_K]) * LOG2_E
        k_new = _dot(xb_ref[...], w_ref[:, C_K:C_V])
        k_ref[...] = k_new

        @pl.when(kept)
        def _():
            kt_keep_ref[...] = k_new.T

    @pl.when(phase == 1)
    def _():
        sample_step()
        v_new = _dot(xb_ref[...], w_ref[:, C_V:C_GA])
        v_ref[...] = v_new
        sga_ref[...] = _silu(_dot(xb_ref[...], w_ref[:, C_GA:C_XL])).astype(BF16)
        sgl = _silu(_dot(xb_ref[...], w_ref[:, C_GL:D_IN]))

        row = lax.broadcasted_iota(jnp.int32, (8, D_LRU), 0)

        def group(g, carry):
            r0 = pl.multiple_of(g * 8, 8)
            av = a_ref[pl.ds(r0, 8), :]
            bv = b_ref[pl.ds(r0, 8), :]
            for s in (1, 2, 4):
                keep = row >= s
                bv = jnp.where(keep, av * pltpu.roll(bv, s, axis=0) + bv, bv)
                av = jnp.where(keep, av * pltpu.roll(av, s, axis=0), av)
            h8 = av * carry + bv
            b_ref[pl.ds(r0, 8), :] = h8
            return h8[7:8, :]

        carry = lax.fori_loop(0, rows // 8, group, carry_ref[...], unroll=True)
        carry_ref[...] = carry
        hl_ref[...] = (b_ref[...] * sgl).astype(BF16)

        @pl.when(kept)
        def _():
            vt_keep_ref[...] = v_new.T

        @pl.when(j == pl.num_programs(1) - 1)
        def _():
            conv_ref[...] = xc_ref[5:8, :]
            hlast_ref[...] = carry


def _proj_lru(x, w_bf, w_gate, vecs, sample, n_keep):
    b, s, _ = x.shape
    nt = s // PROJ_ROWS
    assert n_keep % PROJ_ROWS == 0
    first_kept = nt - n_keep // PROJ_ROWS
    keep_spec = pl.BlockSpec((None, D_ATT, PROJ_ROWS),
                             lambda i, j, c: (i, 0, jnp.maximum(j - first_kept, 0)))
    keep_shape = jax.ShapeDtypeStruct((b, D_ATT, n_keep), F32)
    grid = (b, nt, PROJ_PHASES)
    s_ops, s_in, s_out, s_shape, s_scratch = _sample_attn_operands(sample, grid)
    row_spec = lambda width: pl.BlockSpec((None, PROJ_ROWS, width), lambda i, j, c: (i, j, 0))
    const = lambda shape: pl.BlockSpec(shape, lambda i, j, c: (0,) * len(shape),
                                       pipeline_mode=pl.Buffered(1))
    per_batch = lambda n: pl.BlockSpec((None, n, D_LRU), lambda i, j, c: (i, 0, 0))
    tile = lambda dt: jax.ShapeDtypeStruct((b, s, D_ATT), dt)
    return pl.pallas_call(
        functools.partial(_proj_lru_kernel, first_kept, sample[-1], b * nt * PROJ_PHASES),
        grid=grid,
        in_specs=[row_spec(D_MODEL), const((D_MODEL, D_IN)),
                  const((D_LRU, 2 * D_LRU)), const((8, D_LRU))] + s_in,
        out_specs=[row_spec(D_ATT), row_spec(D_ATT), row_spec(D_ATT),
                   row_spec(D_LRU), row_spec(D_ATT),
                   per_batch(CONV_WIDTH - 1), per_batch(1), keep_spec, keep_spec, s_out],
        out_shape=[tile(F32), tile(F32), tile(F32), tile(BF16), tile(BF16),
                   jax.ShapeDtypeStruct((b, CONV_WIDTH - 1, D_LRU), F32),
                   jax.ShapeDtypeStruct((b, 1, D_LRU), F32), keep_shape, keep_shape, s_shape],
        scratch_shapes=[pltpu.VMEM((PROJ_ROWS, D_MODEL), BF16),
                        pltpu.VMEM((8, D_LRU), F32),
                        pltpu.VMEM((PROJ_ROWS, D_LRU), F32),
                        pltpu.VMEM((PROJ_ROWS, D_LRU), F32),
                        pltpu.VMEM((1, D_LRU), F32)] + s_scratch,
        compiler_params=pltpu.CompilerParams(
            dimension_semantics=("arbitrary", "arbitrary", "arbitrary"),
            vmem_limit_bytes=VMEM_LIMIT),
        name="prompt_proj_lru",
    )(x, w_bf, w_gate, vecs, *s_ops)


def _prompt_attn_kernel(first_sample, n_steps, q_ref, k_ref, v_ref, sga_ref,
                        sq_ref, skn_ref, svn_ref, ssga_ref, cnt_ref, kt_hbm, vt_hbm,
                        o_ref, sya_ref, bias_ref, *scratch):
    seq = q_ref.shape[0]
    phase = pl.program_id(2)
    n_groups = seq // BAND
    head_a = lax.broadcasted_iota(jnp.int32, (BAND, HEAD_PAIR), 1) < HEAD_DIM
    one = jnp.ones((), BF16)

    stats, cache = scratch[:6], ((kt_hbm, vt_hbm), scratch[6:8], scratch[8])
    _cache_ring_advance(first_sample, n_steps, *cache)
    sample_step = functools.partial(_sample_attn_step, first_sample, sq_ref, skn_ref, svn_ref,
                                    ssga_ref, cnt_ref, *cache, sya_ref)

    def key_value_block(rows):
        v_bf = v_ref[rows, :].astype(BF16)
        v_ext = jnp.concatenate([jnp.where(head_a, v_bf, one),
                                 jnp.where(head_a, one, v_bf)], axis=1)
        return k_ref[rows, :].astype(BF16), v_ext

    def group(rows, blk, carry):
        k_prev, v_prev = carry
        k_cur, v_cur = key_value_block(rows)
        k2 = jnp.concatenate([k_prev, k_cur], axis=0)
        v4 = jnp.concatenate([v_prev, v_cur], axis=0)
        qf = q_ref[rows, :]
        q2 = jnp.concatenate([jnp.where(head_a, qf, 0.0),
                              jnp.where(head_a, 0.0, qf)], axis=0).astype(BF16)
        s = _dot_nt(q2, k2).reshape(2, BAND, 2 * BAND)
        s = s + bias_ref[jnp.minimum(blk, 1)][None]
        m = jnp.max(s, axis=-1, keepdims=True)
        p = jnp.exp2(s - m).reshape(2 * BAND, 2 * BAND).astype(BF16)
        o = _dot(p, v4)
        acc = jnp.where(head_a, o[:BAND, :HEAD_PAIR], o[BAND:, HEAD_PAIR:])
        den = jnp.where(head_a, o[:BAND, HEAD_PAIR:], o[BAND:, :HEAD_PAIR])
        return acc, den, jnp.where(head_a, m[0], m[1]), (k_cur, v_cur)

    zero_block = (jnp.zeros((BAND, HEAD_PAIR), BF16), jnp.zeros((BAND, 2 * HEAD_PAIR), BF16))

    def strided_pattern(d, first, count, acc_ref, l_ref, m_ref):
        span = BAND * d
        n_blk = seq // span

        def body(i, carry):
            g = first + i
            blk = g % n_blk
            rows = pl.ds(blk * span + g // n_blk, BAND, stride=d)
            acc, den, m, carry = group(rows, blk, carry)
            acc_ref[rows, :] = acc
            l_ref[rows, :] = den
            m_ref[rows, :] = m
            return carry

        lax.fori_loop(0, count, body, zero_block, unroll=True)

    half = n_groups // 2

    @pl.when(phase == 0)
    def _():
        row = lax.broadcasted_iota(jnp.int32, (BAND, 2 * BAND), 0)
        col = lax.broadcasted_iota(jnp.int32, (BAND, 2 * BAND), 1)
        band = (col >= row) & (col <= row + N_DIL_KEYS)
        bias_ref[0] = jnp.where(band & (col >= BAND), 0.0, NEG_INF)
        bias_ref[1] = jnp.where(band, 0.0, NEG_INF)
        sample_step()
        strided_pattern(DILATIONS[1], 0, n_groups, *stats[0:3])

    @pl.when(phase == 1)
    def _():
        sample_step()
        strided_pattern(DILATIONS[2], 0, half, *stats[3:6])

    @pl.when(phase == 2)
    def _():
        sample_step()
        strided_pattern(DILATIONS[2], half, half, *stats[3:6])

    @pl.when(phase == 3)
    def _():
        def body(g, carry):
            rows = pl.ds(pl.multiple_of(g * BAND, BAND), BAND)
            acc, den, m, carry = group(rows, g, carry)
            parts = [(acc, den, m)] + [(stats[3 * t][rows, :], stats[3 * t + 1][rows, :],
                                        stats[3 * t + 2][rows, :]) for t in range(2)]
            m_max = jnp.maximum(jnp.maximum(parts[0][2], parts[1][2]), parts[2][2])
            e = [jnp.exp2(pt[2] - m_max) for pt in parts]
            num = sum(ei * pt[0] for ei, pt in zip(e, parts))
            tot = sum(ei * pt[1] for ei, pt in zip(e, parts))
            o_ref[rows, :] = (num / tot * sga_ref[rows, :].astype(F32)).astype(BF16)
            return carry

        sample_step()
        lax.fori_loop(0, n_groups, body, zero_block, unroll=True)


def _prompt_attn(q, k, v, sga, sample):
    b, s, _ = q.shape
    grid = (b, D_ATT // HEAD_PAIR, ATTN_PHASES)
    s_ops, s_in, s_out, s_shape, s_scratch = _sample_attn_operands(sample, grid)
    spec = pl.BlockSpec((None, s, HEAD_PAIR), lambda i, j, c: (i, 0, j))
    return pl.pallas_call(
        functools.partial(_prompt_attn_kernel, sample[-1], grid[0] * grid[1] * grid[2]),
        grid=grid,
        in_specs=[spec, spec, spec, spec] + s_in,
        out_specs=[spec, s_out],
        out_shape=[jax.ShapeDtypeStruct((b, s, D_ATT), BF16), s_shape],
        scratch_shapes=[pltpu.VMEM((2, BAND, 2 * BAND), F32)]
                       + [pltpu.VMEM((s, HEAD_PAIR), F32) for _ in range(6)] + s_scratch,
        compiler_params=pltpu.CompilerParams(
            dimension_semantics=("arbitrary", "arbitrary", "arbitrary"),
            vmem_limit_bytes=VMEM_LIMIT),
        name="prompt_attn",
    )(q, k, v, sga, *s_ops)


def _finish_kernel(ya_ref, hl_ref, x_ref, wo_ref, ln_ref, o_ref):
    sub = (_dot(ya_ref[...], wo_ref[0:D_ATT, :])
           + _dot(hl_ref[...], wo_ref[D_ATT:D_ATT + D_LRU, :]))
    z = ALPHA * x_ref[...] + sub
    mu = jnp.mean(z, axis=-1, keepdims=True)
    zc = z - mu
    var = jnp.mean(zc * zc, axis=-1, keepdims=True)
    o_ref[...] = zc * lax.rsqrt(var + LN_EPS) * ln_ref[0:1, :] + ln_ref[1:2, :]


def _finish(ya, hl, x, w_o, ln, name):
    n = x.shape[0]
    rows = min(n, FINISH_ROWS)
    row_spec = lambda width: pl.BlockSpec((rows, width), lambda i: (i, 0))
    const = lambda shape: pl.BlockSpec(shape, lambda i: (0, 0),
                                       pipeline_mode=pl.Buffered(1))
    return pl.pallas_call(
        _finish_kernel,
        grid=(n // rows,),
        in_specs=[row_spec(D_ATT), row_spec(D_LRU), row_spec(D_MODEL),
                  const((D_ATT + D_LRU, D_MODEL)), const((2, D_MODEL))],
        out_specs=row_spec(D_MODEL),
        out_shape=jax.ShapeDtypeStruct((n, D_MODEL), F32),
        compiler_params=pltpu.CompilerParams(
            dimension_semantics=("arbitrary",), vmem_limit_bytes=VMEM_LIMIT),
        name=name,
    )(ya, hl, x, w_o, ln)


def _sample_proj_kernel(x_ref, sc_ref, sh_ref, w_ref, wg_ref, vec_ref,
                        q_ref, k_ref, v_ref, hl_ref, sga_ref, conv_ref, hlast_ref):
    t_steps, nb, _ = x_ref.shape
    xb = x_ref[...].reshape(t_steps * nb, D_MODEL).astype(BF16)
    split = lambda y: y.reshape(t_steps, nb, y.shape[-1])
    q_ref[...] = split(_dot(xb, w_ref[:, C_Q:C_K]))
    k_ref[...] = split(_dot(xb, w_ref[:, C_K:C_V]))
    v_ref[...] = split(_dot(xb, w_ref[:, C_V:C_GA]))
    sga_ref[...] = split(_silu(_dot(xb, w_ref[:, C_GA:C_XL])))

    xl = _dot(xb, w_ref[:, C_XL:C_GL])
    xc = [sc_ref[i] for i in range(CONV_WIDTH - 1)]
    xc += [xl[t * nb:(t + 1) * nb, :] for t in range(t_steps)]
    u = jnp.concatenate(
        [vec_ref[4:5, :] + sum(xc[t + w] * vec_ref[w:w + 1, :] for w in range(CONV_WIDTH))
         for t in range(t_steps)], axis=0)
    for i in range(CONV_WIDTH - 1):
        conv_ref[i] = xc[t_steps + i]

    a, bx = _lru_gates(u, wg_ref, vec_ref)
    sgl = _silu(_dot(xb, w_ref[:, C_GL:D_IN]))
    h = sh_ref[...]
    for t in range(t_steps):
        rows = slice(t * nb, (t + 1) * nb)
        h = a[rows, :] * h + bx[rows, :]
        hl_ref[t] = (h * sgl[rows, :]).astype(BF16)
    hlast_ref[...] = h


def _sample_proj(xs_t, sc_t, sh, w_bf, w_gate, vecs):
    t_steps, nb, _ = xs_t.shape
    full = lambda shape: pl.BlockSpec(shape, lambda i: (0,) * len(shape))
    tok = lambda dt: jax.ShapeDtypeStruct((t_steps, nb, D_ATT), dt)
    return pl.pallas_call(
        _sample_proj_kernel,
        grid=(1,),
        in_specs=[full(xs_t.shape), full(sc_t.shape), full(sh.shape),
                  full(w_bf.shape), full(w_gate.shape), full(vecs.shape)],
        out_specs=[full((t_steps, nb, D_ATT))] * 5
                  + [full(sc_t.shape), full(sh.shape)],
        out_shape=[tok(F32), tok(F32), tok(F32), tok(BF16), tok(F32),
                   jax.ShapeDtypeStruct(sc_t.shape, F32),
                   jax.ShapeDtypeStruct(sh.shape, F32)],
        compiler_params=pltpu.CompilerParams(
            dimension_semantics=("arbitrary",), vmem_limit_bytes=VMEM_LIMIT),
        name="sample_proj_lru",
    )(xs_t, sc_t, sh, w_bf, w_gate, vecs)


def kernel(x_prompt, x_sample, cache_k, cache_v, state_conv, state_h, w_in, conv_w, conv_b, w_ra, b_ra, w_ri, b_ri, lru_lambda, w_out, ln_g, ln_b):
    assert w_in.shape[0] == 1, "single layer"
    b, s, _ = x_prompt.shape
    nb, t_steps, _ = x_sample.shape
    w_buf = cache_k.shape[2]

    col_scale = jnp.concatenate([jnp.full((D_ATT,), HEAD_DIM ** -0.5, F32),
                                 jnp.ones((D_IN - D_ATT,), F32)])
    w_bf = (w_in[0] * col_scale).astype(BF16)
    eye = jnp.eye(LRU_BLOCKS, dtype=F32)
    block_diag = lambda wb: (eye[:, None, :, None] * wb[:, :, None, :]).reshape(D_LRU, D_LRU)
    w_gate = (0.5 * jnp.concatenate([block_diag(w_ra[0]), block_diag(w_ri[0])], axis=1)).astype(BF16)
    vecs = jnp.concatenate([conv_w[0], conv_b, b_ra, b_ri, lru_lambda], axis=0)
    w_o = w_out[0].astype(BF16)
    ln = jnp.concatenate([ln_g, ln_b], axis=0)
    heads = lambda a: a.reshape(a.shape[:-1] + (N_HEADS, HEAD_DIM))

    xs_t = jnp.transpose(x_sample, (1, 0, 2))
    sc_t = jnp.transpose(state_conv[0], (1, 0, 2))
    q_s, k_s, v_s, hl_s, sga_s, conv_s, h_s = _sample_proj(
        xs_t, sc_t, state_h[0], w_bf, w_gate, vecs)
    kt = jnp.transpose(cache_k[0], (0, 2, 3, 1)).reshape(nb, D_ATT, w_buf)
    vt = jnp.transpose(cache_v[0], (0, 2, 3, 1)).reshape(nb, D_ATT, w_buf)
    sample = (q_s, k_s, v_s, sga_s, kt, vt)

    n_proj = b * (s // PROJ_ROWS) * PROJ_PHASES
    n_attn = b * (D_ATT // HEAD_PAIR) * ATTN_PHASES
    assert n_proj + n_attn == nb
    n_keep = min(w_buf, s)
    q, k, v, hl, sga, conv_p, h_p, kt_p, vt_p, ya_s0 = _proj_lru(
        x_prompt, w_bf, w_gate, vecs, sample + (0,), n_keep)
    ya, ya_s1 = _prompt_attn(q, k, v, sga, sample + (n_proj,))
    y_p = _finish(ya.reshape(b * s, D_ATT), hl.reshape(b * s, D_LRU),
                  x_prompt.reshape(b * s, D_MODEL), w_o, ln, "prompt_finish")
    window = lambda a: jnp.transpose(a.reshape(b, N_HEADS, HEAD_DIM, n_keep), (0, 3, 1, 2))[None]
    k_p, v_p = window(kt_p), window(vt_p)
    ya_s = jnp.concatenate([ya_s0, ya_s1], axis=1)
    y_s = _finish(ya_s.reshape(t_steps * nb, D_ATT).astype(BF16),
                  hl_s.reshape(t_steps * nb, D_LRU),
                  xs_t.reshape(t_steps * nb, D_MODEL), w_o, ln, "sample_finish")
    to_batch_major = lambda a: jnp.transpose(a, (1, 0, 2))

    return (y_p.reshape(b, s, D_MODEL),
            to_batch_major(y_s.reshape(t_steps, nb, D_MODEL)),
            k_p, v_p,
            conv_p[None], h_p.reshape(1, b, D_LRU),
            heads(to_batch_major(k_s))[None], heads(to_batch_major(v_s))[None],
            to_batch_major(conv_s)[None], h_s[None])
```

```python
import functools

import numpy as np
import jax
import jax.numpy as jnp
from jax import lax
from jax.experimental import pallas as pl
from jax.experimental.pallas import tpu as pltpu

F32 = jnp.float32
BF16 = jnp.bfloat16

D_MODEL = 1024
N_HEADS = 8
HEAD_DIM = 64
D_ATT = N_HEADS * HEAD_DIM
D_LRU = 512
LRU_BLOCKS = 8
D_IN = 4 * D_ATT + 2 * D_LRU
CONV_WIDTH = 4
LRU_C = 8.0
DILATIONS = (1, 4, 16)
N_DIL_KEYS = 128
BAND = 128
ALPHA = 2.0 ** 0.25
LN_EPS = 1e-5
NEG_INF = -1e30
LOG2_E = 1.4426950408889634

C_Q, C_K, C_V, C_GA, C_XL, C_GL = (0, D_ATT, 2 * D_ATT, 3 * D_ATT, 4 * D_ATT,
                                   4 * D_ATT + D_LRU)

VMEM_LIMIT = 60 * 1024 * 1024
PROJ_ROWS = 512
FINISH_ROWS = 1024
PROJ_PHASES = 2
HEAD_PAIR = 2 * HEAD_DIM
SUBLANES = 8
CACHE_SLOTS = 3
ATTN_PHASES = 4


def _dot(a, b):
    return jnp.dot(a, b, preferred_element_type=F32)


def _dot_nt(a, b):
    return lax.dot_general(a, b, (((1,), (1,)), ((), ())),
                           preferred_element_type=F32)


def _sigmoid(x):
    return 0.5 * jnp.tanh(0.5 * x) + 0.5


def _silu(x):
    return x * _sigmoid(x)


def _softplus(x):
    return jnp.maximum(x, 0.0) + jnp.log1p(jnp.exp(-jnp.abs(x)))


def _lru_gates(u, wg_ref, vec_ref):
    g = _dot(u.astype(BF16), wg_ref[...])
    tr = jnp.tanh(g[:, :D_LRU] + 0.5 * vec_ref[5:6, :])
    ig = 0.5 * jnp.tanh(g[:, D_LRU:] + 0.5 * vec_ref[6:7, :]) + 0.5
    half_c = (-0.5 * LRU_C) * _softplus(-vec_ref[7:8, :])
    log_a = half_c * tr + half_c
    a = jnp.exp(log_a)
    x = jnp.tanh(-log_a) * (1.0 + a * a)
    bx = jnp.where(x > 0.0, x * lax.rsqrt(x), 0.0) * ig * u
    return a, bx


def _grid_step():
    return ((pl.program_id(0) * pl.num_programs(1) + pl.program_id(1)) * pl.num_programs(2)
            + pl.program_id(2))


def _cache_copies(first, step, cache_hbm, ring, sems):
    slot = step % CACHE_SLOTS
    return [pltpu.make_async_copy(src.at[first + step], dst.at[slot], sems.at[which, slot])
            for which, (src, dst) in enumerate(zip(cache_hbm, ring))]


def _cache_ring_advance(first, n_steps, cache_hbm, ring, sems):
    step = _grid_step()
    lookahead = CACHE_SLOTS - 1

    @pl.when(step == 0)
    def _():
        for early in range(lookahead):
            for copy in _cache_copies(first, early, cache_hbm, ring, sems):
                copy.start()

    @pl.when(step + lookahead < n_steps)
    def _():
        for copy in _cache_copies(first, step + lookahead, cache_hbm, ring, sems):
            copy.start()


def _sample_attn_step(first, q_ref, kn_ref, vn_ref, sga_ref, cnt_ref, cache_hbm, ring, sems, o_ref):
    t_steps = q_ref.shape[0]
    step = _grid_step()
    mine = pl.ds(step % SUBLANES, 1)
    for copy in _cache_copies(first, step, cache_hbm, ring, sems):
        copy.wait()
    kt_ref, vt_ref = (buf.at[step % CACHE_SLOTS] for buf in ring)
    n_rows = t_steps * N_HEADS
    sub = lax.broadcasted_iota(jnp.int32, (N_HEADS, D_ATT), 0)
    lane_head = lax.broadcasted_iota(jnp.int32, (N_HEADS, D_ATT), 1) // HEAD_DIM
    own = sub == lane_head
    qbd = jnp.concatenate(
        [jnp.where(own, jnp.broadcast_to(q_ref[t, mine, :], (N_HEADS, D_ATT)), 0.0)
         for t in range(t_steps)], axis=0)

    cnt = cnt_ref[...]
    s = jnp.where(cnt > 0.0, _dot(qbd.astype(BF16), kt_ref[...].astype(BF16)), NEG_INF)
    m = jnp.max(s, axis=-1, keepdims=True)

    row_t = lax.broadcasted_iota(jnp.int32, (n_rows, 1), 0) // N_HEADS
    s_new, c_new = [], []
    for t2 in range(t_steps):
        c = jnp.where(row_t == t2, float(len(DILATIONS)),
                      jnp.where(row_t > t2, 1.0, 0.0))
        sn = jnp.sum(qbd * kn_ref[t2, mine, :], axis=-1, keepdims=True)
        sn = jnp.where(c > 0.0, sn, NEG_INF)
        m = jnp.maximum(m, sn)
        s_new.append(sn)
        c_new.append(c)

    p = jnp.exp(s - m) * cnt
    den = jnp.sum(p, axis=-1, keepdims=True)
    acc = _dot_nt(p.astype(BF16), vt_ref[...].astype(BF16))
    for t2 in range(t_steps):
        pn = jnp.exp(s_new[t2] - m) * c_new[t2]
        den = den + pn
        acc = acc + pn * vn_ref[t2, mine, :]
    y = acc / den
    for t in range(t_steps):
        yt = jnp.sum(jnp.where(own, y[t * N_HEADS:(t + 1) * N_HEADS, :], 0.0),
                     axis=0, keepdims=True)
        o_ref[t, mine, :] = yt * sga_ref[t, mine, :]


def _pattern_counts(t_steps, w_buf):
    delta = w_buf + np.arange(t_steps)[:, None] - np.arange(w_buf)[None, :]
    cnt = np.zeros((t_steps, w_buf), np.float32)
    for d in DILATIONS:
        cnt += (delta % d == 0) & (delta // d <= N_DIL_KEYS)
    return np.repeat(cnt, N_HEADS, axis=0)


def _sample_attn_operands(sample, grid):
    q_s, kn_s, vn_s, sga_s, kt, vt, first = sample
    t_steps = q_s.shape[0]
    w_buf = kt.shape[-1]
    n_local = grid[0] * grid[1] * grid[2]
    assert first % SUBLANES == 0 and n_local % SUBLANES == 0 and n_local >= CACHE_SLOTS
    cnt = jnp.asarray(_pattern_counts(t_steps, w_buf))
    step = lambda i, j, c: (i * grid[1] + j) * grid[2] + c
    tok_block = (t_steps, SUBLANES, D_ATT)
    tok_in = pl.BlockSpec(tok_block, lambda *g: (0, (first + step(*g)) // SUBLANES, 0))
    in_hbm = pl.BlockSpec(memory_space=pl.ANY)
    cnt_spec = pl.BlockSpec(cnt.shape, lambda *g: (0, 0), pipeline_mode=pl.Buffered(1))
    operands = (q_s, kn_s, vn_s, sga_s, cnt, kt, vt)
    in_specs = [tok_in, tok_in, tok_in, tok_in, cnt_spec, in_hbm, in_hbm]
    out_spec = pl.BlockSpec(tok_block, lambda *g: (0, step(*g) // SUBLANES, 0))
    out_shape = jax.ShapeDtypeStruct((t_steps, n_local, D_ATT), F32)
    scratch = [pltpu.VMEM((CACHE_SLOTS, D_ATT, w_buf), F32),
               pltpu.VMEM((CACHE_SLOTS, D_ATT, w_buf), F32),
               pltpu.SemaphoreType.DMA((2, CACHE_SLOTS))]
    return operands, in_specs, out_spec, out_shape, scratch


def _proj_lru_kernel(first_kept, first_sample, n_steps, x_ref, w_ref, wg_ref, vec_ref,
                     sq_ref, skn_ref, svn_ref, ssga_ref, cnt_ref, kt_hbm, vt_hbm,
                     q_ref, k_ref, v_ref, hl_ref, sga_ref, conv_ref, hlast_ref,
                     kt_keep_ref, vt_keep_ref, sya_ref,
                     xb_ref, xc_ref, a_ref, b_ref, carry_ref, kt_ring, vt_ring, cache_sems):
    j = pl.program_id(1)
    phase = pl.program_id(2)
    rows = x_ref.shape[0]
    kept = j >= first_kept
    cache = ((kt_hbm, vt_hbm), (kt_ring, vt_ring), cache_sems)
    _cache_ring_advance(first_sample, n_steps, *cache)
    sample_step = functools.partial(_sample_attn_step, first_sample, sq_ref, skn_ref, svn_ref,
                                    ssga_ref, cnt_ref, *cache, sya_ref)

    @pl.when(phase == 0)
    def _():
        @pl.when(j == 0)
        def _():
            xc_ref[...] = jnp.zeros((8, D_LRU), F32)
            carry_ref[...] = jnp.zeros((1, D_LRU), F32)

        sample_step()
        xb_ref[...] = x_ref[...].astype(BF16)
        xl = _dot(xb_ref[...], w_ref[:, C_XL:C_GL])
        tail = xc_ref[...]
        xc_ref[...] = xl[rows - 8:, :]
        row8 = lax.broadcasted_iota(jnp.int32, (8, D_LRU), 0)
        u = vec_ref[4:5, :] + xl * vec_ref[3:4, :]
        for s in range(1, CONV_WIDTH):
            rolled = pltpu.roll(xl, s, axis=0)
            head = jnp.where(row8 < s, pltpu.roll(tail, s, axis=0), rolled[0:8, :])
            u = u + jnp.concatenate([head, rolled[8:, :]], axis=0) * vec_ref[3 - s:4 - s, :]
        a, bx = _lru_gates(u, wg_ref, vec_ref)
        a_ref[...] = a
        b_ref[...] = bx
        q_ref[...] = _dot(xb_ref[...], w_ref[:, C_Q:C_K]) * LOG2_E
        k_new = _dot(xb_ref[...], w_ref[:, C_K:C_V])
        k_ref[...] = k_new
        v_new = _dot(xb_ref[...], w_ref[:, C_V:C_GA])
        v_ref[...] = v_new

        @pl.when(kept)
        def _():
            kt_keep_ref[...] = k_new.T
            vt_keep_ref[...] = v_new.T

    @pl.when(phase == 1)
    def _():
        sample_step()
        sga_ref[...] = _silu(_dot(xb_ref[...], w_ref[:, C_GA:C_XL])).astype(BF16)
        sgl = _silu(_dot(xb_ref[...], w_ref[:, C_GL:D_IN]))

        row = lax.broadcasted_iota(jnp.int32, (8, D_LRU), 0)

        def group(g, carry):
            r0 = pl.multiple_of(g * 8, 8)
            av = a_ref[pl.ds(r0, 8), :]
            bv = b_ref[pl.ds(r0, 8), :]
            for s in (1, 2, 4):
                keep = row >= s
                bv = jnp.where(keep, av * pltpu.roll(bv, s, axis=0) + bv, bv)
                av = jnp.where(keep, av * pltpu.roll(av, s, axis=0), av)
            h8 = av * carry + bv
            b_ref[pl.ds(r0, 8), :] = h8
            return h8[7:8, :]

        carry = lax.fori_loop(0, rows // 8, group, carry_ref[...], unroll=True)
        carry_ref[...] = carry
        hl_ref[...] = (b_ref[...] * sgl).astype(BF16)

        @pl.when(j == pl.num_programs(1) - 1)
        def _():
            conv_ref[...] = xc_ref[5:8, :]
            hlast_ref[...] = carry


def _proj_lru(x, w_bf, w_gate, vecs, sample, n_keep):
    b, s, _ = x.shape
    nt = s // PROJ_ROWS
    assert n_keep % PROJ_ROWS == 0
    first_kept = nt - n_keep // PROJ_ROWS
    keep_spec = pl.BlockSpec((None, D_ATT, PROJ_ROWS),
                             lambda i, j, c: (i, 0, jnp.maximum(j - first_kept, 0)))
    keep_shape = jax.ShapeDtypeStruct((b, D_ATT, n_keep), F32)
    grid = (b, nt, PROJ_PHASES)
    s_ops, s_in, s_out, s_shape, s_scratch = _sample_attn_operands(sample, grid)
    row_spec = lambda width: pl.BlockSpec((None, PROJ_ROWS, width), lambda i, j, c: (i, j, 0))
    const = lambda shape: pl.BlockSpec(shape, lambda i, j, c: (0,) * len(shape),
                                       pipeline_mode=pl.Buffered(1))
    per_batch = lambda n: pl.BlockSpec((None, n, D_LRU), lambda i, j, c: (i, 0, 0))
    tile = lambda dt: jax.ShapeDtypeStruct((b, s, D_ATT), dt)
    return pl.pallas_call(
        functools.partial(_proj_lru_kernel, first_kept, sample[-1], b * nt * PROJ_PHASES),
        grid=grid,
        in_specs=[row_spec(D_MODEL), const((D_MODEL, D_IN)),
                  const((D_LRU, 2 * D_LRU)), const((8, D_LRU))] + s_in,
        out_specs=[row_spec(D_ATT), row_spec(D_ATT), row_spec(D_ATT),
                   row_spec(D_LRU), row_spec(D_ATT),
                   per_batch(CONV_WIDTH - 1), per_batch(1), keep_spec, keep_spec, s_out],
        out_shape=[tile(F32), tile(F32), tile(F32), tile(BF16), tile(BF16),
                   jax.ShapeDtypeStruct((b, CONV_WIDTH - 1, D_LRU), F32),
                   jax.ShapeDtypeStruct((b, 1, D_LRU), F32), keep_shape, keep_shape, s_shape],
        scratch_shapes=[pltpu.VMEM((PROJ_ROWS, D_MODEL), BF16),
                        pltpu.VMEM((8, D_LRU), F32),
                        pltpu.VMEM((PROJ_ROWS, D_LRU), F32),
                        pltpu.VMEM((PROJ_ROWS, D_LRU), F32),
                        pltpu.VMEM((1, D_LRU), F32)] + s_scratch,
        compiler_params=pltpu.CompilerParams(
            dimension_semantics=("arbitrary", "arbitrary", "arbitrary"),
            vmem_limit_bytes=VMEM_LIMIT),
        name="prompt_proj_lru",
    )(x, w_bf, w_gate, vecs, *s_ops)


def _prompt_attn_kernel(first_sample, n_steps, q_ref, k_ref, v_ref, sga_ref,
                        sq_ref, skn_ref, svn_ref, ssga_ref, cnt_ref, kt_hbm, vt_hbm,
                        o_ref, sya_ref, bias_ref, *scratch):
    seq = q_ref.shape[0]
    phase = pl.program_id(2)
    n_groups = seq // BAND
    head_a = lax.broadcasted_iota(jnp.int32, (BAND, HEAD_PAIR), 1) < HEAD_DIM
    one = jnp.ones((), BF16)

    stats, cache = scratch[:6], ((kt_hbm, vt_hbm), scratch[6:8], scratch[8])
    _cache_ring_advance(first_sample, n_steps, *cache)
    sample_step = functools.partial(_sample_attn_step, first_sample, sq_ref, skn_ref, svn_ref,
                                    ssga_ref, cnt_ref, *cache, sya_ref)

    def key_value_block(rows):
        v_bf = v_ref[rows, :].astype(BF16)
        v_ext = jnp.concatenate([jnp.where(head_a, v_bf, one),
                                 jnp.where(head_a, one, v_bf)], axis=1)
        return k_ref[rows, :].astype(BF16), v_ext

    def group(rows, blk, carry):
        k_prev, v_prev = carry
        k_cur, v_cur = key_value_block(rows)
        k2 = jnp.concatenate([k_prev, k_cur], axis=0)
        v4 = jnp.concatenate([v_prev, v_cur], axis=0)
        qf = q_ref[rows, :]
        q2 = jnp.concatenate([jnp.where(head_a, qf, 0.0),
                              jnp.where(head_a, 0.0, qf)], axis=0).astype(BF16)
        s = _dot_nt(q2, k2).reshape(2, BAND, 2 * BAND)
        s = s + bias_ref[jnp.minimum(blk, 1)][None]
        m = jnp.max(s, axis=-1, keepdims=True)
        p = jnp.exp2(s - m).reshape(2 * BAND, 2 * BAND).astype(BF16)
        o = _dot(p, v4)
        acc = jnp.where(head_a, o[:BAND, :HEAD_PAIR], o[BAND:, HEAD_PAIR:])
        den = jnp.where(head_a, o[:BAND, HEAD_PAIR:], o[BAND:, :HEAD_PAIR])
        return acc, den, jnp.where(head_a, m[0], m[1]), (k_cur, v_cur)

    zero_block = (jnp.zeros((BAND, HEAD_PAIR), BF16), jnp.zeros((BAND, 2 * HEAD_PAIR), BF16))

    def strided_pattern(d, first, count, acc_ref, l_ref, m_ref):
        span = BAND * d
        n_blk = seq // span

        def body(i, carry):
            g = first + i
            blk = g % n_blk
            rows = pl.ds(blk * span + g // n_blk, BAND, stride=d)
            acc, den, m, carry = group(rows, blk, carry)
            acc_ref[rows, :] = acc
            l_ref[rows, :] = den
            m_ref[rows, :] = m
            return carry

        lax.fori_loop(0, count, body, zero_block, unroll=True)

    half = n_groups // 2

    @pl.when(phase == 0)
    def _():
        row = lax.broadcasted_iota(jnp.int32, (BAND, 2 * BAND), 0)
        col = lax.broadcasted_iota(jnp.int32, (BAND, 2 * BAND), 1)
        band = (col >= row) & (col <= row + N_DIL_KEYS)
        bias_ref[0] = jnp.where(band & (col >= BAND), 0.0, NEG_INF)
        bias_ref[1] = jnp.where(band, 0.0, NEG_INF)
        sample_step()
        strided_pattern(DILATIONS[1], 0, n_groups, *stats[0:3])

    @pl.when(phase == 1)
    def _():
        sample_step()
        strided_pattern(DILATIONS[2], 0, half, *stats[3:6])

    @pl.when(phase == 2)
    def _():
        sample_step()
        strided_pattern(DILATIONS[2], half, half, *stats[3:6])

    @pl.when(phase == 3)
    def _():
        def body(g, carry):
            rows = pl.ds(pl.multiple_of(g * BAND, BAND), BAND)
            acc, den, m, carry = group(rows, g, carry)
            parts = [(acc, den, m)] + [(stats[3 * t][rows, :], stats[3 * t + 1][rows, :],
                                        stats[3 * t + 2][rows, :]) for t in range(2)]
            m_max = jnp.maximum(jnp.maximum(parts[0][2], parts[1][2]), parts[2][2])
            e = [jnp.exp2(pt[2] - m_max) for pt in parts]
            num = sum(ei * pt[0] for ei, pt in zip(e, parts))
            tot = sum(ei * pt[1] for ei, pt in zip(e, parts))
            o_ref[rows, :] = (num / tot * sga_ref[rows, :].astype(F32)).astype(BF16)
            return carry

        sample_step()
        lax.fori_loop(0, n_groups, body, zero_block, unroll=True)


def _prompt_attn(q, k, v, sga, sample):
    b, s, _ = q.shape
    grid = (b, D_ATT // HEAD_PAIR, ATTN_PHASES)
    s_ops, s_in, s_out, s_shape, s_scratch = _sample_attn_operands(sample, grid)
    spec = pl.BlockSpec((None, s, HEAD_PAIR), lambda i, j, c: (i, 0, j))
    return pl.pallas_call(
        functools.partial(_prompt_attn_kernel, sample[-1], grid[0] * grid[1] * grid[2]),
        grid=grid,
        in_specs=[spec, spec, spec, spec] + s_in,
        out_specs=[spec, s_out],
        out_shape=[jax.ShapeDtypeStruct((b, s, D_ATT), BF16), s_shape],
        scratch_shapes=[pltpu.VMEM((2, BAND, 2 * BAND), F32)]
                       + [pltpu.VMEM((s, HEAD_PAIR), F32) for _ in range(6)] + s_scratch,
        compiler_params=pltpu.CompilerParams(
            dimension_semantics=("arbitrary", "arbitrary", "arbitrary"),
            vmem_limit_bytes=VMEM_LIMIT),
        name="prompt_attn",
    )(q, k, v, sga, *s_ops)


def _finish_kernel(ya_ref, hl_ref, x_ref, wo_ref, ln_ref, o_ref):
    sub = (_dot(ya_ref[...], wo_ref[0:D_ATT, :])
           + _dot(hl_ref[...], wo_ref[D_ATT:D_ATT + D_LRU, :]))
    z = ALPHA * x_ref[...] + sub
    mu = jnp.mean(z, axis=-1, keepdims=True)
    zc = z - mu
    var = jnp.mean(zc * zc, axis=-1, keepdims=True)
    o_ref[...] = zc * lax.rsqrt(var + LN_EPS) * ln_ref[0:1, :] + ln_ref[1:2, :]


def _finish(ya, hl, x, w_o, ln, name):
    n = x.shape[0]
    rows = min(n, FINISH_ROWS)
    row_spec = lambda width: pl.BlockSpec((rows, width), lambda i: (i, 0))
    const = lambda shape: pl.BlockSpec(shape, lambda i: (0, 0),
                                       pipeline_mode=pl.Buffered(1))
    return pl.pallas_call(
        _finish_kernel,
        grid=(n // rows,),
        in_specs=[row_spec(D_ATT), row_spec(D_LRU), row_spec(D_MODEL),
                  const((D_ATT + D_LRU, D_MODEL)), const((2, D_MODEL))],
        out_specs=row_spec(D_MODEL),
        out_shape=jax.ShapeDtypeStruct((n, D_MODEL), F32),
        compiler_params=pltpu.CompilerParams(
            dimension_semantics=("arbitrary",), vmem_limit_bytes=VMEM_LIMIT),
        name=name,
    )(ya, hl, x, w_o, ln)


def _sample_proj_kernel(x_ref, sc_ref, sh_ref, w_ref, wg_ref, vec_ref,
                        q_ref, k_ref, v_ref, hl_ref, sga_ref, conv_ref, hlast_ref):
    t_steps, nb, _ = x_ref.shape
    xb = x_ref[...].reshape(t_steps * nb, D_MODEL).astype(BF16)
    split = lambda y: y.reshape(t_steps, nb, y.shape[-1])
    q_ref[...] = split(_dot(xb, w_ref[:, C_Q:C_K]))
    k_ref[...] = split(_dot(xb, w_ref[:, C_K:C_V]))
    v_ref[...] = split(_dot(xb, w_ref[:, C_V:C_GA]))
    sga_ref[...] = split(_silu(_dot(xb, w_ref[:, C_GA:C_XL])))

    xl = _dot(xb, w_ref[:, C_XL:C_GL])
    xc = [sc_ref[i] for i in range(CONV_WIDTH - 1)]
    xc += [xl[t * nb:(t + 1) * nb, :] for t in range(t_steps)]
    u = jnp.concatenate(
        [vec_ref[4:5, :] + sum(xc[t + w] * vec_ref[w:w + 1, :] for w in range(CONV_WIDTH))
         for t in range(t_steps)], axis=0)
    for i in range(CONV_WIDTH - 1):
        conv_ref[i] = xc[t_steps + i]

    a, bx = _lru_gates(u, wg_ref, vec_ref)
    sgl = _silu(_dot(xb, w_ref[:, C_GL:D_IN]))
    h = sh_ref[...]
    for t in range(t_steps):
        rows = slice(t * nb, (t + 1) * nb)
        h = a[rows, :] * h + bx[rows, :]
        hl_ref[t] = (h * sgl[rows, :]).astype(BF16)
    hlast_ref[...] = h


def _sample_proj(xs_t, sc_t, sh, w_bf, w_gate, vecs):
    t_steps, nb, _ = xs_t.shape
    full = lambda shape: pl.BlockSpec(shape, lambda i: (0,) * len(shape))
    tok = lambda dt: jax.ShapeDtypeStruct((t_steps, nb, D_ATT), dt)
    return pl.pallas_call(
        _sample_proj_kernel,
        grid=(1,),
        in_specs=[full(xs_t.shape), full(sc_t.shape), full(sh.shape),
                  full(w_bf.shape), full(w_gate.shape), full(vecs.shape)],
        out_specs=[full((t_steps, nb, D_ATT))] * 5
                  + [full(sc_t.shape), full(sh.shape)],
        out_shape=[tok(F32), tok(F32), tok(F32), tok(BF16), tok(F32),
                   jax.ShapeDtypeStruct(sc_t.shape, F32),
                   jax.ShapeDtypeStruct(sh.shape, F32)],
        compiler_params=pltpu.CompilerParams(
            dimension_semantics=("arbitrary",), vmem_limit_bytes=VMEM_LIMIT),
        name="sample_proj_lru",
    )(xs_t, sc_t, sh, w_bf, w_gate, vecs)


def kernel(x_prompt, x_sample, cache_k, cache_v, state_conv, state_h, w_in, conv_w, conv_b, w_ra, b_ra, w_ri, b_ri, lru_lambda, w_out, ln_g, ln_b):
    assert w_in.shape[0] == 1, "single layer"
    b, s, _ = x_prompt.shape
    nb, t_steps, _ = x_sample.shape
    w_buf = cache_k.shape[2]

    col_scale = jnp.concatenate([jnp.full((D_ATT,), HEAD_DIM ** -0.5, F32),
                                 jnp.ones((D_IN - D_ATT,), F32)])
    w_bf = (w_in[0] * col_scale).astype(BF16)
    eye = jnp.eye(LRU_BLOCKS, dtype=F32)
    block_diag = lambda wb: (eye[:, None, :, None] * wb[:, :, None, :]).reshape(D_LRU, D_LRU)
    w_gate = (0.5 * jnp.concatenate([block_diag(w_ra[0]), block_diag(w_ri[0])], axis=1)).astype(BF16)
    vecs = jnp.concatenate([conv_w[0], conv_b, b_ra, b_ri, lru_lambda], axis=0)
    w_o = w_out[0].astype(BF16)
    ln = jnp.concatenate([ln_g, ln_b], axis=0)
    heads = lambda a: a.reshape(a.shape[:-1] + (N_HEADS, HEAD_DIM))

    xs_t = jnp.transpose(x_sample, (1, 0, 2))
    sc_t = jnp.transpose(state_conv[0], (1, 0, 2))
    q_s, k_s, v_s, hl_s, sga_s, conv_s, h_s = _sample_proj(
        xs_t, sc_t, state_h[0], w_bf, w_gate, vecs)
    kt = jnp.transpose(cache_k[0], (0, 2, 3, 1)).reshape(nb, D_ATT, w_buf)
    vt = jnp.transpose(cache_v[0], (0, 2, 3, 1)).reshape(nb, D_ATT, w_buf)
    sample = (q_s, k_s, v_s, sga_s, kt, vt)

    n_proj = b * (s // PROJ_ROWS) * PROJ_PHASES
    n_attn = b * (D_ATT // HEAD_PAIR) * ATTN_PHASES
    assert n_proj + n_attn == nb
    n_keep = min(w_buf, s)
    q, k, v, hl, sga, conv_p, h_p, kt_p, vt_p, ya_s0 = _proj_lru(
        x_prompt, w_bf, w_gate, vecs, sample + (0,), n_keep)
    ya, ya_s1 = _prompt_attn(q, k, v, sga, sample + (n_proj,))
    y_p = _finish(ya.reshape(b * s, D_ATT), hl.reshape(b * s, D_LRU),
                  x_prompt.reshape(b * s, D_MODEL), w_o, ln, "prompt_finish")
    window = lambda a: jnp.transpose(a.reshape(b, N_HEADS, HEAD_DIM, n_keep), (0, 3, 1, 2))[None]
    k_p, v_p = window(kt_p), window(vt_p)
    ya_s = jnp.concatenate([ya_s0, ya_s1], axis=1)
    y_s = _finish(ya_s.reshape(t_steps * nb, D_ATT).astype(BF16),
                  hl_s.reshape(t_steps * nb, D_LRU),
                  xs_t.reshape(t_steps * nb, D_MODEL), w_o, ln, "sample_finish")
    to_batch_major = lambda a: jnp.transpose(a, (1, 0, 2))

    return (y_p.reshape(b, s, D_MODEL),
            to_batch_major(y_s.reshape(t_steps, nb, D_MODEL)),
            k_p, v_p,
            conv_p[None], h_p.reshape(1, b, D_LRU),
            heads(to_batch_major(k_s))[None], heads(to_batch_major(v_s))[None],
            to_batch_major(conv_s)[None], h_s[None])
```

```python
import functools

import numpy as np
import jax
import jax.numpy as jnp
from jax import lax
from jax.experimental import pallas as pl
from jax.experimental.pallas import tpu as pltpu

F32 = jnp.float32
BF16 = jnp.bfloat16

D_MODEL = 1024
N_HEADS = 8
HEAD_DIM = 64
D_ATT = N_HEADS * HEAD_DIM
D_LRU = 512
LRU_BLOCKS = 8
D_IN = 4 * D_ATT + 2 * D_LRU
CONV_WIDTH = 4
LRU_C = 8.0
DILATIONS = (1, 4, 16)
N_DIL_KEYS = 128
BAND = 128
ALPHA = 2.0 ** 0.25
LN_EPS = 1e-5
NEG_INF = -1e30
LOG2_E = 1.4426950408889634

C_Q, C_K, C_V, C_GA, C_XL, C_GL = (0, D_ATT, 2 * D_ATT, 3 * D_ATT, 4 * D_ATT,
                                   4 * D_ATT + D_LRU)

VMEM_LIMIT = 60 * 1024 * 1024
PROJ_ROWS = 512
FINISH_ROWS = 1024
HEAD_PAIR = 2 * HEAD_DIM
SUBLANES = 8
CACHE_SLOTS = 3
PROJ_PHASES = 2
ATTN_SAMPLE_ROWS = 2
ATTN_PHASES = 2


def _dot(a, b):
    return jnp.dot(a, b, preferred_element_type=F32)


def _dot_nt(a, b):
    return lax.dot_general(a, b, (((1,), (1,)), ((), ())),
                           preferred_element_type=F32)


def _sigmoid(x):
    return 0.5 * jnp.tanh(0.5 * x) + 0.5


def _silu(x):
    return x * _sigmoid(x)


def _softplus(x):
    return jnp.maximum(x, 0.0) + jnp.log1p(jnp.exp(-jnp.abs(x)))


def _lru_gates(u, wg_ref, vec_ref):
    g = _dot(u.astype(BF16), wg_ref[...])
    tr = jnp.tanh(g[:, :D_LRU] + 0.5 * vec_ref[5:6, :])
    ig = 0.5 * jnp.tanh(g[:, D_LRU:] + 0.5 * vec_ref[6:7, :]) + 0.5
    half_c = (-0.5 * LRU_C) * _softplus(-vec_ref[7:8, :])
    log_a = half_c * tr + half_c
    a = jnp.exp(log_a)
    x = jnp.tanh(-log_a) * (1.0 + a * a)
    bx = jnp.where(x > 0.0, x * lax.rsqrt(x), 0.0) * ig * u
    return a, bx


def _grid_step():
    return ((pl.program_id(0) * pl.num_programs(1) + pl.program_id(1)) * pl.num_programs(2)
            + pl.program_id(2))


def _cache_copies(row, cache_hbm, ring, sems):
    slot = row % CACHE_SLOTS
    return [pltpu.make_async_copy(src.at[row], dst.at[slot], sems.at[which, slot])
            for which, (src, dst) in enumerate(zip(cache_hbm, ring))]


def _sample_attn_step(first, n_rows_hosted, per_step, which, q_ref, kn_ref, vn_ref, sga_ref,
                      cnt_ref, cache_hbm, ring, sems, o_ref):
    t_steps = q_ref.shape[0]
    local = _grid_step() * per_step + which
    lookahead = CACHE_SLOTS - 1
    if which == 0:
        @pl.when(local == 0)
        def _():
            for early in range(lookahead):
                for copy in _cache_copies(first + early, cache_hbm, ring, sems):
                    copy.start()
    for copy in _cache_copies(first + local, cache_hbm, ring, sems):
        copy.wait()

    @pl.when(local + lookahead < n_rows_hosted)
    def _():
        for copy in _cache_copies(first + local + lookahead, cache_hbm, ring, sems):
            copy.start()

    kt_ref, vt_ref = (buf.at[(first + local) % CACHE_SLOTS] for buf in ring)
    mine = pl.ds(local % SUBLANES, 1)
    n_rows = t_steps * N_HEADS
    sub = lax.broadcasted_iota(jnp.int32, (N_HEADS, D_ATT), 0)
    lane_head = lax.broadcasted_iota(jnp.int32, (N_HEADS, D_ATT), 1) // HEAD_DIM
    own = sub == lane_head
    qbd = jnp.concatenate(
        [jnp.where(own, jnp.broadcast_to(q_ref[t, mine, :], (N_HEADS, D_ATT)), 0.0)
         for t in range(t_steps)], axis=0)

    cnt = cnt_ref[...]
    s = jnp.where(cnt > 0.0, _dot(qbd.astype(BF16), kt_ref[...].astype(BF16)), NEG_INF)
    m = jnp.max(s, axis=-1, keepdims=True)

    row_t = lax.broadcasted_iota(jnp.int32, (n_rows, 1), 0) // N_HEADS
    s_new, c_new = [], []
    for t2 in range(t_steps):
        c = jnp.where(row_t == t2, float(len(DILATIONS)),
                      jnp.where(row_t > t2, 1.0, 0.0))
        sn = jnp.sum(qbd * kn_ref[t2, mine, :], axis=-1, keepdims=True)
        sn = jnp.where(c > 0.0, sn, NEG_INF)
        m = jnp.maximum(m, sn)
        s_new.append(sn)
        c_new.append(c)

    p = jnp.exp(s - m) * cnt
    den = jnp.sum(p, axis=-1, keepdims=True)
    acc = _dot_nt(p.astype(BF16), vt_ref[...].astype(BF16))
    for t2 in range(t_steps):
        pn = jnp.exp(s_new[t2] - m) * c_new[t2]
        den = den + pn
        acc = acc + pn * vn_ref[t2, mine, :]
    y = acc / den
    for t in range(t_steps):
        yt = jnp.sum(jnp.where(own, y[t * N_HEADS:(t + 1) * N_HEADS, :], 0.0),
                     axis=0, keepdims=True)
        o_ref[t, mine, :] = yt * sga_ref[t, mine, :]


def _pattern_counts(t_steps, w_buf):
    delta = w_buf + np.arange(t_steps)[:, None] - np.arange(w_buf)[None, :]
    cnt = np.zeros((t_steps, w_buf), np.float32)
    for d in DILATIONS:
        cnt += (delta % d == 0) & (delta // d <= N_DIL_KEYS)
    return np.repeat(cnt, N_HEADS, axis=0)


def _sample_attn_operands(sample, grid, per_step):
    q_s, kn_s, vn_s, sga_s, kt, vt, first = sample
    t_steps = q_s.shape[0]
    w_buf = kt.shape[-1]
    n_local = grid[0] * grid[1] * grid[2] * per_step
    assert first % SUBLANES == 0 and SUBLANES % per_step == 0 and n_local >= CACHE_SLOTS
    cnt = jnp.asarray(_pattern_counts(t_steps, w_buf))
    step = lambda i, j, c: ((i * grid[1] + j) * grid[2] + c) * per_step
    tok_block = (t_steps, SUBLANES, D_ATT)
    tok_in = pl.BlockSpec(tok_block, lambda *g: (0, (first + step(*g)) // SUBLANES, 0))
    in_hbm = pl.BlockSpec(memory_space=pl.ANY)
    cnt_spec = pl.BlockSpec(cnt.shape, lambda *g: (0, 0), pipeline_mode=pl.Buffered(1))
    operands = (q_s, kn_s, vn_s, sga_s, cnt, kt, vt)
    in_specs = [tok_in, tok_in, tok_in, tok_in, cnt_spec, in_hbm, in_hbm]
    out_spec = pl.BlockSpec(tok_block, lambda *g: (0, step(*g) // SUBLANES, 0))
    out_shape = jax.ShapeDtypeStruct((t_steps, n_local, D_ATT), F32)
    scratch = [pltpu.VMEM((CACHE_SLOTS, D_ATT, w_buf), F32),
               pltpu.VMEM((CACHE_SLOTS, D_ATT, w_buf), F32),
               pltpu.SemaphoreType.DMA((2, CACHE_SLOTS))]
    return operands, in_specs, out_spec, out_shape, scratch


def _proj_lru_kernel(first_kept, first_sample, n_hosted, x_ref, w_ref, wg_ref, vec_ref,
                     sq_ref, skn_ref, svn_ref, ssga_ref, cnt_ref, kt_hbm, vt_hbm,
                     q_ref, k_ref, v_ref, hl_ref, sga_ref, conv_ref, hlast_ref,
                     kt_keep_ref, vt_keep_ref, sya_ref,
                     xb_ref, xc_ref, a_ref, b_ref, carry_ref, kt_ring, vt_ring, cache_sems):
    j = pl.program_id(1)
    phase = pl.program_id(2)
    rows = x_ref.shape[0]
    kept = j >= first_kept
    sample_step = functools.partial(_sample_attn_step, first_sample, n_hosted, 1, 0, sq_ref, skn_ref,
                                    svn_ref, ssga_ref, cnt_ref, (kt_hbm, vt_hbm),
                                    (kt_ring, vt_ring), cache_sems, sya_ref)

    @pl.when(phase == 0)
    def _():
        @pl.when(j == 0)
        def _():
            xc_ref[...] = jnp.zeros((8, D_LRU), F32)
            carry_ref[...] = jnp.zeros((1, D_LRU), F32)

        sample_step()
        xb_ref[...] = x_ref[...].astype(BF16)
        xl = _dot(xb_ref[...], w_ref[:, C_XL:C_GL])
        tail = xc_ref[...]
        xc_ref[...] = xl[rows - 8:, :]
        row8 = lax.broadcasted_iota(jnp.int32, (8, D_LRU), 0)
        u = vec_ref[4:5, :] + xl * vec_ref[3:4, :]
        for s in range(1, CONV_WIDTH):
            rolled = pltpu.roll(xl, s, axis=0)
            head = jnp.where(row8 < s, pltpu.roll(tail, s, axis=0), rolled[0:8, :])
            u = u + jnp.concatenate([head, rolled[8:, :]], axis=0) * vec_ref[3 - s:4 - s, :]
        a, bx = _lru_gates(u, wg_ref, vec_ref)
        a_ref[...] = a
        b_ref[...] = bx
        q_ref[...] = _dot(xb_ref[...], w_ref[:, C_Q:C_K]) * LOG2_E
        k_new = _dot(xb_ref[...], w_ref[:, C_K:C_V])
        k_ref[...] = k_new
        v_new = _dot(xb_ref[...], w_ref[:, C_V:C_GA])
        v_ref[...] = v_new

        @pl.when(kept)
        def _():
            kt_keep_ref[...] = k_new.T
            vt_keep_ref[...] = v_new.T

    @pl.when(phase == 1)
    def _():
        sample_step()
        sga_ref[...] = _silu(_dot(xb_ref[...], w_ref[:, C_GA:C_XL])).astype(BF16)
        sgl = _silu(_dot(xb_ref[...], w_ref[:, C_GL:D_IN]))

        row = lax.broadcasted_iota(jnp.int32, (8, D_LRU), 0)

        def group(g, carry):
            r0 = pl.multiple_of(g * 8, 8)
            av = a_ref[pl.ds(r0, 8), :]
            bv = b_ref[pl.ds(r0, 8), :]
            for s in (1, 2, 4):
                keep = row >= s
                bv = jnp.where(keep, av * pltpu.roll(bv, s, axis=0) + bv, bv)
                av = jnp.where(keep, av * pltpu.roll(av, s, axis=0), av)
            h8 = av * carry + bv
            b_ref[pl.ds(r0, 8), :] = h8
            return h8[7:8, :]

        carry = lax.fori_loop(0, rows // 8, group, carry_ref[...], unroll=True)
        carry_ref[...] = carry
        hl_ref[...] = (b_ref[...] * sgl).astype(BF16)

        @pl.when(j == pl.num_programs(1) - 1)
        def _():
            conv_ref[...] = xc_ref[5:8, :]
            hlast_ref[...] = carry


def _proj_lru(x, w_bf, w_gate, vecs, sample, n_keep):
    b, s, _ = x.shape
    nt = s // PROJ_ROWS
    assert n_keep % PROJ_ROWS == 0
    first_kept = nt - n_keep // PROJ_ROWS
    keep_spec = pl.BlockSpec((None, D_ATT, PROJ_ROWS),
                             lambda i, j, c: (i, 0, jnp.maximum(j - first_kept, 0)))
    keep_shape = jax.ShapeDtypeStruct((b, D_ATT, n_keep), F32)
    grid = (b, nt, PROJ_PHASES)
    s_ops, s_in, s_out, s_shape, s_scratch = _sample_attn_operands(sample, grid, 1)
    row_spec = lambda width: pl.BlockSpec((None, PROJ_ROWS, width), lambda i, j, c: (i, j, 0))
    const = lambda shape: pl.BlockSpec(shape, lambda i, j, c: (0,) * len(shape),
                                       pipeline_mode=pl.Buffered(1))
    per_batch = lambda n: pl.BlockSpec((None, n, D_LRU), lambda i, j, c: (i, 0, 0))
    tile = lambda dt: jax.ShapeDtypeStruct((b, s, D_ATT), dt)
    return pl.pallas_call(
        functools.partial(_proj_lru_kernel, first_kept, sample[-1], b * nt * PROJ_PHASES),
        grid=grid,
        in_specs=[row_spec(D_MODEL), const((D_MODEL, D_IN)),
                  const((D_LRU, 2 * D_LRU)), const((8, D_LRU))] + s_in,
        out_specs=[row_spec(D_ATT), row_spec(D_ATT), row_spec(D_ATT),
                   row_spec(D_LRU), row_spec(D_ATT),
                   per_batch(CONV_WIDTH - 1), per_batch(1), keep_spec, keep_spec, s_out],
        out_shape=[tile(F32), tile(F32), tile(F32), tile(BF16), tile(BF16),
                   jax.ShapeDtypeStruct((b, CONV_WIDTH - 1, D_LRU), F32),
                   jax.ShapeDtypeStruct((b, 1, D_LRU), F32), keep_shape, keep_shape, s_shape],
        scratch_shapes=[pltpu.VMEM((PROJ_ROWS, D_MODEL), BF16),
                        pltpu.VMEM((8, D_LRU), F32),
                        pltpu.VMEM((PROJ_ROWS, D_LRU), F32),
                        pltpu.VMEM((PROJ_ROWS, D_LRU), F32),
                        pltpu.VMEM((1, D_LRU), F32)] + s_scratch,
        compiler_params=pltpu.CompilerParams(
            dimension_semantics=("arbitrary", "arbitrary", "arbitrary"),
            vmem_limit_bytes=VMEM_LIMIT),
        name="prompt_proj_lru",
    )(x, w_bf, w_gate, vecs, *s_ops)


def _prompt_attn_kernel(first_sample, n_hosted, q_ref, k_ref, v_ref, sga_ref,
                        sq_ref, skn_ref, svn_ref, ssga_ref, cnt_ref, kt_hbm, vt_hbm,
                        o_ref, sya_ref, bias_ref, *scratch):
    seq = q_ref.shape[0]
    phase = pl.program_id(2)
    n_groups = seq // BAND
    head_a = lax.broadcasted_iota(jnp.int32, (BAND, HEAD_PAIR), 1) < HEAD_DIM
    one = jnp.ones((), BF16)

    stats = scratch[:6]

    def sample_steps():
        for which in range(ATTN_SAMPLE_ROWS):
            _sample_attn_step(first_sample, n_hosted, ATTN_SAMPLE_ROWS, which, sq_ref, skn_ref,
                              svn_ref, ssga_ref, cnt_ref, (kt_hbm, vt_hbm), scratch[6:8],
                              scratch[8], sya_ref)

    def key_value_block(rows):
        v_bf = v_ref[rows, :].astype(BF16)
        v_ext = jnp.concatenate([jnp.where(head_a, v_bf, one),
                                 jnp.where(head_a, one, v_bf)], axis=1)
        return k_ref[rows, :].astype(BF16), v_ext

    def group(rows, blk, carry):
        k_prev, v_prev = carry
        k_cur, v_cur = key_value_block(rows)
        k2 = jnp.concatenate([k_prev, k_cur], axis=0)
        v4 = jnp.concatenate([v_prev, v_cur], axis=0)
        qf = q_ref[rows, :]
        q2 = jnp.concatenate([jnp.where(head_a, qf, 0.0),
                              jnp.where(head_a, 0.0, qf)], axis=0).astype(BF16)
        s = _dot_nt(q2, k2).reshape(2, BAND, 2 * BAND)
        s = s + bias_ref[jnp.minimum(blk, 1)][None]
        m = jnp.max(s, axis=-1, keepdims=True)
        p = jnp.exp2(s - m).reshape(2 * BAND, 2 * BAND).astype(BF16)
        o = _dot(p, v4)
        acc = jnp.where(head_a, o[:BAND, :HEAD_PAIR], o[BAND:, HEAD_PAIR:])
        den = jnp.where(head_a, o[:BAND, HEAD_PAIR:], o[BAND:, :HEAD_PAIR])
        return acc, den, jnp.where(head_a, m[0], m[1]), (k_cur, v_cur)

    zero_block = (jnp.zeros((BAND, HEAD_PAIR), BF16), jnp.zeros((BAND, 2 * HEAD_PAIR), BF16))

    def strided_pattern(d, first, count, acc_ref, l_ref, m_ref):
        span = BAND * d
        n_blk = seq // span

        def body(i, carry):
            g = first + i
            blk = g % n_blk
            rows = pl.ds(blk * span + g // n_blk, BAND, stride=d)
            acc, den, m, carry = group(rows, blk, carry)
            acc_ref[rows, :] = acc
            l_ref[rows, :] = den
            m_ref[rows, :] = m
            return carry

        lax.fori_loop(0, count, body, zero_block, unroll=True)

    half = n_groups // 2

    @pl.when(phase == 0)
    def _():
        row = lax.broadcasted_iota(jnp.int32, (BAND, 2 * BAND), 0)
        col = lax.broadcasted_iota(jnp.int32, (BAND, 2 * BAND), 1)
        band = (col >= row) & (col <= row + N_DIL_KEYS)
        bias_ref[0] = jnp.where(band & (col >= BAND), 0.0, NEG_INF)
        bias_ref[1] = jnp.where(band, 0.0, NEG_INF)
        sample_steps()
        strided_pattern(DILATIONS[1], 0, n_groups, *stats[0:3])
        strided_pattern(DILATIONS[2], 0, half, *stats[3:6])

    @pl.when(phase == 1)
    def _():
        sample_steps()
        strided_pattern(DILATIONS[2], half, half, *stats[3:6])

        def body(g, carry):
            rows = pl.ds(pl.multiple_of(g * BAND, BAND), BAND)
            acc, den, m, carry = group(rows, g, carry)
            parts = [(acc, den, m)] + [(stats[3 * t][rows, :], stats[3 * t + 1][rows, :],
                                        stats[3 * t + 2][rows, :]) for t in range(2)]
            m_max = jnp.maximum(jnp.maximum(parts[0][2], parts[1][2]), parts[2][2])
            e = [jnp.exp2(pt[2] - m_max) for pt in parts]
            num = sum(ei * pt[0] for ei, pt in zip(e, parts))
            tot = sum(ei * pt[1] for ei, pt in zip(e, parts))
            o_ref[rows, :] = (num / tot * sga_ref[rows, :].astype(F32)).astype(BF16)
            return carry

        lax.fori_loop(0, n_groups, body, zero_block, unroll=True)


def _prompt_attn(q, k, v, sga, sample):
    b, s, _ = q.shape
    grid = (b, D_ATT // HEAD_PAIR, ATTN_PHASES)
    s_ops, s_in, s_out, s_shape, s_scratch = _sample_attn_operands(sample, grid, ATTN_SAMPLE_ROWS)
    spec = pl.BlockSpec((None, s, HEAD_PAIR), lambda i, j, c: (i, 0, j))
    return pl.pallas_call(
        functools.partial(_prompt_attn_kernel, sample[-1],
                          grid[0] * grid[1] * grid[2] * ATTN_SAMPLE_ROWS),
        grid=grid,
        in_specs=[spec, spec, spec, spec] + s_in,
        out_specs=[spec, s_out],
        out_shape=[jax.ShapeDtypeStruct((b, s, D_ATT), BF16), s_shape],
        scratch_shapes=[pltpu.VMEM((2, BAND, 2 * BAND), F32)]
                       + [pltpu.VMEM((s, HEAD_PAIR), F32) for _ in range(6)] + s_scratch,
        compiler_params=pltpu.CompilerParams(
            dimension_semantics=("arbitrary", "arbitrary", "arbitrary"),
            vmem_limit_bytes=VMEM_LIMIT),
        name="prompt_attn",
    )(q, k, v, sga, *s_ops)


def _finish_kernel(ya_ref, hl_ref, x_ref, wo_ref, ln_ref, o_ref):
    sub = (_dot(ya_ref[...], wo_ref[0:D_ATT, :])
           + _dot(hl_ref[...], wo_ref[D_ATT:D_ATT + D_LRU, :]))
    z = ALPHA * x_ref[...] + sub
    mu = jnp.mean(z, axis=-1, keepdims=True)
    zc = z - mu
    var = jnp.mean(zc * zc, axis=-1, keepdims=True)
    o_ref[...] = zc * lax.rsqrt(var + LN_EPS) * ln_ref[0:1, :] + ln_ref[1:2, :]


def _finish(ya, hl, x, w_o, ln, name):
    n = x.shape[0]
    rows = min(n, FINISH_ROWS)
    row_spec = lambda width: pl.BlockSpec((rows, width), lambda i: (i, 0))
    const = lambda shape: pl.BlockSpec(shape, lambda i: (0, 0),
                                       pipeline_mode=pl.Buffered(1))
    return pl.pallas_call(
        _finish_kernel,
        grid=(n // rows,),
        in_specs=[row_spec(D_ATT), row_spec(D_LRU), row_spec(D_MODEL),
                  const((D_ATT + D_LRU, D_MODEL)), const((2, D_MODEL))],
        out_specs=row_spec(D_MODEL),
        out_shape=jax.ShapeDtypeStruct((n, D_MODEL), F32),
        compiler_params=pltpu.CompilerParams(
            dimension_semantics=("arbitrary",), vmem_limit_bytes=VMEM_LIMIT),
        name=name,
    )(ya, hl, x, w_o, ln)


def _sample_proj_kernel(x_ref, sc_ref, sh_ref, w_ref, wg_ref, vec_ref,
                        q_ref, k_ref, v_ref, hl_ref, sga_ref, conv_ref, hlast_ref):
    t_steps, nb, _ = x_ref.shape
    xb = x_ref[...].reshape(t_steps * nb, D_MODEL).astype(BF16)
    split = lambda y: y.reshape(t_steps, nb, y.shape[-1])
    q_ref[...] = split(_dot(xb, w_ref[:, C_Q:C_K]))
    k_ref[...] = split(_dot(xb, w_ref[:, C_K:C_V]))
    v_ref[...] = split(_dot(xb, w_ref[:, C_V:C_GA]))
    sga_ref[...] = split(_silu(_dot(xb, w_ref[:, C_GA:C_XL])))

    xl = _dot(xb, w_ref[:, C_XL:C_GL])
    xc = [sc_ref[i] for i in range(CONV_WIDTH - 1)]
    xc += [xl[t * nb:(t + 1) * nb, :] for t in range(t_steps)]
    u = jnp.concatenate(
        [vec_ref[4:5, :] + sum(xc[t + w] * vec_ref[w:w + 1, :] for w in range(CONV_WIDTH))
         for t in range(t_steps)], axis=0)
    for i in range(CONV_WIDTH - 1):
        conv_ref[i] = xc[t_steps + i]

    a, bx = _lru_gates(u, wg_ref, vec_ref)
    sgl = _silu(_dot(xb, w_ref[:, C_GL:D_IN]))
    h = sh_ref[...]
    for t in range(t_steps):
        rows = slice(t * nb, (t + 1) * nb)
        h = a[rows, :] * h + bx[rows, :]
        hl_ref[t] = (h * sgl[rows, :]).astype(BF16)
    hlast_ref[...] = h


def _sample_proj(xs_t, sc_t, sh, w_bf, w_gate, vecs):
    t_steps, nb, _ = xs_t.shape
    full = lambda shape: pl.BlockSpec(shape, lambda i: (0,) * len(shape))
    tok = lambda dt: jax.ShapeDtypeStruct((t_steps, nb, D_ATT), dt)
    return pl.pallas_call(
        _sample_proj_kernel,
        grid=(1,),
        in_specs=[full(xs_t.shape), full(sc_t.shape), full(sh.shape),
                  full(w_bf.shape), full(w_gate.shape), full(vecs.shape)],
        out_specs=[full((t_steps, nb, D_ATT))] * 5
                  + [full(sc_t.shape), full(sh.shape)],
        out_shape=[tok(F32), tok(F32), tok(F32), tok(BF16), tok(F32),
                   jax.ShapeDtypeStruct(sc_t.shape, F32),
                   jax.ShapeDtypeStruct(sh.shape, F32)],
        compiler_params=pltpu.CompilerParams(
            dimension_semantics=("arbitrary",), vmem_limit_bytes=VMEM_LIMIT),
        name="sample_proj_lru",
    )(xs_t, sc_t, sh, w_bf, w_gate, vecs)


def kernel(x_prompt, x_sample, cache_k, cache_v, state_conv, state_h, w_in, conv_w, conv_b, w_ra, b_ra, w_ri, b_ri, lru_lambda, w_out, ln_g, ln_b):
    assert w_in.shape[0] == 1, "single layer"
    b, s, _ = x_prompt.shape
    nb, t_steps, _ = x_sample.shape
    w_buf = cache_k.shape[2]

    col_scale = jnp.concatenate([jnp.full((D_ATT,), HEAD_DIM ** -0.5, F32),
                                 jnp.ones((D_IN - D_ATT,), F32)])
    w_bf = (w_in[0] * col_scale).astype(BF16)
    eye = jnp.eye(LRU_BLOCKS, dtype=F32)
    block_diag = lambda wb: (eye[:, None, :, None] * wb[:, :, None, :]).reshape(D_LRU, D_LRU)
    w_gate = (0.5 * jnp.concatenate([block_diag(w_ra[0]), block_diag(w_ri[0])], axis=1)).astype(BF16)
    vecs = jnp.concatenate([conv_w[0], conv_b, b_ra, b_ri, lru_lambda], axis=0)
    w_o = w_out[0].astype(BF16)
    ln = jnp.concatenate([ln_g, ln_b], axis=0)
    heads = lambda a: a.reshape(a.shape[:-1] + (N_HEADS, HEAD_DIM))

    xs_t = jnp.transpose(x_sample, (1, 0, 2))
    sc_t = jnp.transpose(state_conv[0], (1, 0, 2))
    q_s, k_s, v_s, hl_s, sga_s, conv_s, h_s = _sample_proj(
        xs_t, sc_t, state_h[0], w_bf, w_gate, vecs)
    kt = jnp.transpose(cache_k[0], (0, 2, 3, 1)).reshape(nb, D_ATT, w_buf)
    vt = jnp.transpose(cache_v[0], (0, 2, 3, 1)).reshape(nb, D_ATT, w_buf)
    sample = (q_s, k_s, v_s, sga_s, kt, vt)

    n_proj = b * (s // PROJ_ROWS) * PROJ_PHASES
    n_attn = b * (D_ATT // HEAD_PAIR) * ATTN_PHASES * ATTN_SAMPLE_ROWS
    assert n_proj + n_attn == nb
    n_keep = min(w_buf, s)
    q, k, v, hl, sga, conv_p, h_p, kt_p, vt_p, ya_s0 = _proj_lru(
        x_prompt, w_bf, w_gate, vecs, sample + (0,), n_keep)
    ya, ya_s1 = _prompt_attn(q, k, v, sga, sample + (n_proj,))
    y_p = _finish(ya.reshape(b * s, D_ATT), hl.reshape(b * s, D_LRU),
                  x_prompt.reshape(b * s, D_MODEL), w_o, ln, "prompt_finish")
    window = lambda a: jnp.transpose(a.reshape(b, N_HEADS, HEAD_DIM, n_keep), (0, 3, 1, 2))[None]
    k_p, v_p = window(kt_p), window(vt_p)
    ya_s = jnp.concatenate([ya_s0, ya_s1], axis=1)
    y_s = _finish(ya_s.reshape(t_steps * nb, D_ATT).astype(BF16),
                  hl_s.reshape(t_steps * nb, D_LRU),
                  xs_t.reshape(t_steps * nb, D_MODEL), w_o, ln, "sample_finish")
    to_batch_major = lambda a: jnp.transpose(a, (1, 0, 2))

    return (y_p.reshape(b, s, D_MODEL),
            to_batch_major(y_s.reshape(t_steps, nb, D_MODEL)),
            k_p, v_p,
            conv_p[None], h_p.reshape(1, b, D_LRU),
            heads(to_batch_major(k_s))[None], heads(to_batch_major(v_s))[None],
            to_batch_major(conv_s)[None], h_s[None])
```

```python
import functools

import numpy as np
import jax
import jax.numpy as jnp
from jax import lax
from jax.experimental import pallas as pl
from jax.experimental.pallas import tpu as pltpu

F32 = jnp.float32
BF16 = jnp.bfloat16

D_MODEL = 1024
N_HEADS = 8
HEAD_DIM = 64
D_ATT = N_HEADS * HEAD_DIM
D_LRU = 512
LRU_BLOCKS = 8
D_IN = 4 * D_ATT + 2 * D_LRU
CONV_WIDTH = 4
LRU_C = 8.0
DILATIONS = (1, 4, 16)
N_DIL_KEYS = 128
BAND = 128
ALPHA = 2.0 ** 0.25
LN_EPS = 1e-5
NEG_INF = -1e30
LOG2_E = 1.4426950408889634

C_Q, C_K, C_V, C_GA, C_XL, C_GL = (0, D_ATT, 2 * D_ATT, 3 * D_ATT, 4 * D_ATT,
                                   4 * D_ATT + D_LRU)

VMEM_LIMIT = 60 * 1024 * 1024
PROJ_ROWS = 512
FINISH_ROWS = 1024
HEAD_PAIR = 2 * HEAD_DIM
SUBLANES = 8
CACHE_SLOTS = 3
PROJ_PHASES = 2
ATTN_SAMPLE_ROWS = 4


def _dot(a, b):
    return jnp.dot(a, b, preferred_element_type=F32)


def _dot_nt(a, b):
    return lax.dot_general(a, b, (((1,), (1,)), ((), ())),
                           preferred_element_type=F32)


def _sigmoid(x):
    return 0.5 * jnp.tanh(0.5 * x) + 0.5


def _silu(x):
    return x * _sigmoid(x)


def _softplus(x):
    return jnp.maximum(x, 0.0) + jnp.log1p(jnp.exp(-jnp.abs(x)))


def _lru_gates(u, wg_ref, vec_ref):
    g = _dot(u.astype(BF16), wg_ref[...])
    tr = jnp.tanh(g[:, :D_LRU] + 0.5 * vec_ref[5:6, :])
    ig = 0.5 * jnp.tanh(g[:, D_LRU:] + 0.5 * vec_ref[6:7, :]) + 0.5
    half_c = (-0.5 * LRU_C) * _softplus(-vec_ref[7:8, :])
    log_a = half_c * tr + half_c
    a = jnp.exp(log_a)
    x = jnp.tanh(-log_a) * (1.0 + a * a)
    bx = jnp.where(x > 0.0, x * lax.rsqrt(x), 0.0) * ig * u
    return a, bx


def _grid_step():
    return ((pl.program_id(0) * pl.num_programs(1) + pl.program_id(1)) * pl.num_programs(2)
            + pl.program_id(2))


def _cache_copies(row, cache_hbm, ring, sems):
    slot = row % CACHE_SLOTS
    return [pltpu.make_async_copy(src.at[row], dst.at[slot], sems.at[which, slot])
            for which, (src, dst) in enumerate(zip(cache_hbm, ring))]


def _sample_attn_step(first, n_rows_hosted, per_step, which, q_ref, kn_ref, vn_ref, sga_ref,
                      cnt_ref, cache_hbm, ring, sems, o_ref):
    t_steps = q_ref.shape[0]
    local = _grid_step() * per_step + which
    lookahead = CACHE_SLOTS - 1
    if which == 0:
        @pl.when(local == 0)
        def _():
            for early in range(lookahead):
                for copy in _cache_copies(first + early, cache_hbm, ring, sems):
                    copy.start()
    for copy in _cache_copies(first + local, cache_hbm, ring, sems):
        copy.wait()

    @pl.when(local + lookahead < n_rows_hosted)
    def _():
        for copy in _cache_copies(first + local + lookahead, cache_hbm, ring, sems):
            copy.start()

    kt_ref, vt_ref = (buf.at[(first + local) % CACHE_SLOTS] for buf in ring)
    mine = pl.ds(local % SUBLANES, 1)
    n_rows = t_steps * N_HEADS
    sub = lax.broadcasted_iota(jnp.int32, (N_HEADS, D_ATT), 0)
    lane_head = lax.broadcasted_iota(jnp.int32, (N_HEADS, D_ATT), 1) // HEAD_DIM
    own = sub == lane_head
    qbd = jnp.concatenate(
        [jnp.where(own, jnp.broadcast_to(q_ref[t, mine, :], (N_HEADS, D_ATT)), 0.0)
         for t in range(t_steps)], axis=0)

    cnt = cnt_ref[...]
    s = jnp.where(cnt > 0.0, _dot(qbd.astype(BF16), kt_ref[...].astype(BF16)), NEG_INF)
    m = jnp.max(s, axis=-1, keepdims=True)

    row_t = lax.broadcasted_iota(jnp.int32, (n_rows, 1), 0) // N_HEADS
    s_new, c_new = [], []
    for t2 in range(t_steps):
        c = jnp.where(row_t == t2, float(len(DILATIONS)),
                      jnp.where(row_t > t2, 1.0, 0.0))
        sn = jnp.sum(qbd * kn_ref[t2, mine, :], axis=-1, keepdims=True)
        sn = jnp.where(c > 0.0, sn, NEG_INF)
        m = jnp.maximum(m, sn)
        s_new.append(sn)
        c_new.append(c)

    p = jnp.exp(s - m) * cnt
    den = jnp.sum(p, axis=-1, keepdims=True)
    acc = _dot_nt(p.astype(BF16), vt_ref[...].astype(BF16))
    for t2 in range(t_steps):
        pn = jnp.exp(s_new[t2] - m) * c_new[t2]
        den = den + pn
        acc = acc + pn * vn_ref[t2, mine, :]
    y = acc / den
    for t in range(t_steps):
        yt = jnp.sum(jnp.where(own, y[t * N_HEADS:(t + 1) * N_HEADS, :], 0.0),
                     axis=0, keepdims=True)
        o_ref[t, mine, :] = yt * sga_ref[t, mine, :]


def _pattern_counts(t_steps, w_buf):
    delta = w_buf + np.arange(t_steps)[:, None] - np.arange(w_buf)[None, :]
    cnt = np.zeros((t_steps, w_buf), np.float32)
    for d in DILATIONS:
        cnt += (delta % d == 0) & (delta // d <= N_DIL_KEYS)
    return np.repeat(cnt, N_HEADS, axis=0)


def _sample_attn_operands(sample, grid, per_step):
    q_s, kn_s, vn_s, sga_s, kt, vt, first = sample
    t_steps = q_s.shape[0]
    w_buf = kt.shape[-1]
    n_local = grid[0] * grid[1] * grid[2] * per_step
    assert first % SUBLANES == 0 and SUBLANES % per_step == 0 and n_local >= CACHE_SLOTS
    cnt = jnp.asarray(_pattern_counts(t_steps, w_buf))
    step = lambda i, j, c: ((i * grid[1] + j) * grid[2] + c) * per_step
    tok_block = (t_steps, SUBLANES, D_ATT)
    tok_in = pl.BlockSpec(tok_block, lambda *g: (0, (first + step(*g)) // SUBLANES, 0))
    in_hbm = pl.BlockSpec(memory_space=pl.ANY)
    cnt_spec = pl.BlockSpec(cnt.shape, lambda *g: (0, 0), pipeline_mode=pl.Buffered(1))
    operands = (q_s, kn_s, vn_s, sga_s, cnt, kt, vt)
    in_specs = [tok_in, tok_in, tok_in, tok_in, cnt_spec, in_hbm, in_hbm]
    out_spec = pl.BlockSpec(tok_block, lambda *g: (0, step(*g) // SUBLANES, 0))
    out_shape = jax.ShapeDtypeStruct((t_steps, n_local, D_ATT), F32)
    scratch = [pltpu.VMEM((CACHE_SLOTS, D_ATT, w_buf), F32),
               pltpu.VMEM((CACHE_SLOTS, D_ATT, w_buf), F32),
               pltpu.SemaphoreType.DMA((2, CACHE_SLOTS))]
    return operands, in_specs, out_spec, out_shape, scratch


def _proj_lru_kernel(first_kept, first_sample, n_hosted, x_ref, w_ref, wg_ref, vec_ref,
                     sq_ref, skn_ref, svn_ref, ssga_ref, cnt_ref, kt_hbm, vt_hbm,
                     q_ref, k_ref, v_ref, hl_ref, sga_ref, conv_ref, hlast_ref,
                     kt_keep_ref, vt_keep_ref, sya_ref,
                     xb_ref, xc_ref, a_ref, b_ref, carry_ref, kt_ring, vt_ring, cache_sems):
    j = pl.program_id(1)
    phase = pl.program_id(2)
    rows = x_ref.shape[0]
    kept = j >= first_kept
    sample_step = functools.partial(_sample_attn_step, first_sample, n_hosted, 1, 0, sq_ref, skn_ref,
                                    svn_ref, ssga_ref, cnt_ref, (kt_hbm, vt_hbm),
                                    (kt_ring, vt_ring), cache_sems, sya_ref)

    @pl.when(phase == 0)
    def _():
        @pl.when(j == 0)
        def _():
            xc_ref[...] = jnp.zeros((8, D_LRU), F32)
            carry_ref[...] = jnp.zeros((1, D_LRU), F32)

        sample_step()
        xb_ref[...] = x_ref[...].astype(BF16)
        xl = _dot(xb_ref[...], w_ref[:, C_XL:C_GL])
        tail = xc_ref[...]
        xc_ref[...] = xl[rows - 8:, :]
        row8 = lax.broadcasted_iota(jnp.int32, (8, D_LRU), 0)
        u = vec_ref[4:5, :] + xl * vec_ref[3:4, :]
        for s in range(1, CONV_WIDTH):
            rolled = pltpu.roll(xl, s, axis=0)
            head = jnp.where(row8 < s, pltpu.roll(tail, s, axis=0), rolled[0:8, :])
            u = u + jnp.concatenate([head, rolled[8:, :]], axis=0) * vec_ref[3 - s:4 - s, :]
        a, bx = _lru_gates(u, wg_ref, vec_ref)
        a_ref[...] = a
        b_ref[...] = bx
        q_ref[...] = _dot(xb_ref[...], w_ref[:, C_Q:C_K]) * LOG2_E
        k_new = _dot(xb_ref[...], w_ref[:, C_K:C_V])
        k_ref[...] = k_new
        v_new = _dot(xb_ref[...], w_ref[:, C_V:C_GA])
        v_ref[...] = v_new

        @pl.when(kept)
        def _():
            kt_keep_ref[...] = k_new.T
            vt_keep_ref[...] = v_new.T

    @pl.when(phase == 1)
    def _():
        sample_step()
        sga_ref[...] = _silu(_dot(xb_ref[...], w_ref[:, C_GA:C_XL])).astype(BF16)
        sgl = _silu(_dot(xb_ref[...], w_ref[:, C_GL:D_IN]))

        row = lax.broadcasted_iota(jnp.int32, (8, D_LRU), 0)

        def group(g, carry):
            r0 = pl.multiple_of(g * 8, 8)
            av = a_ref[pl.ds(r0, 8), :]
            bv = b_ref[pl.ds(r0, 8), :]
            for s in (1, 2, 4):
                keep = row >= s
                bv = jnp.where(keep, av * pltpu.roll(bv, s, axis=0) + bv, bv)
                av = jnp.where(keep, av * pltpu.roll(av, s, axis=0), av)
            h8 = av * carry + bv
            b_ref[pl.ds(r0, 8), :] = h8
            return h8[7:8, :]

        carry = lax.fori_loop(0, rows // 8, group, carry_ref[...], unroll=True)
        carry_ref[...] = carry
        hl_ref[...] = (b_ref[...] * sgl).astype(BF16)

        @pl.when(j == pl.num_programs(1) - 1)
        def _():
            conv_ref[...] = xc_ref[5:8, :]
            hlast_ref[...] = carry


def _proj_lru(x, w_bf, w_gate, vecs, sample, n_keep):
    b, s, _ = x.shape
    nt = s // PROJ_ROWS
    assert n_keep % PROJ_ROWS == 0
    first_kept = nt - n_keep // PROJ_ROWS
    keep_spec = pl.BlockSpec((None, D_ATT, PROJ_ROWS),
                             lambda i, j, c: (i, 0, jnp.maximum(j - first_kept, 0)))
    keep_shape = jax.ShapeDtypeStruct((b, D_ATT, n_keep), F32)
    grid = (b, nt, PROJ_PHASES)
    s_ops, s_in, s_out, s_shape, s_scratch = _sample_attn_operands(sample, grid, 1)
    row_spec = lambda width: pl.BlockSpec((None, PROJ_ROWS, width), lambda i, j, c: (i, j, 0))
    const = lambda shape: pl.BlockSpec(shape, lambda i, j, c: (0,) * len(shape),
                                       pipeline_mode=pl.Buffered(1))
    per_batch = lambda n: pl.BlockSpec((None, n, D_LRU), lambda i, j, c: (i, 0, 0))
    tile = lambda dt: jax.ShapeDtypeStruct((b, s, D_ATT), dt)
    return pl.pallas_call(
        functools.partial(_proj_lru_kernel, first_kept, sample[-1], b * nt * PROJ_PHASES),
        grid=grid,
        in_specs=[row_spec(D_MODEL), const((D_MODEL, D_IN)),
                  const((D_LRU, 2 * D_LRU)), const((8, D_LRU))] + s_in,
        out_specs=[row_spec(D_ATT), row_spec(D_ATT), row_spec(D_ATT),
                   row_spec(D_LRU), row_spec(D_ATT),
                   per_batch(CONV_WIDTH - 1), per_batch(1), keep_spec, keep_spec, s_out],
        out_shape=[tile(F32), tile(F32), tile(F32), tile(BF16), tile(BF16),
                   jax.ShapeDtypeStruct((b, CONV_WIDTH - 1, D_LRU), F32),
                   jax.ShapeDtypeStruct((b, 1, D_LRU), F32), keep_shape, keep_shape, s_shape],
        scratch_shapes=[pltpu.VMEM((PROJ_ROWS, D_MODEL), BF16),
                        pltpu.VMEM((8, D_LRU), F32),
                        pltpu.VMEM((PROJ_ROWS, D_LRU), F32),
                        pltpu.VMEM((PROJ_ROWS, D_LRU), F32),
                        pltpu.VMEM((1, D_LRU), F32)] + s_scratch,
        compiler_params=pltpu.CompilerParams(
            dimension_semantics=("arbitrary", "arbitrary", "arbitrary"),
            vmem_limit_bytes=VMEM_LIMIT),
        name="prompt_proj_lru",
    )(x, w_bf, w_gate, vecs, *s_ops)


def _prompt_attn_kernel(first_sample, n_hosted, q_ref, k_ref, v_ref, sga_ref,
                        sq_ref, skn_ref, svn_ref, ssga_ref, cnt_ref, kt_hbm, vt_hbm,
                        o_ref, sya_ref, bias_ref, *scratch):
    seq = q_ref.shape[0]
    n_groups = seq // BAND
    head_a = lax.broadcasted_iota(jnp.int32, (BAND, HEAD_PAIR), 1) < HEAD_DIM
    one = jnp.ones((), BF16)

    stats = scratch[:6]

    def key_value_block(rows):
        v_bf = v_ref[rows, :].astype(BF16)
        v_ext = jnp.concatenate([jnp.where(head_a, v_bf, one),
                                 jnp.where(head_a, one, v_bf)], axis=1)
        return k_ref[rows, :].astype(BF16), v_ext

    def group(rows, blk, carry):
        k_prev, v_prev = carry
        k_cur, v_cur = key_value_block(rows)
        k2 = jnp.concatenate([k_prev, k_cur], axis=0)
        v4 = jnp.concatenate([v_prev, v_cur], axis=0)
        qf = q_ref[rows, :]
        q2 = jnp.concatenate([jnp.where(head_a, qf, 0.0),
                              jnp.where(head_a, 0.0, qf)], axis=0).astype(BF16)
        s = _dot_nt(q2, k2).reshape(2, BAND, 2 * BAND)
        s = s + bias_ref[jnp.minimum(blk, 1)][None]
        m = jnp.max(s, axis=-1, keepdims=True)
        p = jnp.exp2(s - m).reshape(2 * BAND, 2 * BAND).astype(BF16)
        o = _dot(p, v4)
        acc = jnp.where(head_a, o[:BAND, :HEAD_PAIR], o[BAND:, HEAD_PAIR:])
        den = jnp.where(head_a, o[:BAND, HEAD_PAIR:], o[BAND:, :HEAD_PAIR])
        return acc, den, jnp.where(head_a, m[0], m[1]), (k_cur, v_cur)

    zero_block = (jnp.zeros((BAND, HEAD_PAIR), BF16), jnp.zeros((BAND, 2 * HEAD_PAIR), BF16))

    def strided_pattern(d, first, count, acc_ref, l_ref, m_ref):
        span = BAND * d
        n_blk = seq // span

        def body(i, carry):
            g = first + i
            blk = g % n_blk
            rows = pl.ds(blk * span + g // n_blk, BAND, stride=d)
            acc, den, m, carry = group(rows, blk, carry)
            acc_ref[rows, :] = acc
            l_ref[rows, :] = den
            m_ref[rows, :] = m
            return carry

        lax.fori_loop(0, count, body, zero_block, unroll=True)

    row = lax.broadcasted_iota(jnp.int32, (BAND, 2 * BAND), 0)
    col = lax.broadcasted_iota(jnp.int32, (BAND, 2 * BAND), 1)
    band = (col >= row) & (col <= row + N_DIL_KEYS)
    bias_ref[0] = jnp.where(band & (col >= BAND), 0.0, NEG_INF)
    bias_ref[1] = jnp.where(band, 0.0, NEG_INF)

    sample_row = functools.partial(_sample_attn_step, first_sample, n_hosted, ATTN_SAMPLE_ROWS)
    sample_refs = (sq_ref, skn_ref, svn_ref, ssga_ref, cnt_ref, (kt_hbm, vt_hbm), scratch[6:8],
                   scratch[8], sya_ref)
    half = n_groups // 2
    assert ATTN_SAMPLE_ROWS == 4, "one hosted row in front of each of the four pattern blocks"
    sample_row(0, *sample_refs)
    strided_pattern(DILATIONS[1], 0, n_groups, *stats[0:3])
    sample_row(1, *sample_refs)
    strided_pattern(DILATIONS[2], 0, half, *stats[3:6])
    sample_row(2, *sample_refs)
    strided_pattern(DILATIONS[2], half, half, *stats[3:6])
    sample_row(3, *sample_refs)

    def body(g, carry):
        rows = pl.ds(pl.multiple_of(g * BAND, BAND), BAND)
        acc, den, m, carry = group(rows, g, carry)
        parts = [(acc, den, m)] + [(stats[3 * t][rows, :], stats[3 * t + 1][rows, :],
                                    stats[3 * t + 2][rows, :]) for t in range(2)]
        m_max = jnp.maximum(jnp.maximum(parts[0][2], parts[1][2]), parts[2][2])
        e = [jnp.exp2(pt[2] - m_max) for pt in parts]
        num = sum(ei * pt[0] for ei, pt in zip(e, parts))
        tot = sum(ei * pt[1] for ei, pt in zip(e, parts))
        o_ref[rows, :] = (num / tot * sga_ref[rows, :].astype(F32)).astype(BF16)
        return carry

    lax.fori_loop(0, n_groups, body, zero_block, unroll=True)


def _prompt_attn(q, k, v, sga, sample):
    b, s, _ = q.shape
    grid = (b, D_ATT // HEAD_PAIR, 1)
    s_ops, s_in, s_out, s_shape, s_scratch = _sample_attn_operands(sample, grid, ATTN_SAMPLE_ROWS)
    spec = pl.BlockSpec((None, s, HEAD_PAIR), lambda i, j, c: (i, 0, j))
    return pl.pallas_call(
        functools.partial(_prompt_attn_kernel, sample[-1],
                          grid[0] * grid[1] * grid[2] * ATTN_SAMPLE_ROWS),
        grid=grid,
        in_specs=[spec, spec, spec, spec] + s_in,
        out_specs=[spec, s_out],
        out_shape=[jax.ShapeDtypeStruct((b, s, D_ATT), BF16), s_shape],
        scratch_shapes=[pltpu.VMEM((2, BAND, 2 * BAND), F32)]
                       + [pltpu.VMEM((s, HEAD_PAIR), F32) for _ in range(6)] + s_scratch,
        compiler_params=pltpu.CompilerParams(
            dimension_semantics=("arbitrary", "arbitrary", "arbitrary"),
            vmem_limit_bytes=VMEM_LIMIT),
        name="prompt_attn",
    )(q, k, v, sga, *s_ops)


def _finish_kernel(ya_ref, hl_ref, x_ref, wo_ref, ln_ref, o_ref):
    sub = (_dot(ya_ref[...], wo_ref[0:D_ATT, :])
           + _dot(hl_ref[...], wo_ref[D_ATT:D_ATT + D_LRU, :]))
    z = ALPHA * x_ref[...] + sub
    mu = jnp.mean(z, axis=-1, keepdims=True)
    zc = z - mu
    var = jnp.mean(zc * zc, axis=-1, keepdims=True)
    o_ref[...] = zc * lax.rsqrt(var + LN_EPS) * ln_ref[0:1, :] + ln_ref[1:2, :]


def _finish(ya, hl, x, w_o, ln, name):
    n = x.shape[0]
    rows = min(n, FINISH_ROWS)
    row_spec = lambda width: pl.BlockSpec((rows, width), lambda i: (i, 0))
    const = lambda shape: pl.BlockSpec(shape, lambda i: (0, 0),
                                       pipeline_mode=pl.Buffered(1))
    return pl.pallas_call(
        _finish_kernel,
        grid=(n // rows,),
        in_specs=[row_spec(D_ATT), row_spec(D_LRU), row_spec(D_MODEL),
                  const((D_ATT + D_LRU, D_MODEL)), const((2, D_MODEL))],
        out_specs=row_spec(D_MODEL),
        out_shape=jax.ShapeDtypeStruct((n, D_MODEL), F32),
        compiler_params=pltpu.CompilerParams(
            dimension_semantics=("arbitrary",), vmem_limit_bytes=VMEM_LIMIT),
        name=name,
    )(ya, hl, x, w_o, ln)


def _sample_proj_kernel(x_ref, sc_ref, sh_ref, w_ref, wg_ref, vec_ref,
                        q_ref, k_ref, v_ref, hl_ref, sga_ref, conv_ref, hlast_ref):
    t_steps, nb, _ = x_ref.shape
    xb = x_ref[...].reshape(t_steps * nb, D_MODEL).astype(BF16)
    split = lambda y: y.reshape(t_steps, nb, y.shape[-1])
    q_ref[...] = split(_dot(xb, w_ref[:, C_Q:C_K]))
    k_ref[...] = split(_dot(xb, w_ref[:, C_K:C_V]))
    v_ref[...] = split(_dot(xb, w_ref[:, C_V:C_GA]))
    sga_ref[...] = split(_silu(_dot(xb, w_ref[:, C_GA:C_XL])))

    xl = _dot(xb, w_ref[:, C_XL:C_GL])
    xc = [sc_ref[i] for i in range(CONV_WIDTH - 1)]
    xc += [xl[t * nb:(t + 1) * nb, :] for t in range(t_steps)]
    u = jnp.concatenate(
        [vec_ref[4:5, :] + sum(xc[t + w] * vec_ref[w:w + 1, :] for w in range(CONV_WIDTH))
         for t in range(t_steps)], axis=0)
    for i in range(CONV_WIDTH - 1):
        conv_ref[i] = xc[t_steps + i]

    a, bx = _lru_gates(u, wg_ref, vec_ref)
    sgl = _silu(_dot(xb, w_ref[:, C_GL:D_IN]))
    h = sh_ref[...]
    for t in range(t_steps):
        rows = slice(t * nb, (t + 1) * nb)
        h = a[rows, :] * h + bx[rows, :]
        hl_ref[t] = (h * sgl[rows, :]).astype(BF16)
    hlast_ref[...] = h


def _sample_proj(xs_t, sc_t, sh, w_bf, w_gate, vecs):
    t_steps, nb, _ = xs_t.shape
    full = lambda shape: pl.BlockSpec(shape, lambda i: (0,) * len(shape))
    tok = lambda dt: jax.ShapeDtypeStruct((t_steps, nb, D_ATT), dt)
    return pl.pallas_call(
        _sample_proj_kernel,
        grid=(1,),
        in_specs=[full(xs_t.shape), full(sc_t.shape), full(sh.shape),
                  full(w_bf.shape), full(w_gate.shape), full(vecs.shape)],
        out_specs=[full((t_steps, nb, D_ATT))] * 5
                  + [full(sc_t.shape), full(sh.shape)],
        out_shape=[tok(F32), tok(F32), tok(F32), tok(BF16), tok(F32),
                   jax.ShapeDtypeStruct(sc_t.shape, F32),
                   jax.ShapeDtypeStruct(sh.shape, F32)],
        compiler_params=pltpu.CompilerParams(
            dimension_semantics=("arbitrary",), vmem_limit_bytes=VMEM_LIMIT),
        name="sample_proj_lru",
    )(xs_t, sc_t, sh, w_bf, w_gate, vecs)


def kernel(x_prompt, x_sample, cache_k, cache_v, state_conv, state_h, w_in, conv_w, conv_b, w_ra, b_ra, w_ri, b_ri, lru_lambda, w_out, ln_g, ln_b):
    assert w_in.shape[0] == 1, "single layer"
    b, s, _ = x_prompt.shape
    nb, t_steps, _ = x_sample.shape
    w_buf = cache_k.shape[2]

    col_scale = jnp.concatenate([jnp.full((D_ATT,), HEAD_DIM ** -0.5, F32),
                                 jnp.ones((D_IN - D_ATT,), F32)])
    w_bf = (w_in[0] * col_scale).astype(BF16)
    eye = jnp.eye(LRU_BLOCKS, dtype=F32)
    block_diag = lambda wb: (eye[:, None, :, None] * wb[:, :, None, :]).reshape(D_LRU, D_LRU)
    w_gate = (0.5 * jnp.concatenate([block_diag(w_ra[0]), block_diag(w_ri[0])], axis=1)).astype(BF16)
    vecs = jnp.concatenate([conv_w[0], conv_b, b_ra, b_ri, lru_lambda], axis=0)
    w_o = w_out[0].astype(BF16)
    ln = jnp.concatenate([ln_g, ln_b], axis=0)
    heads = lambda a: a.reshape(a.shape[:-1] + (N_HEADS, HEAD_DIM))

    xs_t = jnp.transpose(x_sample, (1, 0, 2))
    sc_t = jnp.transpose(state_conv[0], (1, 0, 2))
    q_s, k_s, v_s, hl_s, sga_s, conv_s, h_s = _sample_proj(
        xs_t, sc_t, state_h[0], w_bf, w_gate, vecs)
    kt = jnp.transpose(cache_k[0], (0, 2, 3, 1)).reshape(nb, D_ATT, w_buf)
    vt = jnp.transpose(cache_v[0], (0, 2, 3, 1)).reshape(nb, D_ATT, w_buf)
    sample = (q_s, k_s, v_s, sga_s, kt, vt)

    n_proj = b * (s // PROJ_ROWS) * PROJ_PHASES
    n_attn = b * (D_ATT // HEAD_PAIR) * ATTN_SAMPLE_ROWS
    assert n_proj + n_attn == nb
    n_keep = min(w_buf, s)
    q, k, v, hl, sga, conv_p, h_p, kt_p, vt_p, ya_s0 = _proj_lru(
        x_prompt, w_bf, w_gate, vecs, sample + (0,), n_keep)
    ya, ya_s1 = _prompt_attn(q, k, v, sga, sample + (n_proj,))
    y_p = _finish(ya.reshape(b * s, D_ATT), hl.reshape(b * s, D_LRU),
                  x_prompt.reshape(b * s, D_MODEL), w_o, ln, "prompt_finish")
    window = lambda a: jnp.transpose(a.reshape(b, N_HEADS, HEAD_DIM, n_keep), (0, 3, 1, 2))[None]
    k_p, v_p = window(kt_p), window(vt_p)
    ya_s = jnp.concatenate([ya_s0, ya_s1], axis=1)
    y_s = _finish(ya_s.reshape(t_steps * nb, D_ATT).astype(BF16),
                  hl_s.reshape(t_steps * nb, D_LRU),
                  xs_t.reshape(t_steps * nb, D_MODEL), w_o, ln, "sample_finish")
    to_batch_major = lambda a: jnp.transpose(a, (1, 0, 2))

    return (y_p.reshape(b, s, D_MODEL),
            to_batch_major(y_s.reshape(t_steps, nb, D_MODEL)),
            k_p, v_p,
            conv_p[None], h_p.reshape(1, b, D_LRU),
            heads(to_batch_major(k_s))[None], heads(to_batch_major(v_s))[None],
            to_batch_major(conv_s)[None], h_s[None])
```

```python
import functools

import numpy as np
import jax
import jax.numpy as jnp
from jax import lax
from jax.experimental import pallas as pl
from jax.experimental.pallas import tpu as pltpu

F32 = jnp.float32
BF16 = jnp.bfloat16

D_MODEL = 1024
N_HEADS = 8
HEAD_DIM = 64
D_ATT = N_HEADS * HEAD_DIM
D_LRU = 512
LRU_BLOCKS = 8
D_IN = 4 * D_ATT + 2 * D_LRU
CONV_WIDTH = 4
LRU_C = 8.0
DILATIONS = (1, 4, 16)
N_DIL_KEYS = 128
BAND = 128
ALPHA = 2.0 ** 0.25
LN_EPS = 1e-5
NEG_INF = -1e30
LOG2_E = 1.4426950408889634

C_Q, C_K, C_V, C_GA, C_XL, C_GL = (0, D_ATT, 2 * D_ATT, 3 * D_ATT, 4 * D_ATT,
                                   4 * D_ATT + D_LRU)

VMEM_LIMIT = 60 * 1024 * 1024
PROJ_ROWS = 512
FINISH_ROWS = 1024
HEAD_PAIR = 2 * HEAD_DIM
SUBLANES = 8
CACHE_SLOTS = 3
PROJ_PHASES = 2
ATTN_SAMPLE_ROWS = 2
ATTN_PHASES = 2


def _dot(a, b):
    return jnp.dot(a, b, preferred_element_type=F32)


def _dot_nt(a, b):
    return lax.dot_general(a, b, (((1,), (1,)), ((), ())),
                           preferred_element_type=F32)


def _sigmoid(x):
    return 0.5 * jnp.tanh(0.5 * x) + 0.5


def _silu(x):
    return x * _sigmoid(x)


def _softplus(x):
    return jnp.maximum(x, 0.0) + jnp.log1p(jnp.exp(-jnp.abs(x)))


def _lru_gates(u, wg_ref, vec_ref):
    g = _dot(u.astype(BF16), wg_ref[...])
    tr = jnp.tanh(g[:, :D_LRU] + 0.5 * vec_ref[5:6, :])
    ig = 0.5 * jnp.tanh(g[:, D_LRU:] + 0.5 * vec_ref[6:7, :]) + 0.5
    half_c = (-0.5 * LRU_C) * _softplus(-vec_ref[7:8, :])
    log_a = half_c * tr + half_c
    a = jnp.exp(log_a)
    x = jnp.tanh(-log_a) * (1.0 + a * a)
    bx = jnp.where(x > 0.0, x * lax.rsqrt(x), 0.0) * ig * u
    return a, bx


def _grid_step():
    return ((pl.program_id(0) * pl.num_programs(1) + pl.program_id(1)) * pl.num_programs(2)
            + pl.program_id(2))


def _cache_copies(row, cache_hbm, ring, sems):
    slot = row % CACHE_SLOTS
    return [pltpu.make_async_copy(src.at[row], dst.at[slot], sems.at[which, slot])
            for which, (src, dst) in enumerate(zip(cache_hbm, ring))]


def _sample_attn_step(first, n_rows_hosted, per_step, which, q_ref, kn_ref, vn_ref, sga_ref,
                      cnt_ref, cache_hbm, ring, sems, o_ref):
    t_steps = q_ref.shape[0]
    local = _grid_step() * per_step + which
    lookahead = CACHE_SLOTS - 1
    if which == 0:
        @pl.when(local == 0)
        def _():
            for early in range(lookahead):
                for copy in _cache_copies(first + early, cache_hbm, ring, sems):
                    copy.start()
    for copy in _cache_copies(first + local, cache_hbm, ring, sems):
        copy.wait()

    @pl.when(local + lookahead < n_rows_hosted)
    def _():
        for copy in _cache_copies(first + local + lookahead, cache_hbm, ring, sems):
            copy.start()

    kt_ref, vt_ref = (buf.at[(first + local) % CACHE_SLOTS] for buf in ring)
    mine = pl.ds(local % SUBLANES, 1)
    n_rows = t_steps * N_HEADS
    sub = lax.broadcasted_iota(jnp.int32, (N_HEADS, D_ATT), 0)
    lane_head = lax.broadcasted_iota(jnp.int32, (N_HEADS, D_ATT), 1) // HEAD_DIM
    own = sub == lane_head
    qbd = jnp.concatenate(
        [jnp.where(own, jnp.broadcast_to(q_ref[t, mine, :], (N_HEADS, D_ATT)), 0.0)
         for t in range(t_steps)], axis=0)

    cnt = cnt_ref[...]
    s = jnp.where(cnt > 0.0, _dot(qbd.astype(BF16), kt_ref[...].astype(BF16)), NEG_INF)
    m = jnp.max(s, axis=-1, keepdims=True)

    row_t = lax.broadcasted_iota(jnp.int32, (n_rows, 1), 0) // N_HEADS
    s_new, c_new = [], []
    for t2 in range(t_steps):
        c = jnp.where(row_t == t2, float(len(DILATIONS)),
                      jnp.where(row_t > t2, 1.0, 0.0))
        sn = jnp.sum(qbd * kn_ref[t2, mine, :], axis=-1, keepdims=True)
        sn = jnp.where(c > 0.0, sn, NEG_INF)
        m = jnp.maximum(m, sn)
        s_new.append(sn)
        c_new.append(c)

    p = jnp.exp(s - m) * cnt
    den = jnp.sum(p, axis=-1, keepdims=True)
    acc = _dot_nt(p.astype(BF16), vt_ref[...].astype(BF16))
    for t2 in range(t_steps):
        pn = jnp.exp(s_new[t2] - m) * c_new[t2]
        den = den + pn
        acc = acc + pn * vn_ref[t2, mine, :]
    y = acc / den
    for t in range(t_steps):
        yt = jnp.sum(jnp.where(own, y[t * N_HEADS:(t + 1) * N_HEADS, :], 0.0),
                     axis=0, keepdims=True)
        o_ref[t, mine, :] = yt * sga_ref[t, mine, :]


def _pattern_counts(t_steps, w_buf):
    delta = w_buf + np.arange(t_steps)[:, None] - np.arange(w_buf)[None, :]
    cnt = np.zeros((t_steps, w_buf), np.float32)
    for d in DILATIONS:
        cnt += (delta % d == 0) & (delta // d <= N_DIL_KEYS)
    return np.repeat(cnt, N_HEADS, axis=0)


def _sample_attn_operands(sample, grid, per_step):
    q_s, kn_s, vn_s, sga_s, kt, vt, first = sample
    t_steps = q_s.shape[0]
    w_buf = kt.shape[-1]
    n_local = grid[0] * grid[1] * grid[2] * per_step
    assert first % SUBLANES == 0 and SUBLANES % per_step == 0 and n_local >= CACHE_SLOTS
    cnt = jnp.asarray(_pattern_counts(t_steps, w_buf))
    step = lambda i, j, c: ((i * grid[1] + j) * grid[2] + c) * per_step
    tok_block = (t_steps, SUBLANES, D_ATT)
    tok_in = pl.BlockSpec(tok_block, lambda *g: (0, (first + step(*g)) // SUBLANES, 0))
    in_hbm = pl.BlockSpec(memory_space=pl.ANY)
    cnt_spec = pl.BlockSpec(cnt.shape, lambda *g: (0, 0), pipeline_mode=pl.Buffered(1))
    operands = (q_s, kn_s, vn_s, sga_s, cnt, kt, vt)
    in_specs = [tok_in, tok_in, tok_in, tok_in, cnt_spec, in_hbm, in_hbm]
    out_spec = pl.BlockSpec(tok_block, lambda *g: (0, step(*g) // SUBLANES, 0))
    out_shape = jax.ShapeDtypeStruct((t_steps, n_local, D_ATT), F32)
    scratch = [pltpu.VMEM((CACHE_SLOTS, D_ATT, w_buf), F32),
               pltpu.VMEM((CACHE_SLOTS, D_ATT, w_buf), F32),
               pltpu.SemaphoreType.DMA((2, CACHE_SLOTS))]
    return operands, in_specs, out_spec, out_shape, scratch


def _proj_lru_kernel(first_kept, first_sample, n_hosted, x_ref, w_ref, wg_ref, vec_ref,
                     sq_ref, skn_ref, svn_ref, ssga_ref, cnt_ref, kt_hbm, vt_hbm,
                     q_ref, k_ref, v_ref, hl_ref, sga_ref, conv_ref, hlast_ref,
                     kt_keep_ref, vt_keep_ref, sya_ref,
                     xb_ref, xc_ref, a_ref, b_ref, carry_ref, kt_ring, vt_ring, cache_sems):
    j = pl.program_id(1)
    phase = pl.program_id(2)
    rows = x_ref.shape[0]
    kept = j >= first_kept
    sample_step = functools.partial(_sample_attn_step, first_sample, n_hosted, 1, 0, sq_ref, skn_ref,
                                    svn_ref, ssga_ref, cnt_ref, (kt_hbm, vt_hbm),
                                    (kt_ring, vt_ring), cache_sems, sya_ref)

    @pl.when(phase == 0)
    def _():
        @pl.when(j == 0)
        def _():
            xc_ref[...] = jnp.zeros((8, D_LRU), F32)
            carry_ref[...] = jnp.zeros((1, D_LRU), F32)

        sample_step()
        xb_ref[...] = x_ref[...].astype(BF16)
        xl = _dot(xb_ref[...], w_ref[:, C_XL:C_GL])
        tail = xc_ref[...]
        xc_ref[...] = xl[rows - 8:, :]
        row8 = lax.broadcasted_iota(jnp.int32, (8, D_LRU), 0)
        u = vec_ref[4:5, :] + xl * vec_ref[3:4, :]
        for s in range(1, CONV_WIDTH):
            rolled = pltpu.roll(xl, s, axis=0)
            head = jnp.where(row8 < s, pltpu.roll(tail, s, axis=0), rolled[0:8, :])
            u = u + jnp.concatenate([head, rolled[8:, :]], axis=0) * vec_ref[3 - s:4 - s, :]
        a, bx = _lru_gates(u, wg_ref, vec_ref)
        a_ref[...] = a
        b_ref[...] = bx
        q_ref[...] = _dot(xb_ref[...], w_ref[:, C_Q:C_K]) * LOG2_E
        k_new = _dot(xb_ref[...], w_ref[:, C_K:C_V])
        k_ref[...] = k_new
        v_new = _dot(xb_ref[...], w_ref[:, C_V:C_GA])
        v_ref[...] = v_new

        @pl.when(kept)
        def _():
            kt_keep_ref[...] = k_new.T
            vt_keep_ref[...] = v_new.T

    @pl.when(phase == 1)
    def _():
        sample_step()
        sga_ref[...] = _silu(_dot(xb_ref[...], w_ref[:, C_GA:C_XL])).astype(BF16)
        sgl = _silu(_dot(xb_ref[...], w_ref[:, C_GL:D_IN]))

        row = lax.broadcasted_iota(jnp.int32, (8, D_LRU), 0)

        def group(g, carry):
            r0 = pl.multiple_of(g * 8, 8)
            av = a_ref[pl.ds(r0, 8), :]
            bv = b_ref[pl.ds(r0, 8), :]
            for s in (1, 2, 4):
                keep = row >= s
                bv = jnp.where(keep, av * pltpu.roll(bv, s, axis=0) + bv, bv)
                av = jnp.where(keep, av * pltpu.roll(av, s, axis=0), av)
            h8 = av * carry + bv
            b_ref[pl.ds(r0, 8), :] = h8
            return h8[7:8, :]

        carry = lax.fori_loop(0, rows // 8, group, carry_ref[...], unroll=True)
        carry_ref[...] = carry
        hl_ref[...] = (b_ref[...] * sgl).astype(BF16)

        @pl.when(j == pl.num_programs(1) - 1)
        def _():
            conv_ref[...] = xc_ref[5:8, :]
            hlast_ref[...] = carry


def _proj_lru(x, w_bf, w_gate, vecs, sample, n_keep):
    b, s, _ = x.shape
    nt = s // PROJ_ROWS
    assert n_keep % PROJ_ROWS == 0
    first_kept = nt - n_keep // PROJ_ROWS
    keep_spec = pl.BlockSpec((None, D_ATT, PROJ_ROWS),
                             lambda i, j, c: (i, 0, jnp.maximum(j - first_kept, 0)))
    keep_shape = jax.ShapeDtypeStruct((b, D_ATT, n_keep), F32)
    grid = (b, nt, PROJ_PHASES)
    s_ops, s_in, s_out, s_shape, s_scratch = _sample_attn_operands(sample, grid, 1)
    row_spec = lambda width: pl.BlockSpec((None, PROJ_ROWS, width), lambda i, j, c: (i, j, 0))
    const = lambda shape: pl.BlockSpec(shape, lambda i, j, c: (0,) * len(shape),
                                       pipeline_mode=pl.Buffered(1))
    per_batch = lambda n: pl.BlockSpec((None, n, D_LRU), lambda i, j, c: (i, 0, 0))
    tile = lambda dt: jax.ShapeDtypeStruct((b, s, D_ATT), dt)
    return pl.pallas_call(
        functools.partial(_proj_lru_kernel, first_kept, sample[-1], b * nt * PROJ_PHASES),
        grid=grid,
        in_specs=[row_spec(D_MODEL), const((D_MODEL, D_IN)),
                  const((D_LRU, 2 * D_LRU)), const((8, D_LRU))] + s_in,
        out_specs=[row_spec(D_ATT), row_spec(D_ATT), row_spec(D_ATT),
                   row_spec(D_LRU), row_spec(D_ATT),
                   per_batch(CONV_WIDTH - 1), per_batch(1), keep_spec, keep_spec, s_out],
        out_shape=[tile(F32), tile(F32), tile(F32), tile(BF16), tile(BF16),
                   jax.ShapeDtypeStruct((b, CONV_WIDTH - 1, D_LRU), F32),
                   jax.ShapeDtypeStruct((b, 1, D_LRU), F32), keep_shape, keep_shape, s_shape],
        scratch_shapes=[pltpu.VMEM((PROJ_ROWS, D_MODEL), BF16),
                        pltpu.VMEM((8, D_LRU), F32),
                        pltpu.VMEM((PROJ_ROWS, D_LRU), F32),
                        pltpu.VMEM((PROJ_ROWS, D_LRU), F32),
                        pltpu.VMEM((1, D_LRU), F32)] + s_scratch,
        compiler_params=pltpu.CompilerParams(
            dimension_semantics=("arbitrary", "arbitrary", "arbitrary"),
            vmem_limit_bytes=VMEM_LIMIT),
        name="prompt_proj_lru",
    )(x, w_bf, w_gate, vecs, *s_ops)


def _prompt_attn_kernel(first_sample, n_hosted, q_ref, k_ref, v_ref, sga_ref,
                        sq_ref, skn_ref, svn_ref, ssga_ref, cnt_ref, kt_hbm, vt_hbm,
                        o_ref, sya_ref, bias_ref, *scratch):
    seq = q_ref.shape[0]
    phase = pl.program_id(2)
    n_groups = seq // BAND
    head_a = lax.broadcasted_iota(jnp.int32, (BAND, HEAD_PAIR), 1) < HEAD_DIM
    one = jnp.ones((), BF16)

    stats = scratch[:6]

    def sample_steps():
        for which in range(ATTN_SAMPLE_ROWS):
            _sample_attn_step(first_sample, n_hosted, ATTN_SAMPLE_ROWS, which, sq_ref, skn_ref,
                              svn_ref, ssga_ref, cnt_ref, (kt_hbm, vt_hbm), scratch[6:8],
                              scratch[8], sya_ref)

    def key_value_block(rows):
        v_bf = v_ref[rows, :].astype(BF16)
        v_ext = jnp.concatenate([jnp.where(head_a, v_bf, one),
                                 jnp.where(head_a, one, v_bf)], axis=1)
        return k_ref[rows, :].astype(BF16), v_ext

    def group(rows, blk, carry):
        k_prev, v_prev = carry
        k_cur, v_cur = key_value_block(rows)
        k2 = jnp.concatenate([k_prev, k_cur], axis=0)
        v4 = jnp.concatenate([v_prev, v_cur], axis=0)
        qf = q_ref[rows, :]
        q2 = jnp.concatenate([jnp.where(head_a, qf, 0.0),
                              jnp.where(head_a, 0.0, qf)], axis=0).astype(BF16)
        s = _dot_nt(q2, k2).reshape(2, BAND, 2 * BAND)
        s = s + bias_ref[jnp.minimum(blk, 1)][None]
        m = jnp.max(s, axis=-1, keepdims=True)
        p = jnp.exp2(s - m).reshape(2 * BAND, 2 * BAND).astype(BF16)
        o = _dot(p, v4)
        acc = jnp.where(head_a, o[:BAND, :HEAD_PAIR], o[BAND:, HEAD_PAIR:])
        den = jnp.where(head_a, o[:BAND, HEAD_PAIR:], o[BAND:, :HEAD_PAIR])
        return acc, den, jnp.where(head_a, m[0], m[1]), (k_cur, v_cur)

    zero_block = (jnp.zeros((BAND, HEAD_PAIR), BF16), jnp.zeros((BAND, 2 * HEAD_PAIR), BF16))

    def strided_pattern(d, first, count, acc_ref, l_ref, m_ref):
        span = BAND * d
        n_blk = seq // span

        def body(i, carry):
            g = first + i
            blk = g % n_blk
            rows = pl.ds(blk * span + g // n_blk, BAND, stride=d)
            acc, den, m, carry = group(rows, blk, carry)
            acc_ref[rows, :] = acc
            l_ref[rows, :] = den
            m_ref[rows, :] = m
            return carry

        lax.fori_loop(0, count, body, zero_block, unroll=True)

    half = n_groups // 2

    @pl.when(phase == 0)
    def _():
        row = lax.broadcasted_iota(jnp.int32, (BAND, 2 * BAND), 0)
        col = lax.broadcasted_iota(jnp.int32, (BAND, 2 * BAND), 1)
        band = (col >= row) & (col <= row + N_DIL_KEYS)
        bias_ref[0] = jnp.where(band & (col >= BAND), 0.0, NEG_INF)
        bias_ref[1] = jnp.where(band, 0.0, NEG_INF)
        sample_steps()
        strided_pattern(DILATIONS[1], 0, n_groups, *stats[0:3])
        strided_pattern(DILATIONS[2], 0, half, *stats[3:6])

    @pl.when(phase == 1)
    def _():
        sample_steps()
        strided_pattern(DILATIONS[2], half, half, *stats[3:6])

        def body(g, carry):
            rows = pl.ds(pl.multiple_of(g * BAND, BAND), BAND)
            acc, den, m, carry = group(rows, g, carry)
            parts = [(acc, den, m)] + [(stats[3 * t][rows, :], stats[3 * t + 1][rows, :],
                                        stats[3 * t + 2][rows, :]) for t in range(2)]
            m_max = jnp.maximum(jnp.maximum(parts[0][2], parts[1][2]), parts[2][2])
            e = [jnp.exp2(pt[2] - m_max) for pt in parts]
            num = sum(ei * pt[0] for ei, pt in zip(e, parts))
            tot = sum(ei * pt[1] for ei, pt in zip(e, parts))
            o_ref[rows, :] = (num / tot * sga_ref[rows, :].astype(F32)).astype(BF16)
            return carry

        lax.fori_loop(0, n_groups, body, zero_block, unroll=True)


def _prompt_attn(q, k, v, sga, sample):
    b, s, _ = q.shape
    grid = (b, D_ATT // HEAD_PAIR, ATTN_PHASES)
    s_ops, s_in, s_out, s_shape, s_scratch = _sample_attn_operands(sample, grid, ATTN_SAMPLE_ROWS)
    spec = pl.BlockSpec((None, s, HEAD_PAIR), lambda i, j, c: (i, 0, j))
    return pl.pallas_call(
        functools.partial(_prompt_attn_kernel, sample[-1],
                          grid[0] * grid[1] * grid[2] * ATTN_SAMPLE_ROWS),
        grid=grid,
        in_specs=[spec, spec, spec, spec] + s_in,
        out_specs=[spec, s_out],
        out_shape=[jax.ShapeDtypeStruct((b, s, D_ATT), BF16), s_shape],
        scratch_shapes=[pltpu.VMEM((2, BAND, 2 * BAND), F32)]
                       + [pltpu.VMEM((s, HEAD_PAIR), F32) for _ in range(6)] + s_scratch,
        compiler_params=pltpu.CompilerParams(
            dimension_semantics=("arbitrary", "arbitrary", "arbitrary"),
            vmem_limit_bytes=VMEM_LIMIT),
        name="prompt_attn",
    )(q, k, v, sga, *s_ops)


def _finish_kernel(ya_ref, hl_ref, x_ref, wo_ref, ln_ref, o_ref):
    sub = (_dot(ya_ref[...], wo_ref[0:D_ATT, :])
           + _dot(hl_ref[...], wo_ref[D_ATT:D_ATT + D_LRU, :]))
    z = ALPHA * x_ref[...] + sub
    mu = jnp.mean(z, axis=-1, keepdims=True)
    zc = z - mu
    var = jnp.mean(zc * zc, axis=-1, keepdims=True)
    o_ref[...] = zc * lax.rsqrt(var + LN_EPS) * ln_ref[0:1, :] + ln_ref[1:2, :]


def _finish(ya, hl, x, w_o, ln, name):
    n = x.shape[0]
    rows = min(n, FINISH_ROWS)
    row_spec = lambda width: pl.BlockSpec((rows, width), lambda i: (i, 0))
    const = lambda shape: pl.BlockSpec(shape, lambda i: (0, 0),
                                       pipeline_mode=pl.Buffered(1))
    return pl.pallas_call(
        _finish_kernel,
        grid=(n // rows,),
        in_specs=[row_spec(D_ATT), row_spec(D_LRU), row_spec(D_MODEL),
                  const((D_ATT + D_LRU, D_MODEL)), const((2, D_MODEL))],
        out_specs=row_spec(D_MODEL),
        out_shape=jax.ShapeDtypeStruct((n, D_MODEL), F32),
        compiler_params=pltpu.CompilerParams(
            dimension_semantics=("arbitrary",), vmem_limit_bytes=VMEM_LIMIT),
        name=name,
    )(ya, hl, x, w_o, ln)


def _sample_proj_kernel(x_ref, sc_ref, sh_ref, w_ref, wg_ref, vec_ref,
                        q_ref, k_ref, v_ref, hl_ref, sga_ref, conv_ref, hlast_ref, wbf_ref):
    t_steps, nb, _ = x_ref.shape
    wbf_ref[:, C_Q:C_K] = (w_ref[:, C_Q:C_K] * HEAD_DIM ** -0.5).astype(BF16)
    wbf_ref[:, C_K:D_IN] = w_ref[:, C_K:D_IN].astype(BF16)
    xb = x_ref[...].reshape(t_steps * nb, D_MODEL).astype(BF16)
    split = lambda y: y.reshape(t_steps, nb, y.shape[-1])
    q_ref[...] = split(_dot(xb, wbf_ref[:, C_Q:C_K]))
    k_ref[...] = split(_dot(xb, wbf_ref[:, C_K:C_V]))
    v_ref[...] = split(_dot(xb, wbf_ref[:, C_V:C_GA]))
    sga_ref[...] = split(_silu(_dot(xb, wbf_ref[:, C_GA:C_XL])))

    xl = _dot(xb, wbf_ref[:, C_XL:C_GL])
    xc = [sc_ref[i] for i in range(CONV_WIDTH - 1)]
    xc += [xl[t * nb:(t + 1) * nb, :] for t in range(t_steps)]
    u = jnp.concatenate(
        [vec_ref[4:5, :] + sum(xc[t + w] * vec_ref[w:w + 1, :] for w in range(CONV_WIDTH))
         for t in range(t_steps)], axis=0)
    for i in range(CONV_WIDTH - 1):
        conv_ref[i] = xc[t_steps + i]

    a, bx = _lru_gates(u, wg_ref, vec_ref)
    sgl = _silu(_dot(xb, wbf_ref[:, C_GL:D_IN]))
    h = sh_ref[...]
    for t in range(t_steps):
        rows = slice(t * nb, (t + 1) * nb)
        h = a[rows, :] * h + bx[rows, :]
        hl_ref[t] = (h * sgl[rows, :]).astype(BF16)
    hlast_ref[...] = h


def _sample_proj(xs_t, sc_t, sh, w_f32, w_gate, vecs):
    t_steps, nb, _ = xs_t.shape
    full = lambda shape: pl.BlockSpec(shape, lambda i: (0,) * len(shape),
                                      pipeline_mode=pl.Buffered(1))
    tok = lambda dt: jax.ShapeDtypeStruct((t_steps, nb, D_ATT), dt)
    return pl.pallas_call(
        _sample_proj_kernel,
        grid=(1,),
        in_specs=[full(xs_t.shape), full(sc_t.shape), full(sh.shape),
                  full(w_f32.shape), full(w_gate.shape), full(vecs.shape)],
        out_specs=[full((t_steps, nb, D_ATT))] * 5
                  + [full(sc_t.shape), full(sh.shape), full(w_f32.shape)],
        out_shape=[tok(F32), tok(F32), tok(F32), tok(BF16), tok(F32),
                   jax.ShapeDtypeStruct(sc_t.shape, F32),
                   jax.ShapeDtypeStruct(sh.shape, F32),
                   jax.ShapeDtypeStruct(w_f32.shape, BF16)],
        compiler_params=pltpu.CompilerParams(
            dimension_semantics=("arbitrary",), vmem_limit_bytes=VMEM_LIMIT),
        name="sample_proj_lru",
    )(xs_t, sc_t, sh, w_f32, w_gate, vecs)


def kernel(x_prompt, x_sample, cache_k, cache_v, state_conv, state_h, w_in, conv_w, conv_b, w_ra, b_ra, w_ri, b_ri, lru_lambda, w_out, ln_g, ln_b):
    assert w_in.shape[0] == 1, "single layer"
    b, s, _ = x_prompt.shape
    nb, t_steps, _ = x_sample.shape
    w_buf = cache_k.shape[2]

    eye = jnp.eye(LRU_BLOCKS, dtype=F32)
    block_diag = lambda wb: (eye[:, None, :, None] * wb[:, :, None, :]).reshape(D_LRU, D_LRU)
    w_gate = (0.5 * jnp.concatenate([block_diag(w_ra[0]), block_diag(w_ri[0])], axis=1)).astype(BF16)
    vecs = jnp.concatenate([conv_w[0], conv_b, b_ra, b_ri, lru_lambda], axis=0)
    w_o = w_out[0].astype(BF16)
    ln = jnp.concatenate([ln_g, ln_b], axis=0)
    heads = lambda a: a.reshape(a.shape[:-1] + (N_HEADS, HEAD_DIM))

    xs_t = jnp.transpose(x_sample, (1, 0, 2))
    sc_t = jnp.transpose(state_conv[0], (1, 0, 2))
    q_s, k_s, v_s, hl_s, sga_s, conv_s, h_s, w_bf = _sample_proj(
        xs_t, sc_t, state_h[0], w_in[0], w_gate, vecs)
    kt = jnp.transpose(cache_k[0], (0, 2, 3, 1)).reshape(nb, D_ATT, w_buf)
    vt = jnp.transpose(cache_v[0], (0, 2, 3, 1)).reshape(nb, D_ATT, w_buf)
    sample = (q_s, k_s, v_s, sga_s, kt, vt)

    n_proj = b * (s // PROJ_ROWS) * PROJ_PHASES
    n_attn = b * (D_ATT // HEAD_PAIR) * ATTN_PHASES * ATTN_SAMPLE_ROWS
    assert n_proj + n_attn == nb
    n_keep = min(w_buf, s)
    q, k, v, hl, sga, conv_p, h_p, kt_p, vt_p, ya_s0 = _proj_lru(
        x_prompt, w_bf, w_gate, vecs, sample + (0,), n_keep)
    ya, ya_s1 = _prompt_attn(q, k, v, sga, sample + (n_proj,))
    y_p = _finish(ya.reshape(b * s, D_ATT), hl.reshape(b * s, D_LRU),
                  x_prompt.reshape(b * s, D_MODEL), w_o, ln, "prompt_finish")
    window = lambda a: jnp.transpose(a.reshape(b, N_HEADS, HEAD_DIM, n_keep), (0, 3, 1, 2))[None]
    k_p, v_p = window(kt_p), window(vt_p)
    ya_s = jnp.concatenate([ya_s0, ya_s1], axis=1)
    y_s = _finish(ya_s.reshape(t_steps * nb, D_ATT).astype(BF16),
                  hl_s.reshape(t_steps * nb, D_LRU),
                  xs_t.reshape(t_steps * nb, D_MODEL), w_o, ln, "sample_finish")
    to_batch_major = lambda a: jnp.transpose(a, (1, 0, 2))

    return (y_p.reshape(b, s, D_MODEL),
            to_batch_major(y_s.reshape(t_steps, nb, D_MODEL)),
            k_p, v_p,
            conv_p[None], h_p.reshape(1, b, D_LRU),
            heads(to_batch_major(k_s))[None], heads(to_batch_major(v_s))[None],
            to_batch_major(conv_s)[None], h_s[None])
```

```python
import functools

import numpy as np
import jax
import jax.numpy as jnp
from jax import lax
from jax.experimental import pallas as pl
from jax.experimental.pallas import tpu as pltpu

F32 = jnp.float32
BF16 = jnp.bfloat16

D_MODEL = 1024
N_HEADS = 8
HEAD_DIM = 64
D_ATT = N_HEADS * HEAD_DIM
D_LRU = 512
LRU_BLOCKS = 8
D_IN = 4 * D_ATT + 2 * D_LRU
CONV_WIDTH = 4
LRU_C = 8.0
DILATIONS = (1, 4, 16)
N_DIL_KEYS = 128
BAND = 128
ALPHA = 2.0 ** 0.25
LN_EPS = 1e-5
NEG_INF = -1e30
LOG2_E = 1.4426950408889634

C_Q, C_K, C_V, C_GA, C_XL, C_GL = (0, D_ATT, 2 * D_ATT, 3 * D_ATT, 4 * D_ATT,
                                   4 * D_ATT + D_LRU)

VMEM_LIMIT = 60 * 1024 * 1024
PROJ_ROWS = 512
FINISH_ROWS = 2048
HEAD_PAIR = 2 * HEAD_DIM
SUBLANES = 8
CACHE_SLOTS = 3
PROJ_PHASES = 2
ATTN_SAMPLE_ROWS = 2
ATTN_PHASES = 2


def _dot(a, b):
    return jnp.dot(a, b, preferred_element_type=F32)


def _dot_nt(a, b):
    return lax.dot_general(a, b, (((1,), (1,)), ((), ())),
                           preferred_element_type=F32)


def _sigmoid(x):
    return 0.5 * jnp.tanh(0.5 * x) + 0.5


def _silu(x):
    return x * _sigmoid(x)


def _softplus(x):
    return jnp.maximum(x, 0.0) + jnp.log1p(jnp.exp(-jnp.abs(x)))


def _lru_gates(u, wg_ref, vec_ref):
    g = _dot(u.astype(BF16), wg_ref[...])
    tr = jnp.tanh(g[:, :D_LRU] + 0.5 * vec_ref[5:6, :])
    ig = 0.5 * jnp.tanh(g[:, D_LRU:] + 0.5 * vec_ref[6:7, :]) + 0.5
    half_c = (-0.5 * LRU_C) * _softplus(-vec_ref[7:8, :])
    log_a = half_c * tr + half_c
    a = jnp.exp(log_a)
    x = jnp.tanh(-log_a) * (1.0 + a * a)
    bx = jnp.where(x > 0.0, x * lax.rsqrt(x), 0.0) * ig * u
    return a, bx


def _grid_step():
    return ((pl.program_id(0) * pl.num_programs(1) + pl.program_id(1)) * pl.num_programs(2)
            + pl.program_id(2))


def _cache_copies(row, cache_hbm, ring, sems):
    slot = row % CACHE_SLOTS
    return [pltpu.make_async_copy(src.at[row], dst.at[slot], sems.at[which, slot])
            for which, (src, dst) in enumerate(zip(cache_hbm, ring))]


def _sample_attn_step(first, n_rows_hosted, per_step, which, q_ref, kn_ref, vn_ref, sga_ref,
                      cnt_ref, cache_hbm, ring, sems, o_ref):
    t_steps = q_ref.shape[0]
    local = _grid_step() * per_step + which
    lookahead = CACHE_SLOTS - 1
    if which == 0:
        @pl.when(local == 0)
        def _():
            for early in range(lookahead):
                for copy in _cache_copies(first + early, cache_hbm, ring, sems):
                    copy.start()
    for copy in _cache_copies(first + local, cache_hbm, ring, sems):
        copy.wait()

    @pl.when(local + lookahead < n_rows_hosted)
    def _():
        for copy in _cache_copies(first + local + lookahead, cache_hbm, ring, sems):
            copy.start()

    kt_ref, vt_ref = (buf.at[(first + local) % CACHE_SLOTS] for buf in ring)
    mine = pl.ds(local % SUBLANES, 1)
    n_rows = t_steps * N_HEADS
    sub = lax.broadcasted_iota(jnp.int32, (N_HEADS, D_ATT), 0)
    lane_head = lax.broadcasted_iota(jnp.int32, (N_HEADS, D_ATT), 1) // HEAD_DIM
    own = sub == lane_head
    qbd = jnp.concatenate(
        [jnp.where(own, jnp.broadcast_to(q_ref[t, mine, :], (N_HEADS, D_ATT)), 0.0)
         for t in range(t_steps)], axis=0)

    cnt = cnt_ref[...]
    s = jnp.where(cnt > 0.0, _dot(qbd.astype(BF16), kt_ref[...].astype(BF16)), NEG_INF)
    m = jnp.max(s, axis=-1, keepdims=True)

    row_t = lax.broadcasted_iota(jnp.int32, (n_rows, 1), 0) // N_HEADS
    s_new, c_new = [], []
    for t2 in range(t_steps):
        c = jnp.where(row_t == t2, float(len(DILATIONS)),
                      jnp.where(row_t > t2, 1.0, 0.0))
        sn = jnp.sum(qbd * kn_ref[t2, mine, :], axis=-1, keepdims=True)
        sn = jnp.where(c > 0.0, sn, NEG_INF)
        m = jnp.maximum(m, sn)
        s_new.append(sn)
        c_new.append(c)

    p = jnp.exp(s - m) * cnt
    den = jnp.sum(p, axis=-1, keepdims=True)
    acc = _dot_nt(p.astype(BF16), vt_ref[...].astype(BF16))
    for t2 in range(t_steps):
        pn = jnp.exp(s_new[t2] - m) * c_new[t2]
        den = den + pn
        acc = acc + pn * vn_ref[t2, mine, :]
    y = acc / den
    for t in range(t_steps):
        yt = jnp.sum(jnp.where(own, y[t * N_HEADS:(t + 1) * N_HEADS, :], 0.0),
                     axis=0, keepdims=True)
        o_ref[t, mine, :] = yt * sga_ref[t, mine, :]


def _pattern_counts(t_steps, w_buf):
    delta = w_buf + np.arange(t_steps)[:, None] - np.arange(w_buf)[None, :]
    cnt = np.zeros((t_steps, w_buf), np.float32)
    for d in DILATIONS:
        cnt += (delta % d == 0) & (delta // d <= N_DIL_KEYS)
    return np.repeat(cnt, N_HEADS, axis=0)


def _sample_attn_operands(sample, grid, per_step):
    q_s, kn_s, vn_s, sga_s, kt, vt, first = sample
    t_steps = q_s.shape[0]
    w_buf = kt.shape[-1]
    n_local = grid[0] * grid[1] * grid[2] * per_step
    assert first % SUBLANES == 0 and SUBLANES % per_step == 0 and n_local >= CACHE_SLOTS
    cnt = jnp.asarray(_pattern_counts(t_steps, w_buf))
    step = lambda i, j, c: ((i * grid[1] + j) * grid[2] + c) * per_step
    tok_block = (t_steps, SUBLANES, D_ATT)
    tok_in = pl.BlockSpec(tok_block, lambda *g: (0, (first + step(*g)) // SUBLANES, 0))
    in_hbm = pl.BlockSpec(memory_space=pl.ANY)
    cnt_spec = pl.BlockSpec(cnt.shape, lambda *g: (0, 0), pipeline_mode=pl.Buffered(1))
    operands = (q_s, kn_s, vn_s, sga_s, cnt, kt, vt)
    in_specs = [tok_in, tok_in, tok_in, tok_in, cnt_spec, in_hbm, in_hbm]
    out_spec = pl.BlockSpec(tok_block, lambda *g: (0, step(*g) // SUBLANES, 0))
    out_shape = jax.ShapeDtypeStruct((t_steps, n_local, D_ATT), F32)
    scratch = [pltpu.VMEM((CACHE_SLOTS, D_ATT, w_buf), F32),
               pltpu.VMEM((CACHE_SLOTS, D_ATT, w_buf), F32),
               pltpu.SemaphoreType.DMA((2, CACHE_SLOTS))]
    return operands, in_specs, out_spec, out_shape, scratch


def _proj_lru_kernel(first_kept, first_sample, n_hosted, x_ref, w_ref, wg_ref, vec_ref,
                     sq_ref, skn_ref, svn_ref, ssga_ref, cnt_ref, kt_hbm, vt_hbm,
                     q_ref, k_ref, v_ref, hl_ref, sga_ref, conv_ref, hlast_ref,
                     kt_keep_ref, vt_keep_ref, sya_ref,
                     xb_ref, xc_ref, a_ref, b_ref, carry_ref, kt_ring, vt_ring, cache_sems):
    j = pl.program_id(1)
    phase = pl.program_id(2)
    rows = x_ref.shape[0]
    kept = j >= first_kept
    sample_step = functools.partial(_sample_attn_step, first_sample, n_hosted, 1, 0, sq_ref, skn_ref,
                                    svn_ref, ssga_ref, cnt_ref, (kt_hbm, vt_hbm),
                                    (kt_ring, vt_ring), cache_sems, sya_ref)

    @pl.when(phase == 0)
    def _():
        @pl.when(j == 0)
        def _():
            xc_ref[...] = jnp.zeros((8, D_LRU), F32)
            carry_ref[...] = jnp.zeros((1, D_LRU), F32)

        sample_step()
        xb_ref[...] = x_ref[...].astype(BF16)
        xl = _dot(xb_ref[...], w_ref[:, C_XL:C_GL])
        tail = xc_ref[...]
        xc_ref[...] = xl[rows - 8:, :]
        row8 = lax.broadcasted_iota(jnp.int32, (8, D_LRU), 0)
        u = vec_ref[4:5, :] + xl * vec_ref[3:4, :]
        for s in range(1, CONV_WIDTH):
            rolled = pltpu.roll(xl, s, axis=0)
            head = jnp.where(row8 < s, pltpu.roll(tail, s, axis=0), rolled[0:8, :])
            u = u + jnp.concatenate([head, rolled[8:, :]], axis=0) * vec_ref[3 - s:4 - s, :]
        a, bx = _lru_gates(u, wg_ref, vec_ref)
        a_ref[...] = a
        b_ref[...] = bx
        q_ref[...] = _dot(xb_ref[...], w_ref[:, C_Q:C_K]) * LOG2_E
        k_new = _dot(xb_ref[...], w_ref[:, C_K:C_V])
        k_ref[...] = k_new
        v_new = _dot(xb_ref[...], w_ref[:, C_V:C_GA])
        v_ref[...] = v_new

        @pl.when(kept)
        def _():
            kt_keep_ref[...] = k_new.T
            vt_keep_ref[...] = v_new.T

    @pl.when(phase == 1)
    def _():
        sample_step()
        sga_ref[...] = _silu(_dot(xb_ref[...], w_ref[:, C_GA:C_XL])).astype(BF16)
        sgl = _silu(_dot(xb_ref[...], w_ref[:, C_GL:D_IN]))

        row = lax.broadcasted_iota(jnp.int32, (8, D_LRU), 0)

        def group(g, carry):
            r0 = pl.multiple_of(g * 8, 8)
            av = a_ref[pl.ds(r0, 8), :]
            bv = b_ref[pl.ds(r0, 8), :]
            for s in (1, 2, 4):
                keep = row >= s
                bv = jnp.where(keep, av * pltpu.roll(bv, s, axis=0) + bv, bv)
                av = jnp.where(keep, av * pltpu.roll(av, s, axis=0), av)
            h8 = av * carry + bv
            b_ref[pl.ds(r0, 8), :] = h8
            return h8[7:8, :]

        carry = lax.fori_loop(0, rows // 8, group, carry_ref[...], unroll=True)
        carry_ref[...] = carry
        hl_ref[...] = (b_ref[...] * sgl).astype(BF16)

        @pl.when(j == pl.num_programs(1) - 1)
        def _():
            conv_ref[...] = xc_ref[5:8, :]
            hlast_ref[...] = carry


def _proj_lru(x, w_bf, w_gate, vecs, sample, n_keep):
    b, s, _ = x.shape
    nt = s // PROJ_ROWS
    assert n_keep % PROJ_ROWS == 0
    first_kept = nt - n_keep // PROJ_ROWS
    keep_spec = pl.BlockSpec((None, D_ATT, PROJ_ROWS),
                             lambda i, j, c: (i, 0, jnp.maximum(j - first_kept, 0)))
    keep_shape = jax.ShapeDtypeStruct((b, D_ATT, n_keep), F32)
    grid = (b, nt, PROJ_PHASES)
    s_ops, s_in, s_out, s_shape, s_scratch = _sample_attn_operands(sample, grid, 1)
    row_spec = lambda width: pl.BlockSpec((None, PROJ_ROWS, width), lambda i, j, c: (i, j, 0))
    const = lambda shape: pl.BlockSpec(shape, lambda i, j, c: (0,) * len(shape),
                                       pipeline_mode=pl.Buffered(1))
    per_batch = lambda n: pl.BlockSpec((None, n, D_LRU), lambda i, j, c: (i, 0, 0))
    tile = lambda dt: jax.ShapeDtypeStruct((b, s, D_ATT), dt)
    return pl.pallas_call(
        functools.partial(_proj_lru_kernel, first_kept, sample[-1], b * nt * PROJ_PHASES),
        grid=grid,
        in_specs=[row_spec(D_MODEL), const((D_MODEL, D_IN)),
                  const((D_LRU, 2 * D_LRU)), const((8, D_LRU))] + s_in,
        out_specs=[row_spec(D_ATT), row_spec(D_ATT), row_spec(D_ATT),
                   row_spec(D_LRU), row_spec(D_ATT),
                   per_batch(CONV_WIDTH - 1), per_batch(1), keep_spec, keep_spec, s_out],
        out_shape=[tile(F32), tile(F32), tile(F32), tile(BF16), tile(BF16),
                   jax.ShapeDtypeStruct((b, CONV_WIDTH - 1, D_LRU), F32),
                   jax.ShapeDtypeStruct((b, 1, D_LRU), F32), keep_shape, keep_shape, s_shape],
        scratch_shapes=[pltpu.VMEM((PROJ_ROWS, D_MODEL), BF16),
                        pltpu.VMEM((8, D_LRU), F32),
                        pltpu.VMEM((PROJ_ROWS, D_LRU), F32),
                        pltpu.VMEM((PROJ_ROWS, D_LRU), F32),
                        pltpu.VMEM((1, D_LRU), F32)] + s_scratch,
        compiler_params=pltpu.CompilerParams(
            dimension_semantics=("arbitrary", "arbitrary", "arbitrary"),
            vmem_limit_bytes=VMEM_LIMIT),
        name="prompt_proj_lru",
    )(x, w_bf, w_gate, vecs, *s_ops)


def _prompt_attn_kernel(first_sample, n_hosted, q_ref, k_ref, v_ref, sga_ref,
                        sq_ref, skn_ref, svn_ref, ssga_ref, cnt_ref, kt_hbm, vt_hbm,
                        o_ref, sya_ref, bias_ref, *scratch):
    seq = q_ref.shape[0]
    phase = pl.program_id(2)
    n_groups = seq // BAND
    head_a = lax.broadcasted_iota(jnp.int32, (BAND, HEAD_PAIR), 1) < HEAD_DIM
    one = jnp.ones((), BF16)

    stats = scratch[:6]

    def sample_steps():
        for which in range(ATTN_SAMPLE_ROWS):
            _sample_attn_step(first_sample, n_hosted, ATTN_SAMPLE_ROWS, which, sq_ref, skn_ref,
                              svn_ref, ssga_ref, cnt_ref, (kt_hbm, vt_hbm), scratch[6:8],
                              scratch[8], sya_ref)

    def key_value_block(rows):
        v_bf = v_ref[rows, :].astype(BF16)
        v_ext = jnp.concatenate([jnp.where(head_a, v_bf, one),
                                 jnp.where(head_a, one, v_bf)], axis=1)
        return k_ref[rows, :].astype(BF16), v_ext

    def group(rows, blk, carry):
        k_prev, v_prev = carry
        k_cur, v_cur = key_value_block(rows)
        k2 = jnp.concatenate([k_prev, k_cur], axis=0)
        v4 = jnp.concatenate([v_prev, v_cur], axis=0)
        qf = q_ref[rows, :]
        q2 = jnp.concatenate([jnp.where(head_a, qf, 0.0),
                              jnp.where(head_a, 0.0, qf)], axis=0).astype(BF16)
        s = _dot_nt(q2, k2).reshape(2, BAND, 2 * BAND)
        s = s + bias_ref[jnp.minimum(blk, 1)][None]
        m = jnp.max(s, axis=-1, keepdims=True)
        p = jnp.exp2(s - m).reshape(2 * BAND, 2 * BAND).astype(BF16)
        o = _dot(p, v4)
        acc = jnp.where(head_a, o[:BAND, :HEAD_PAIR], o[BAND:, HEAD_PAIR:])
        den = jnp.where(head_a, o[:BAND, HEAD_PAIR:], o[BAND:, :HEAD_PAIR])
        return acc, den, jnp.where(head_a, m[0], m[1]), (k_cur, v_cur)

    zero_block = (jnp.zeros((BAND, HEAD_PAIR), BF16), jnp.zeros((BAND, 2 * HEAD_PAIR), BF16))

    def strided_pattern(d, first, count, acc_ref, l_ref, m_ref):
        span = BAND * d
        n_blk = seq // span

        def body(i, carry):
            g = first + i
            blk = g % n_blk
            rows = pl.ds(blk * span + g // n_blk, BAND, stride=d)
            acc, den, m, carry = group(rows, blk, carry)
            acc_ref[rows, :] = acc
            l_ref[rows, :] = den
            m_ref[rows, :] = m
            return carry

        lax.fori_loop(0, count, body, zero_block, unroll=True)

    half = n_groups // 2

    @pl.when(phase == 0)
    def _():
        row = lax.broadcasted_iota(jnp.int32, (BAND, 2 * BAND), 0)
        col = lax.broadcasted_iota(jnp.int32, (BAND, 2 * BAND), 1)
        band = (col >= row) & (col <= row + N_DIL_KEYS)
        bias_ref[0] = jnp.where(band & (col >= BAND), 0.0, NEG_INF)
        bias_ref[1] = jnp.where(band, 0.0, NEG_INF)
        sample_steps()
        strided_pattern(DILATIONS[1], 0, n_groups, *stats[0:3])
        strided_pattern(DILATIONS[2], 0, half, *stats[3:6])

    @pl.when(phase == 1)
    def _():
        sample_steps()
        strided_pattern(DILATIONS[2], half, half, *stats[3:6])

        def body(g, carry):
            rows = pl.ds(pl.multiple_of(g * BAND, BAND), BAND)
            acc, den, m, carry = group(rows, g, carry)
            parts = [(acc, den, m)] + [(stats[3 * t][rows, :], stats[3 * t + 1][rows, :],
                                        stats[3 * t + 2][rows, :]) for t in range(2)]
            m_max = jnp.maximum(jnp.maximum(parts[0][2], parts[1][2]), parts[2][2])
            e = [jnp.exp2(pt[2] - m_max) for pt in parts]
            num = sum(ei * pt[0] for ei, pt in zip(e, parts))
            tot = sum(ei * pt[1] for ei, pt in zip(e, parts))
            o_ref[rows, :] = (num / tot * sga_ref[rows, :].astype(F32)).astype(BF16)
            return carry

        lax.fori_loop(0, n_groups, body, zero_block, unroll=True)


def _prompt_attn(q, k, v, sga, sample):
    b, s, _ = q.shape
    grid = (b, D_ATT // HEAD_PAIR, ATTN_PHASES)
    s_ops, s_in, s_out, s_shape, s_scratch = _sample_attn_operands(sample, grid, ATTN_SAMPLE_ROWS)
    spec = pl.BlockSpec((None, s, HEAD_PAIR), lambda i, j, c: (i, 0, j))
    return pl.pallas_call(
        functools.partial(_prompt_attn_kernel, sample[-1],
                          grid[0] * grid[1] * grid[2] * ATTN_SAMPLE_ROWS),
        grid=grid,
        in_specs=[spec, spec, spec, spec] + s_in,
        out_specs=[spec, s_out],
        out_shape=[jax.ShapeDtypeStruct((b, s, D_ATT), BF16), s_shape],
        scratch_shapes=[pltpu.VMEM((2, BAND, 2 * BAND), F32)]
                       + [pltpu.VMEM((s, HEAD_PAIR), F32) for _ in range(6)] + s_scratch,
        compiler_params=pltpu.CompilerParams(
            dimension_semantics=("arbitrary", "arbitrary", "arbitrary"),
            vmem_limit_bytes=VMEM_LIMIT),
        name="prompt_attn",
    )(q, k, v, sga, *s_ops)


def _finish_kernel(ya_ref, hl_ref, x_ref, wo_ref, ln_ref, o_ref):
    sub = (_dot(ya_ref[...], wo_ref[0:D_ATT, :])
           + _dot(hl_ref[...], wo_ref[D_ATT:D_ATT + D_LRU, :]))
    z = ALPHA * x_ref[...] + sub
    mu = jnp.mean(z, axis=-1, keepdims=True)
    zc = z - mu
    var = jnp.mean(zc * zc, axis=-1, keepdims=True)
    o_ref[...] = zc * lax.rsqrt(var + LN_EPS) * ln_ref[0:1, :] + ln_ref[1:2, :]


def _finish(ya, hl, x, w_o, ln, name):
    n = x.shape[0]
    rows = min(n, FINISH_ROWS)
    row_spec = lambda width: pl.BlockSpec((rows, width), lambda i: (i, 0))
    const = lambda shape: pl.BlockSpec(shape, lambda i: (0, 0),
                                       pipeline_mode=pl.Buffered(1))
    return pl.pallas_call(
        _finish_kernel,
        grid=(n // rows,),
        in_specs=[row_spec(D_ATT), row_spec(D_LRU), row_spec(D_MODEL),
                  const((D_ATT + D_LRU, D_MODEL)), const((2, D_MODEL))],
        out_specs=row_spec(D_MODEL),
        out_shape=jax.ShapeDtypeStruct((n, D_MODEL), F32),
        compiler_params=pltpu.CompilerParams(
            dimension_semantics=("arbitrary",), vmem_limit_bytes=VMEM_LIMIT),
        name=name,
    )(ya, hl, x, w_o, ln)


def _sample_proj_kernel(x_ref, sc_ref, sh_ref, w_ref, wg_ref, vec_ref,
                        q_ref, k_ref, v_ref, hl_ref, sga_ref, conv_ref, hlast_ref, wbf_ref):
    t_steps, nb, _ = x_ref.shape
    wbf_ref[:, C_Q:C_K] = (w_ref[:, C_Q:C_K] * HEAD_DIM ** -0.5).astype(BF16)
    wbf_ref[:, C_K:D_IN] = w_ref[:, C_K:D_IN].astype(BF16)
    xb = x_ref[...].reshape(t_steps * nb, D_MODEL).astype(BF16)
    split = lambda y: y.reshape(t_steps, nb, y.shape[-1])
    q_ref[...] = split(_dot(xb, wbf_ref[:, C_Q:C_K]))
    k_ref[...] = split(_dot(xb, wbf_ref[:, C_K:C_V]))
    v_ref[...] = split(_dot(xb, wbf_ref[:, C_V:C_GA]))
    sga_ref[...] = split(_silu(_dot(xb, wbf_ref[:, C_GA:C_XL])))

    xl = _dot(xb, wbf_ref[:, C_XL:C_GL])
    xc = [sc_ref[i] for i in range(CONV_WIDTH - 1)]
    xc += [xl[t * nb:(t + 1) * nb, :] for t in range(t_steps)]
    u = jnp.concatenate(
        [vec_ref[4:5, :] + sum(xc[t + w] * vec_ref[w:w + 1, :] for w in range(CONV_WIDTH))
         for t in range(t_steps)], axis=0)
    for i in range(CONV_WIDTH - 1):
        conv_ref[i] = xc[t_steps + i]

    a, bx = _lru_gates(u, wg_ref, vec_ref)
    sgl = _silu(_dot(xb, wbf_ref[:, C_GL:D_IN]))
    h = sh_ref[...]
    for t in range(t_steps):
        rows = slice(t * nb, (t + 1) * nb)
        h = a[rows, :] * h + bx[rows, :]
        hl_ref[t] = (h * sgl[rows, :]).astype(BF16)
    hlast_ref[...] = h


def _sample_proj(xs_t, sc_t, sh, w_f32, w_gate, vecs):
    t_steps, nb, _ = xs_t.shape
    full = lambda shape: pl.BlockSpec(shape, lambda i: (0,) * len(shape),
                                      pipeline_mode=pl.Buffered(1))
    tok = lambda dt: jax.ShapeDtypeStruct((t_steps, nb, D_ATT), dt)
    return pl.pallas_call(
        _sample_proj_kernel,
        grid=(1,),
        in_specs=[full(xs_t.shape), full(sc_t.shape), full(sh.shape),
                  full(w_f32.shape), full(w_gate.shape), full(vecs.shape)],
        out_specs=[full((t_steps, nb, D_ATT))] * 5
                  + [full(sc_t.shape), full(sh.shape), full(w_f32.shape)],
        out_shape=[tok(F32), tok(F32), tok(F32), tok(BF16), tok(F32),
                   jax.ShapeDtypeStruct(sc_t.shape, F32),
                   jax.ShapeDtypeStruct(sh.shape, F32),
                   jax.ShapeDtypeStruct(w_f32.shape, BF16)],
        compiler_params=pltpu.CompilerParams(
            dimension_semantics=("arbitrary",), vmem_limit_bytes=VMEM_LIMIT),
        name="sample_proj_lru",
    )(xs_t, sc_t, sh, w_f32, w_gate, vecs)


def kernel(x_prompt, x_sample, cache_k, cache_v, state_conv, state_h, w_in, conv_w, conv_b, w_ra, b_ra, w_ri, b_ri, lru_lambda, w_out, ln_g, ln_b):
    assert w_in.shape[0] == 1, "single layer"
    b, s, _ = x_prompt.shape
    nb, t_steps, _ = x_sample.shape
    w_buf = cache_k.shape[2]

    eye = jnp.eye(LRU_BLOCKS, dtype=F32)
    block_diag = lambda wb: (eye[:, None, :, None] * wb[:, :, None, :]).reshape(D_LRU, D_LRU)
    w_gate = (0.5 * jnp.concatenate([block_diag(w_ra[0]), block_diag(w_ri[0])], axis=1)).astype(BF16)
    vecs = jnp.concatenate([conv_w[0], conv_b, b_ra, b_ri, lru_lambda], axis=0)
    w_o = w_out[0].astype(BF16)
    ln = jnp.concatenate([ln_g, ln_b], axis=0)
    heads = lambda a: a.reshape(a.shape[:-1] + (N_HEADS, HEAD_DIM))

    xs_t = jnp.transpose(x_sample, (1, 0, 2))
    sc_t = jnp.transpose(state_conv[0], (1, 0, 2))
    q_s, k_s, v_s, hl_s, sga_s, conv_s, h_s, w_bf = _sample_proj(
        xs_t, sc_t, state_h[0], w_in[0], w_gate, vecs)
    kt = jnp.transpose(cache_k[0], (0, 2, 3, 1)).reshape(nb, D_ATT, w_buf)
    vt = jnp.transpose(cache_v[0], (0, 2, 3, 1)).reshape(nb, D_ATT, w_buf)
    sample = (q_s, k_s, v_s, sga_s, kt, vt)

    n_proj = b * (s // PROJ_ROWS) * PROJ_PHASES
    n_attn = b * (D_ATT // HEAD_PAIR) * ATTN_PHASES * ATTN_SAMPLE_ROWS
    assert n_proj + n_attn == nb
    n_keep = min(w_buf, s)
    q, k, v, hl, sga, conv_p, h_p, kt_p, vt_p, ya_s0 = _proj_lru(
        x_prompt, w_bf, w_gate, vecs, sample + (0,), n_keep)
    ya, ya_s1 = _prompt_attn(q, k, v, sga, sample + (n_proj,))
    y_p = _finish(ya.reshape(b * s, D_ATT), hl.reshape(b * s, D_LRU),
                  x_prompt.reshape(b * s, D_MODEL), w_o, ln, "prompt_finish")
    window = lambda a: jnp.transpose(a.reshape(b, N_HEADS, HEAD_DIM, n_keep), (0, 3, 1, 2))[None]
    k_p, v_p = window(kt_p), window(vt_p)
    ya_s = jnp.concatenate([ya_s0, ya_s1], axis=1)
    y_s = _finish(ya_s.reshape(t_steps * nb, D_ATT).astype(BF16),
                  hl_s.reshape(t_steps * nb, D_LRU),
                  xs_t.reshape(t_steps * nb, D_MODEL), w_o, ln, "sample_finish")
    to_batch_major = lambda a: jnp.transpose(a, (1, 0, 2))

    return (y_p.reshape(b, s, D_MODEL),
            to_batch_major(y_s.reshape(t_steps, nb, D_MODEL)),
            k_p, v_p,
            conv_p[None], h_p.reshape(1, b, D_LRU),
            heads(to_batch_major(k_s))[None], heads(to_batch_major(v_s))[None],
            to_batch_major(conv_s)[None], h_s[None])
```

```python
import functools

import numpy as np
import jax
import jax.numpy as jnp
from jax import lax
from jax.experimental import pallas as pl
from jax.experimental.pallas import tpu as pltpu

F32 = jnp.float32
BF16 = jnp.bfloat16

D_MODEL = 1024
N_HEADS = 8
HEAD_DIM = 64
D_ATT = N_HEADS * HEAD_DIM
D_LRU = 512
LRU_BLOCKS = 8
D_IN = 4 * D_ATT + 2 * D_LRU
CONV_WIDTH = 4
LRU_C = 8.0
DILATIONS = (1, 4, 16)
N_DIL_KEYS = 128
BAND = 128
ALPHA = 2.0 ** 0.25
LN_EPS = 1e-5
NEG_INF = -1e30
LOG2_E = 1.4426950408889634

C_Q, C_K, C_V, C_GA, C_XL, C_GL = (0, D_ATT, 2 * D_ATT, 3 * D_ATT, 4 * D_ATT,
                                   4 * D_ATT + D_LRU)

VMEM_LIMIT = 60 * 1024 * 1024
PROJ_ROWS = 512
FINISH_ROWS = 2048
HEAD_PAIR = 2 * HEAD_DIM
SUBLANES = 8
CACHE_SLOTS = 3
PROJ_PHASES = 2
ATTN_SAMPLE_ROWS = 2
ATTN_PHASES = 2


def _dot(a, b):
    return jnp.dot(a, b, preferred_element_type=F32)


def _dot_nt(a, b):
    return lax.dot_general(a, b, (((1,), (1,)), ((), ())),
                           preferred_element_type=F32)


def _sigmoid(x):
    return 0.5 * jnp.tanh(0.5 * x) + 0.5


def _silu(x):
    return x * _sigmoid(x)


def _softplus(x):
    return jnp.maximum(x, 0.0) + jnp.log1p(jnp.exp(-jnp.abs(x)))


def _lru_gates(u, wg_ref, vec_ref):
    g = _dot(u.astype(BF16), wg_ref[...])
    tr = jnp.tanh(g[:, :D_LRU] + 0.5 * vec_ref[5:6, :])
    ig = 0.5 * jnp.tanh(g[:, D_LRU:] + 0.5 * vec_ref[6:7, :]) + 0.5
    half_c = (-0.5 * LRU_C) * _softplus(-vec_ref[7:8, :])
    log_a = half_c * tr + half_c
    a = jnp.exp(log_a)
    x = jnp.tanh(-log_a) * (1.0 + a * a)
    bx = jnp.where(x > 0.0, x * lax.rsqrt(x), 0.0) * ig * u
    return a, bx


def _grid_step():
    return ((pl.program_id(0) * pl.num_programs(1) + pl.program_id(1)) * pl.num_programs(2)
            + pl.program_id(2))


def _cache_copies(row, cache_hbm, ring, sems):
    slot = row % CACHE_SLOTS
    return [pltpu.make_async_copy(src.at[row], dst.at[slot], sems.at[which, slot])
            for which, (src, dst) in enumerate(zip(cache_hbm, ring))]


def _sample_attn_step(first, n_rows_hosted, per_step, which, q_ref, kn_ref, vn_ref, sga_ref,
                      cnt_ref, cache_hbm, ring, sems, o_ref):
    t_steps = q_ref.shape[0]
    local = _grid_step() * per_step + which
    lookahead = CACHE_SLOTS - 1
    if which == 0:
        @pl.when(local == 0)
        def _():
            for early in range(lookahead):
                for copy in _cache_copies(first + early, cache_hbm, ring, sems):
                    copy.start()
    for copy in _cache_copies(first + local, cache_hbm, ring, sems):
        copy.wait()

    @pl.when(local + lookahead < n_rows_hosted)
    def _():
        for copy in _cache_copies(first + local + lookahead, cache_hbm, ring, sems):
            copy.start()

    kt_ref, vt_ref = (buf.at[(first + local) % CACHE_SLOTS] for buf in ring)
    mine = pl.ds(local % SUBLANES, 1)
    n_rows = t_steps * N_HEADS
    sub = lax.broadcasted_iota(jnp.int32, (N_HEADS, D_ATT), 0)
    lane_head = lax.broadcasted_iota(jnp.int32, (N_HEADS, D_ATT), 1) // HEAD_DIM
    own = sub == lane_head
    qbd = jnp.concatenate(
        [jnp.where(own, jnp.broadcast_to(q_ref[t, mine, :], (N_HEADS, D_ATT)), 0.0)
         for t in range(t_steps)], axis=0)

    cnt = cnt_ref[...]
    s = jnp.where(cnt > 0.0, _dot(qbd.astype(BF16), kt_ref[...].astype(BF16)), NEG_INF)
    m = jnp.max(s, axis=-1, keepdims=True)

    row_t = lax.broadcasted_iota(jnp.int32, (n_rows, 1), 0) // N_HEADS
    s_new, c_new = [], []
    for t2 in range(t_steps):
        c = jnp.where(row_t == t2, float(len(DILATIONS)),
                      jnp.where(row_t > t2, 1.0, 0.0))
        sn = jnp.sum(qbd * kn_ref[t2, mine, :], axis=-1, keepdims=True)
        sn = jnp.where(c > 0.0, sn, NEG_INF)
        m = jnp.maximum(m, sn)
        s_new.append(sn)
        c_new.append(c)

    p = jnp.exp(s - m) * cnt
    den = jnp.sum(p, axis=-1, keepdims=True)
    acc = _dot_nt(p.astype(BF16), vt_ref[...].astype(BF16))
    for t2 in range(t_steps):
        pn = jnp.exp(s_new[t2] - m) * c_new[t2]
        den = den + pn
        acc = acc + pn * vn_ref[t2, mine, :]
    y = acc / den
    for t in range(t_steps):
        yt = jnp.sum(jnp.where(own, y[t * N_HEADS:(t + 1) * N_HEADS, :], 0.0),
                     axis=0, keepdims=True)
        o_ref[t, mine, :] = yt * sga_ref[t, mine, :]


def _pattern_counts(t_steps, w_buf):
    delta = w_buf + np.arange(t_steps)[:, None] - np.arange(w_buf)[None, :]
    cnt = np.zeros((t_steps, w_buf), np.float32)
    for d in DILATIONS:
        cnt += (delta % d == 0) & (delta // d <= N_DIL_KEYS)
    return np.repeat(cnt, N_HEADS, axis=0)


def _sample_attn_operands(sample, grid, per_step):
    q_s, kn_s, vn_s, sga_s, kt, vt, first = sample
    t_steps = q_s.shape[0]
    w_buf = kt.shape[-1]
    n_local = grid[0] * grid[1] * grid[2] * per_step
    assert first % SUBLANES == 0 and SUBLANES % per_step == 0 and n_local >= CACHE_SLOTS
    cnt = jnp.asarray(_pattern_counts(t_steps, w_buf))
    step = lambda i, j, c: ((i * grid[1] + j) * grid[2] + c) * per_step
    tok_block = (t_steps, SUBLANES, D_ATT)
    tok_in = pl.BlockSpec(tok_block, lambda *g: (0, (first + step(*g)) // SUBLANES, 0))
    in_hbm = pl.BlockSpec(memory_space=pl.ANY)
    cnt_spec = pl.BlockSpec(cnt.shape, lambda *g: (0, 0), pipeline_mode=pl.Buffered(1))
    operands = (q_s, kn_s, vn_s, sga_s, cnt, kt, vt)
    in_specs = [tok_in, tok_in, tok_in, tok_in, cnt_spec, in_hbm, in_hbm]
    out_spec = pl.BlockSpec(tok_block, lambda *g: (0, step(*g) // SUBLANES, 0))
    out_shape = jax.ShapeDtypeStruct((t_steps, n_local, D_ATT), F32)
    scratch = [pltpu.VMEM((CACHE_SLOTS, D_ATT, w_buf), F32),
               pltpu.VMEM((CACHE_SLOTS, D_ATT, w_buf), F32),
               pltpu.SemaphoreType.DMA((2, CACHE_SLOTS))]
    return operands, in_specs, out_spec, out_shape, scratch


def _proj_lru_kernel(first_kept, first_sample, n_hosted, x_ref, w_ref, wg_ref, vec_ref,
                     sq_ref, skn_ref, svn_ref, ssga_ref, cnt_ref, kt_hbm, vt_hbm,
                     q_ref, k_ref, v_ref, hl_ref, sga_ref, conv_ref, hlast_ref,
                     kt_keep_ref, vt_keep_ref, sya_ref,
                     xb_ref, xc_ref, a_ref, b_ref, carry_ref, kt_ring, vt_ring, cache_sems):
    j = pl.program_id(1)
    phase = pl.program_id(2)
    rows = x_ref.shape[0]
    kept = j >= first_kept
    sample_step = functools.partial(_sample_attn_step, first_sample, n_hosted, 1, 0, sq_ref, skn_ref,
                                    svn_ref, ssga_ref, cnt_ref, (kt_hbm, vt_hbm),
                                    (kt_ring, vt_ring), cache_sems, sya_ref)

    @pl.when(phase == 0)
    def _():
        @pl.when(j == 0)
        def _():
            xc_ref[...] = jnp.zeros((SUBLANES, D_LRU), F32)
            carry_ref[...] = jnp.zeros((1, D_LRU), F32)

        sample_step()
        xb_ref[...] = x_ref[...].astype(BF16)
        xl = _dot(xb_ref[...], w_ref[:, C_XL:C_GL])
        tail = xc_ref[...]
        xc_ref[...] = xl[rows - SUBLANES:, :]
        row8 = lax.broadcasted_iota(jnp.int32, (SUBLANES, D_LRU), 0)
        u = vec_ref[4:5, :] + xl * vec_ref[3:4, :]
        for s in range(1, CONV_WIDTH):
            rolled = pltpu.roll(xl, s, axis=0)
            head = jnp.where(row8 < s, pltpu.roll(tail, s, axis=0), rolled[0:SUBLANES, :])
            u = u + jnp.concatenate([head, rolled[SUBLANES:, :]], axis=0) * vec_ref[3 - s:4 - s, :]
        a, bx = _lru_gates(u, wg_ref, vec_ref)
        a_ref[...] = a
        b_ref[...] = bx
        q_ref[...] = _dot(xb_ref[...], w_ref[:, C_Q:C_K]) * LOG2_E
        k_new = _dot(xb_ref[...], w_ref[:, C_K:C_V])
        k_ref[...] = k_new
        v_new = _dot(xb_ref[...], w_ref[:, C_V:C_GA])
        v_ref[...] = v_new

        @pl.when(kept)
        def _():
            kt_keep_ref[...] = k_new.T
            vt_keep_ref[...] = v_new.T

    @pl.when(phase == 1)
    def _():
        sample_step()
        sga_ref[...] = _silu(_dot(xb_ref[...], w_ref[:, C_GA:C_XL])).astype(BF16)
        sgl = _silu(_dot(xb_ref[...], w_ref[:, C_GL:D_IN]))

        row = lax.broadcasted_iota(jnp.int32, (SUBLANES, D_LRU), 0)

        def group(g, carry):
            r0 = pl.multiple_of(g * SUBLANES, SUBLANES)
            av = a_ref[pl.ds(r0, SUBLANES), :]
            bv = b_ref[pl.ds(r0, SUBLANES), :]
            for s in (1, 2, 4):
                keep = row >= s
                bv = jnp.where(keep, av * pltpu.roll(bv, s, axis=0) + bv, bv)
                av = jnp.where(keep, av * pltpu.roll(av, s, axis=0), av)
            h8 = av * carry + bv
            b_ref[pl.ds(r0, SUBLANES), :] = h8
            return h8[SUBLANES - 1:, :]

        carry = lax.fori_loop(0, rows // SUBLANES, group, carry_ref[...], unroll=True)
        carry_ref[...] = carry
        hl_ref[...] = (b_ref[...] * sgl).astype(BF16)

        @pl.when(j == pl.num_programs(1) - 1)
        def _():
            conv_ref[...] = xc_ref[SUBLANES - (CONV_WIDTH - 1):, :]
            hlast_ref[...] = carry


def _proj_lru(x, w_bf, w_gate, vecs, sample, n_keep):
    b, s, _ = x.shape
    nt = s // PROJ_ROWS
    assert n_keep % PROJ_ROWS == 0
    first_kept = nt - n_keep // PROJ_ROWS
    keep_spec = pl.BlockSpec((None, D_ATT, PROJ_ROWS),
                             lambda i, j, c: (i, 0, jnp.maximum(j - first_kept, 0)))
    keep_shape = jax.ShapeDtypeStruct((b, D_ATT, n_keep), F32)
    grid = (b, nt, PROJ_PHASES)
    s_ops, s_in, s_out, s_shape, s_scratch = _sample_attn_operands(sample, grid, 1)
    row_spec = lambda width: pl.BlockSpec((None, PROJ_ROWS, width), lambda i, j, c: (i, j, 0))
    const = lambda shape: pl.BlockSpec(shape, lambda i, j, c: (0,) * len(shape),
                                       pipeline_mode=pl.Buffered(1))
    per_batch = lambda n: pl.BlockSpec((None, n, D_LRU), lambda i, j, c: (i, 0, 0))
    tile = lambda dt: jax.ShapeDtypeStruct((b, s, D_ATT), dt)
    return pl.pallas_call(
        functools.partial(_proj_lru_kernel, first_kept, sample[-1], b * nt * PROJ_PHASES),
        grid=grid,
        in_specs=[row_spec(D_MODEL), const((D_MODEL, D_IN)),
                  const((D_LRU, 2 * D_LRU)), const(vecs.shape)] + s_in,
        out_specs=[row_spec(D_ATT), row_spec(D_ATT), row_spec(D_ATT),
                   row_spec(D_LRU), row_spec(D_ATT),
                   per_batch(CONV_WIDTH - 1), per_batch(1), keep_spec, keep_spec, s_out],
        out_shape=[tile(F32), tile(F32), tile(F32), tile(BF16), tile(BF16),
                   jax.ShapeDtypeStruct((b, CONV_WIDTH - 1, D_LRU), F32),
                   jax.ShapeDtypeStruct((b, 1, D_LRU), F32), keep_shape, keep_shape, s_shape],
        scratch_shapes=[pltpu.VMEM((PROJ_ROWS, D_MODEL), BF16),
                        pltpu.VMEM((SUBLANES, D_LRU), F32),
                        pltpu.VMEM((PROJ_ROWS, D_LRU), F32),
                        pltpu.VMEM((PROJ_ROWS, D_LRU), F32),
                        pltpu.VMEM((1, D_LRU), F32)] + s_scratch,
        compiler_params=pltpu.CompilerParams(
            dimension_semantics=("arbitrary", "arbitrary", "arbitrary"),
            vmem_limit_bytes=VMEM_LIMIT),
        name="prompt_proj_lru",
    )(x, w_bf, w_gate, vecs, *s_ops)


def _prompt_attn_kernel(first_sample, n_hosted, q_ref, k_ref, v_ref, sga_ref,
                        sq_ref, skn_ref, svn_ref, ssga_ref, cnt_ref, kt_hbm, vt_hbm,
                        o_ref, sya_ref, bias_ref, *scratch):
    seq = q_ref.shape[0]
    phase = pl.program_id(2)
    n_groups = seq // BAND
    head_a = lax.broadcasted_iota(jnp.int32, (BAND, HEAD_PAIR), 1) < HEAD_DIM
    one = jnp.ones((), BF16)

    stats = scratch[:6]

    def sample_steps():
        for which in range(ATTN_SAMPLE_ROWS):
            _sample_attn_step(first_sample, n_hosted, ATTN_SAMPLE_ROWS, which, sq_ref, skn_ref,
                              svn_ref, ssga_ref, cnt_ref, (kt_hbm, vt_hbm), scratch[6:8],
                              scratch[8], sya_ref)

    def key_value_block(rows):
        v_bf = v_ref[rows, :].astype(BF16)
        v_ext = jnp.concatenate([jnp.where(head_a, v_bf, one),
                                 jnp.where(head_a, one, v_bf)], axis=1)
        return k_ref[rows, :].astype(BF16), v_ext

    def group(rows, blk, carry):
        k_prev, v_prev = carry
        k_cur, v_cur = key_value_block(rows)
        k2 = jnp.concatenate([k_prev, k_cur], axis=0)
        v4 = jnp.concatenate([v_prev, v_cur], axis=0)
        qf = q_ref[rows, :]
        q2 = jnp.concatenate([jnp.where(head_a, qf, 0.0),
                              jnp.where(head_a, 0.0, qf)], axis=0).astype(BF16)
        s = _dot_nt(q2, k2).reshape(2, BAND, 2 * BAND)
        s = s + bias_ref[jnp.minimum(blk, 1)][None]
        m = jnp.max(s, axis=-1, keepdims=True)
        p = jnp.exp2(s - m).reshape(2 * BAND, 2 * BAND).astype(BF16)
        o = _dot(p, v4)
        acc = jnp.where(head_a, o[:BAND, :HEAD_PAIR], o[BAND:, HEAD_PAIR:])
        den = jnp.where(head_a, o[:BAND, HEAD_PAIR:], o[BAND:, :HEAD_PAIR])
        return acc, den, jnp.where(head_a, m[0], m[1]), (k_cur, v_cur)

    zero_block = (jnp.zeros((BAND, HEAD_PAIR), BF16), jnp.zeros((BAND, 2 * HEAD_PAIR), BF16))

    def strided_pattern(d, first, count, acc_ref, l_ref, m_ref):
        span = BAND * d
        n_blk = seq // span

        def body(i, carry):
            g = first + i
            blk = g % n_blk
            rows = pl.ds(blk * span + g // n_blk, BAND, stride=d)
            acc, den, m, carry = group(rows, blk, carry)
            acc_ref[rows, :] = acc
            l_ref[rows, :] = den
            m_ref[rows, :] = m
            return carry

        lax.fori_loop(0, count, body, zero_block, unroll=True)

    half = n_groups // 2

    @pl.when(phase == 0)
    def _():
        row = lax.broadcasted_iota(jnp.int32, (BAND, 2 * BAND), 0)
        col = lax.broadcasted_iota(jnp.int32, (BAND, 2 * BAND), 1)
        band = (col >= row) & (col <= row + N_DIL_KEYS)
        bias_ref[0] = jnp.where(band & (col >= BAND), 0.0, NEG_INF)
        bias_ref[1] = jnp.where(band, 0.0, NEG_INF)
        sample_steps()
        strided_pattern(DILATIONS[1], 0, n_groups, *stats[0:3])
        strided_pattern(DILATIONS[2], 0, half, *stats[3:6])

    @pl.when(phase == 1)
    def _():
        sample_steps()
        strided_pattern(DILATIONS[2], half, half, *stats[3:6])

        def body(g, carry):
            rows = pl.ds(pl.multiple_of(g * BAND, BAND), BAND)
            acc, den, m, carry = group(rows, g, carry)
            parts = [(acc, den, m)] + [(stats[3 * t][rows, :], stats[3 * t + 1][rows, :],
                                        stats[3 * t + 2][rows, :]) for t in range(2)]
            m_max = jnp.maximum(jnp.maximum(parts[0][2], parts[1][2]), parts[2][2])
            e = [jnp.exp2(pt[2] - m_max) for pt in parts]
            num = sum(ei * pt[0] for ei, pt in zip(e, parts))
            tot = sum(ei * pt[1] for ei, pt in zip(e, parts))
            o_ref[rows, :] = (num / tot * sga_ref[rows, :].astype(F32)).astype(BF16)
            return carry

        lax.fori_loop(0, n_groups, body, zero_block, unroll=True)


def _prompt_attn(q, k, v, sga, sample):
    b, s, _ = q.shape
    grid = (b, D_ATT // HEAD_PAIR, ATTN_PHASES)
    s_ops, s_in, s_out, s_shape, s_scratch = _sample_attn_operands(sample, grid, ATTN_SAMPLE_ROWS)
    spec = pl.BlockSpec((None, s, HEAD_PAIR), lambda i, j, c: (i, 0, j))
    return pl.pallas_call(
        functools.partial(_prompt_attn_kernel, sample[-1],
                          grid[0] * grid[1] * grid[2] * ATTN_SAMPLE_ROWS),
        grid=grid,
        in_specs=[spec, spec, spec, spec] + s_in,
        out_specs=[spec, s_out],
        out_shape=[jax.ShapeDtypeStruct((b, s, D_ATT), BF16), s_shape],
        scratch_shapes=[pltpu.VMEM((2, BAND, 2 * BAND), F32)]
                       + [pltpu.VMEM((s, HEAD_PAIR), F32) for _ in range(6)] + s_scratch,
        compiler_params=pltpu.CompilerParams(
            dimension_semantics=("arbitrary", "arbitrary", "arbitrary"),
            vmem_limit_bytes=VMEM_LIMIT),
        name="prompt_attn",
    )(q, k, v, sga, *s_ops)


def _finish_kernel(ya_ref, hl_ref, x_ref, wo_ref, ln_ref, o_ref):
    sub = (_dot(ya_ref[...], wo_ref[0:D_ATT, :])
           + _dot(hl_ref[...], wo_ref[D_ATT:D_ATT + D_LRU, :]))
    z = ALPHA * x_ref[...] + sub
    mu = jnp.mean(z, axis=-1, keepdims=True)
    zc = z - mu
    var = jnp.mean(zc * zc, axis=-1, keepdims=True)
    o_ref[...] = zc * lax.rsqrt(var + LN_EPS) * ln_ref[0:1, :] + ln_ref[1:2, :]


def _finish(ya, hl, x, w_o, ln, name):
    n = x.shape[0]
    rows = min(n, FINISH_ROWS)
    row_spec = lambda width: pl.BlockSpec((rows, width), lambda i: (i, 0))
    const = lambda shape: pl.BlockSpec(shape, lambda i: (0, 0),
                                       pipeline_mode=pl.Buffered(1))
    return pl.pallas_call(
        _finish_kernel,
        grid=(n // rows,),
        in_specs=[row_spec(D_ATT), row_spec(D_LRU), row_spec(D_MODEL),
                  const((D_ATT + D_LRU, D_MODEL)), const((2, D_MODEL))],
        out_specs=row_spec(D_MODEL),
        out_shape=jax.ShapeDtypeStruct((n, D_MODEL), F32),
        compiler_params=pltpu.CompilerParams(
            dimension_semantics=("arbitrary",), vmem_limit_bytes=VMEM_LIMIT),
        name=name,
    )(ya, hl, x, w_o, ln)


def _sample_proj_kernel(x_ref, sc_ref, sh_ref, w_ref, wg_ref, vec_ref,
                        q_ref, k_ref, v_ref, hl_ref, sga_ref, conv_ref, hlast_ref, wbf_ref):
    t_steps, nb, _ = x_ref.shape
    wbf_ref[:, C_Q:C_K] = (w_ref[:, C_Q:C_K] * HEAD_DIM ** -0.5).astype(BF16)
    wbf_ref[:, C_K:D_IN] = w_ref[:, C_K:D_IN].astype(BF16)
    xb = x_ref[...].reshape(t_steps * nb, D_MODEL).astype(BF16)
    split = lambda y: y.reshape(t_steps, nb, y.shape[-1])
    q_ref[...] = split(_dot(xb, wbf_ref[:, C_Q:C_K]))
    k_ref[...] = split(_dot(xb, wbf_ref[:, C_K:C_V]))
    v_ref[...] = split(_dot(xb, wbf_ref[:, C_V:C_GA]))
    sga_ref[...] = split(_silu(_dot(xb, wbf_ref[:, C_GA:C_XL])))

    xl = _dot(xb, wbf_ref[:, C_XL:C_GL])
    xc = [sc_ref[i] for i in range(CONV_WIDTH - 1)]
    xc += [xl[t * nb:(t + 1) * nb, :] for t in range(t_steps)]
    u = jnp.concatenate(
        [vec_ref[4:5, :] + sum(xc[t + w] * vec_ref[w:w + 1, :] for w in range(CONV_WIDTH))
         for t in range(t_steps)], axis=0)
    for i in range(CONV_WIDTH - 1):
        conv_ref[i] = xc[t_steps + i]

    a, bx = _lru_gates(u, wg_ref, vec_ref)
    sgl = _silu(_dot(xb, wbf_ref[:, C_GL:D_IN]))
    h = sh_ref[...]
    for t in range(t_steps):
        rows = slice(t * nb, (t + 1) * nb)
        h = a[rows, :] * h + bx[rows, :]
        hl_ref[t] = (h * sgl[rows, :]).astype(BF16)
    hlast_ref[...] = h


def _sample_proj(xs_t, sc_t, sh, w_f32, w_gate, vecs):
    t_steps, nb, _ = xs_t.shape
    full = lambda shape: pl.BlockSpec(shape, lambda i: (0,) * len(shape),
                                      pipeline_mode=pl.Buffered(1))
    tok = lambda dt: jax.ShapeDtypeStruct((t_steps, nb, D_ATT), dt)
    return pl.pallas_call(
        _sample_proj_kernel,
        grid=(1,),
        in_specs=[full(xs_t.shape), full(sc_t.shape), full(sh.shape),
                  full(w_f32.shape), full(w_gate.shape), full(vecs.shape)],
        out_specs=[full((t_steps, nb, D_ATT))] * 5
                  + [full(sc_t.shape), full(sh.shape), full(w_f32.shape)],
        out_shape=[tok(F32), tok(F32), tok(F32), tok(BF16), tok(F32),
                   jax.ShapeDtypeStruct(sc_t.shape, F32),
                   jax.ShapeDtypeStruct(sh.shape, F32),
                   jax.ShapeDtypeStruct(w_f32.shape, BF16)],
        compiler_params=pltpu.CompilerParams(
            dimension_semantics=("arbitrary",), vmem_limit_bytes=VMEM_LIMIT),
        name="sample_proj_lru",
    )(xs_t, sc_t, sh, w_f32, w_gate, vecs)


def kernel(x_prompt, x_sample, cache_k, cache_v, state_conv, state_h, w_in, conv_w, conv_b, w_ra, b_ra, w_ri, b_ri, lru_lambda, w_out, ln_g, ln_b):
    assert w_in.shape[0] == 1, "single layer"
    b, s, _ = x_prompt.shape
    nb, t_steps, _ = x_sample.shape
    w_buf = cache_k.shape[2]

    eye = jnp.eye(LRU_BLOCKS, dtype=F32)
    block_diag = lambda wb: (eye[:, None, :, None] * wb[:, :, None, :]).reshape(D_LRU, D_LRU)
    w_gate = (0.5 * jnp.concatenate([block_diag(w_ra[0]), block_diag(w_ri[0])], axis=1)).astype(BF16)
    vecs = jnp.concatenate([conv_w[0], conv_b, b_ra, b_ri, lru_lambda], axis=0)
    w_o = w_out[0].astype(BF16)
    ln = jnp.concatenate([ln_g, ln_b], axis=0)
    heads = lambda a: a.reshape(a.shape[:-1] + (N_HEADS, HEAD_DIM))

    xs_t = jnp.transpose(x_sample, (1, 0, 2))
    sc_t = jnp.transpose(state_conv[0], (1, 0, 2))
    q_s, k_s, v_s, hl_s, sga_s, conv_s, h_s, w_bf = _sample_proj(
        xs_t, sc_t, state_h[0], w_in[0], w_gate, vecs)
    kt = jnp.transpose(cache_k[0], (0, 2, 3, 1)).reshape(nb, D_ATT, w_buf)
    vt = jnp.transpose(cache_v[0], (0, 2, 3, 1)).reshape(nb, D_ATT, w_buf)
    sample = (q_s, k_s, v_s, sga_s, kt, vt)

    n_proj = b * (s // PROJ_ROWS) * PROJ_PHASES
    n_attn = b * (D_ATT // HEAD_PAIR) * ATTN_PHASES * ATTN_SAMPLE_ROWS
    assert n_proj + n_attn == nb
    n_keep = min(w_buf, s)
    q, k, v, hl, sga, conv_p, h_p, kt_p, vt_p, ya_s0 = _proj_lru(
        x_prompt, w_bf, w_gate, vecs, sample + (0,), n_keep)
    ya, ya_s1 = _prompt_attn(q, k, v, sga, sample + (n_proj,))
    y_p = _finish(ya.reshape(b * s, D_ATT), hl.reshape(b * s, D_LRU),
                  x_prompt.reshape(b * s, D_MODEL), w_o, ln, "prompt_finish")
    window = lambda a: jnp.transpose(a.reshape(b, N_HEADS, HEAD_DIM, n_keep), (0, 3, 1, 2))[None]
    k_p, v_p = window(kt_p), window(vt_p)
    ya_s = jnp.concatenate([ya_s0, ya_s1], axis=1)
    y_s = _finish(ya_s.reshape(t_steps * nb, D_ATT).astype(BF16),
                  hl_s.reshape(t_steps * nb, D_LRU),
                  xs_t.reshape(t_steps * nb, D_MODEL), w_o, ln, "sample_finish")
    to_batch_major = lambda a: jnp.transpose(a, (1, 0, 2))

    return (y_p.reshape(b, s, D_MODEL),
            to_batch_major(y_s.reshape(t_steps, nb, D_MODEL)),
            k_p, v_p,
            conv_p[None], h_p.reshape(1, b, D_LRU),
            heads(to_batch_major(k_s))[None], heads(to_batch_major(v_s))[None],
            to_batch_major(conv_s)[None], h_s[None])
```

```python
import functools

import numpy as np
import jax
import jax.numpy as jnp
from jax import lax
from jax.experimental import pallas as pl
from jax.experimental.pallas import tpu as pltpu

F32 = jnp.float32
BF16 = jnp.bfloat16

D_MODEL = 1024
N_HEADS = 8
HEAD_DIM = 64
D_ATT = N_HEADS * HEAD_DIM
D_LRU = 512
LRU_BLOCKS = 8
D_IN = 4 * D_ATT + 2 * D_LRU
CONV_WIDTH = 4
LRU_C = 8.0
DILATIONS = (1, 4, 16)
N_DIL_KEYS = 128
BAND = 128
ALPHA = 2.0 ** 0.25
LN_EPS = 1e-5
NEG_INF = -1e30
LOG2_E = 1.4426950408889634

C_Q, C_K, C_V, C_GA, C_XL, C_GL = (0, D_ATT, 2 * D_ATT, 3 * D_ATT, 4 * D_ATT,
                                   4 * D_ATT + D_LRU)

VMEM_LIMIT = 60 * 1024 * 1024
PROJ_ROWS = 512
FINISH_ROWS = 1024
FINISH_BUFFERS = 3
FINISH_SUB_ROWS = 256
FINISH_MIN_STEPS = 4
HEAD_PAIR = 2 * HEAD_DIM
SUBLANES = 8
CACHE_SLOTS = 3
PROJ_PHASES = 2
ATTN_SAMPLE_ROWS = 2
ATTN_PHASES = 2


def _dot(a, b):
    return jnp.dot(a, b, preferred_element_type=F32)


def _dot_nt(a, b):
    return lax.dot_general(a, b, (((1,), (1,)), ((), ())),
                           preferred_element_type=F32)


def _sigmoid(x):
    return 0.5 * jnp.tanh(0.5 * x) + 0.5


def _silu(x):
    return x * _sigmoid(x)


def _softplus(x):
    return jnp.maximum(x, 0.0) + jnp.log1p(jnp.exp(-jnp.abs(x)))


def _lru_gates(u, wg_ref, vec_ref):
    g = _dot(u.astype(BF16), wg_ref[...])
    tr = jnp.tanh(g[:, :D_LRU] + 0.5 * vec_ref[5:6, :])
    ig = 0.5 * jnp.tanh(g[:, D_LRU:] + 0.5 * vec_ref[6:7, :]) + 0.5
    half_c = (-0.5 * LRU_C) * _softplus(-vec_ref[7:8, :])
    log_a = half_c * tr + half_c
    a = jnp.exp(log_a)
    x = jnp.tanh(-log_a) * (1.0 + a * a)
    bx = jnp.where(x > 0.0, x * lax.rsqrt(x), 0.0) * ig * u
    return a, bx


def _grid_step():
    return ((pl.program_id(0) * pl.num_programs(1) + pl.program_id(1)) * pl.num_programs(2)
            + pl.program_id(2))


def _cache_copies(row, cache_hbm, ring, sems):
    slot = row % CACHE_SLOTS
    return [pltpu.make_async_copy(src.at[row], dst.at[slot], sems.at[which, slot])
            for which, (src, dst) in enumerate(zip(cache_hbm, ring))]


def _sample_attn_step(first, n_rows_hosted, per_step, which, q_ref, kn_ref, vn_ref, sga_ref,
                      cnt_ref, cache_hbm, ring, sems, o_ref):
    t_steps = q_ref.shape[0]
    local = _grid_step() * per_step + which
    lookahead = CACHE_SLOTS - 1
    if which == 0:
        @pl.when(local == 0)
        def _():
            for early in range(lookahead):
                for copy in _cache_copies(first + early, cache_hbm, ring, sems):
                    copy.start()
    for copy in _cache_copies(first + local, cache_hbm, ring, sems):
        copy.wait()

    @pl.when(local + lookahead < n_rows_hosted)
    def _():
        for copy in _cache_copies(first + local + lookahead, cache_hbm, ring, sems):
            copy.start()

    kt_ref, vt_ref = (buf.at[(first + local) % CACHE_SLOTS] for buf in ring)
    mine = pl.ds(local % SUBLANES, 1)
    n_rows = t_steps * N_HEADS
    sub = lax.broadcasted_iota(jnp.int32, (N_HEADS, D_ATT), 0)
    lane_head = lax.broadcasted_iota(jnp.int32, (N_HEADS, D_ATT), 1) // HEAD_DIM
    own = sub == lane_head
    qbd = jnp.concatenate(
        [jnp.where(own, jnp.broadcast_to(q_ref[t, mine, :], (N_HEADS, D_ATT)), 0.0)
         for t in range(t_steps)], axis=0)

    cnt = cnt_ref[...]
    s = jnp.where(cnt > 0.0, _dot(qbd.astype(BF16), kt_ref[...].astype(BF16)), NEG_INF)
    m = jnp.max(s, axis=-1, keepdims=True)

    row_t = lax.broadcasted_iota(jnp.int32, (n_rows, 1), 0) // N_HEADS
    s_new, c_new = [], []
    for t2 in range(t_steps):
        c = jnp.where(row_t == t2, float(len(DILATIONS)),
                      jnp.where(row_t > t2, 1.0, 0.0))
        sn = jnp.sum(qbd * kn_ref[t2, mine, :], axis=-1, keepdims=True)
        sn = jnp.where(c > 0.0, sn, NEG_INF)
        m = jnp.maximum(m, sn)
        s_new.append(sn)
        c_new.append(c)

    p = jnp.exp(s - m) * cnt
    den = jnp.sum(p, axis=-1, keepdims=True)
    acc = _dot_nt(p.astype(BF16), vt_ref[...].astype(BF16))
    for t2 in range(t_steps):
        pn = jnp.exp(s_new[t2] - m) * c_new[t2]
        den = den + pn
        acc = acc + pn * vn_ref[t2, mine, :]
    y = acc / den
    for t in range(t_steps):
        yt = jnp.sum(jnp.where(own, y[t * N_HEADS:(t + 1) * N_HEADS, :], 0.0),
                     axis=0, keepdims=True)
        o_ref[t, mine, :] = yt * sga_ref[t, mine, :]


def _pattern_counts(t_steps, w_buf):
    delta = w_buf + np.arange(t_steps)[:, None] - np.arange(w_buf)[None, :]
    cnt = np.zeros((t_steps, w_buf), np.float32)
    for d in DILATIONS:
        cnt += (delta % d == 0) & (delta // d <= N_DIL_KEYS)
    return np.repeat(cnt, N_HEADS, axis=0)


def _sample_attn_operands(sample, grid, per_step):
    q_s, kn_s, vn_s, sga_s, kt, vt, first = sample
    t_steps = q_s.shape[0]
    w_buf = kt.shape[-1]
    n_local = grid[0] * grid[1] * grid[2] * per_step
    assert first % SUBLANES == 0 and SUBLANES % per_step == 0 and n_local >= CACHE_SLOTS
    cnt = jnp.asarray(_pattern_counts(t_steps, w_buf))
    step = lambda i, j, c: ((i * grid[1] + j) * grid[2] + c) * per_step
    tok_block = (t_steps, SUBLANES, D_ATT)
    tok_in = pl.BlockSpec(tok_block, lambda *g: (0, (first + step(*g)) // SUBLANES, 0))
    in_hbm = pl.BlockSpec(memory_space=pl.ANY)
    cnt_spec = pl.BlockSpec(cnt.shape, lambda *g: (0, 0), pipeline_mode=pl.Buffered(1))
    operands = (q_s, kn_s, vn_s, sga_s, cnt, kt, vt)
    in_specs = [tok_in, tok_in, tok_in, tok_in, cnt_spec, in_hbm, in_hbm]
    out_spec = pl.BlockSpec(tok_block, lambda *g: (0, step(*g) // SUBLANES, 0))
    out_shape = jax.ShapeDtypeStruct((t_steps, n_local, D_ATT), F32)
    scratch = [pltpu.VMEM((CACHE_SLOTS, D_ATT, w_buf), F32),
               pltpu.VMEM((CACHE_SLOTS, D_ATT, w_buf), F32),
               pltpu.SemaphoreType.DMA((2, CACHE_SLOTS))]
    return operands, in_specs, out_spec, out_shape, scratch


def _proj_lru_kernel(first_kept, first_sample, n_hosted, x_ref, w_ref, wg_ref, vec_ref,
                     sq_ref, skn_ref, svn_ref, ssga_ref, cnt_ref, kt_hbm, vt_hbm,
                     q_ref, k_ref, v_ref, hl_ref, sga_ref, conv_ref, hlast_ref,
                     kt_keep_ref, vt_keep_ref, sya_ref,
                     xb_ref, xc_ref, a_ref, b_ref, carry_ref, kt_ring, vt_ring, cache_sems):
    j = pl.program_id(1)
    phase = pl.program_id(2)
    rows = x_ref.shape[0]
    kept = j >= first_kept
    sample_step = functools.partial(_sample_attn_step, first_sample, n_hosted, 1, 0, sq_ref, skn_ref,
                                    svn_ref, ssga_ref, cnt_ref, (kt_hbm, vt_hbm),
                                    (kt_ring, vt_ring), cache_sems, sya_ref)

    @pl.when(phase == 0)
    def _():
        @pl.when(j == 0)
        def _():
            xc_ref[...] = jnp.zeros((SUBLANES, D_LRU), F32)
            carry_ref[...] = jnp.zeros((1, D_LRU), F32)

        sample_step()
        xb_ref[...] = x_ref[...].astype(BF16)
        xl = _dot(xb_ref[...], w_ref[:, C_XL:C_GL])
        tail = xc_ref[...]
        xc_ref[...] = xl[rows - SUBLANES:, :]
        row8 = lax.broadcasted_iota(jnp.int32, (SUBLANES, D_LRU), 0)
        u = vec_ref[4:5, :] + xl * vec_ref[3:4, :]
        for s in range(1, CONV_WIDTH):
            rolled = pltpu.roll(xl, s, axis=0)
            head = jnp.where(row8 < s, pltpu.roll(tail, s, axis=0), rolled[0:SUBLANES, :])
            u = u + jnp.concatenate([head, rolled[SUBLANES:, :]], axis=0) * vec_ref[3 - s:4 - s, :]
        a, bx = _lru_gates(u, wg_ref, vec_ref)
        a_ref[...] = a
        b_ref[...] = bx
        q_ref[...] = _dot(xb_ref[...], w_ref[:, C_Q:C_K]) * LOG2_E
        k_new = _dot(xb_ref[...], w_ref[:, C_K:C_V])
        k_ref[...] = k_new
        v_new = _dot(xb_ref[...], w_ref[:, C_V:C_GA])
        v_ref[...] = v_new

        @pl.when(kept)
        def _():
            kt_keep_ref[...] = k_new.T
            vt_keep_ref[...] = v_new.T

    @pl.when(phase == 1)
    def _():
        sample_step()
        sga_ref[...] = _silu(_dot(xb_ref[...], w_ref[:, C_GA:C_XL])).astype(BF16)
        sgl = _silu(_dot(xb_ref[...], w_ref[:, C_GL:D_IN]))

        row = lax.broadcasted_iota(jnp.int32, (SUBLANES, D_LRU), 0)

        def group(g, carry):
            r0 = pl.multiple_of(g * SUBLANES, SUBLANES)
            av = a_ref[pl.ds(r0, SUBLANES), :]
            bv = b_ref[pl.ds(r0, SUBLANES), :]
            for s in (1, 2, 4):
                keep = row >= s
                bv = jnp.where(keep, av * pltpu.roll(bv, s, axis=0) + bv, bv)
                av = jnp.where(keep, av * pltpu.roll(av, s, axis=0), av)
            h8 = av * carry + bv
            b_ref[pl.ds(r0, SUBLANES), :] = h8
            return h8[SUBLANES - 1:, :]

        carry = lax.fori_loop(0, rows // SUBLANES, group, carry_ref[...], unroll=True)
        carry_ref[...] = carry
        hl_ref[...] = (b_ref[...] * sgl).astype(BF16)

        @pl.when(j == pl.num_programs(1) - 1)
        def _():
            conv_ref[...] = xc_ref[SUBLANES - (CONV_WIDTH - 1):, :]
            hlast_ref[...] = carry


def _proj_lru(x, w_bf, w_gate, vecs, sample, n_keep):
    b, s, _ = x.shape
    nt = s // PROJ_ROWS
    assert n_keep % PROJ_ROWS == 0
    first_kept = nt - n_keep // PROJ_ROWS
    keep_spec = pl.BlockSpec((None, D_ATT, PROJ_ROWS),
                             lambda i, j, c: (i, 0, jnp.maximum(j - first_kept, 0)))
    keep_shape = jax.ShapeDtypeStruct((b, D_ATT, n_keep), F32)
    grid = (b, nt, PROJ_PHASES)
    s_ops, s_in, s_out, s_shape, s_scratch = _sample_attn_operands(sample, grid, 1)
    row_spec = lambda width: pl.BlockSpec((None, PROJ_ROWS, width), lambda i, j, c: (i, j, 0))
    const = lambda shape: pl.BlockSpec(shape, lambda i, j, c: (0,) * len(shape),
                                       pipeline_mode=pl.Buffered(1))
    per_batch = lambda n: pl.BlockSpec((None, n, D_LRU), lambda i, j, c: (i, 0, 0))
    tile = lambda dt: jax.ShapeDtypeStruct((b, s, D_ATT), dt)
    return pl.pallas_call(
        functools.partial(_proj_lru_kernel, first_kept, sample[-1], b * nt * PROJ_PHASES),
        grid=grid,
        in_specs=[row_spec(D_MODEL), const((D_MODEL, D_IN)),
                  const((D_LRU, 2 * D_LRU)), const(vecs.shape)] + s_in,
        out_specs=[row_spec(D_ATT), row_spec(D_ATT), row_spec(D_ATT),
                   row_spec(D_LRU), row_spec(D_ATT),
                   per_batch(CONV_WIDTH - 1), per_batch(1), keep_spec, keep_spec, s_out],
        out_shape=[tile(F32), tile(F32), tile(F32), tile(BF16), tile(BF16),
                   jax.ShapeDtypeStruct((b, CONV_WIDTH - 1, D_LRU), F32),
                   jax.ShapeDtypeStruct((b, 1, D_LRU), F32), keep_shape, keep_shape, s_shape],
        scratch_shapes=[pltpu.VMEM((PROJ_ROWS, D_MODEL), BF16),
                        pltpu.VMEM((SUBLANES, D_LRU), F32),
                        pltpu.VMEM((PROJ_ROWS, D_LRU), F32),
                        pltpu.VMEM((PROJ_ROWS, D_LRU), F32),
                        pltpu.VMEM((1, D_LRU), F32)] + s_scratch,
        compiler_params=pltpu.CompilerParams(
            dimension_semantics=("arbitrary", "arbitrary", "arbitrary"),
            vmem_limit_bytes=VMEM_LIMIT),
        name="prompt_proj_lru",
    )(x, w_bf, w_gate, vecs, *s_ops)


def _prompt_attn_kernel(first_sample, n_hosted, q_ref, k_ref, v_ref, sga_ref,
                        sq_ref, skn_ref, svn_ref, ssga_ref, cnt_ref, kt_hbm, vt_hbm,
                        o_ref, sya_ref, bias_ref, *scratch):
    seq = q_ref.shape[0]
    phase = pl.program_id(2)
    n_groups = seq // BAND
    head_a = lax.broadcasted_iota(jnp.int32, (BAND, HEAD_PAIR), 1) < HEAD_DIM
    one = jnp.ones((), BF16)

    stats = scratch[:6]

    def sample_steps():
        for which in range(ATTN_SAMPLE_ROWS):
            _sample_attn_step(first_sample, n_hosted, ATTN_SAMPLE_ROWS, which, sq_ref, skn_ref,
                              svn_ref, ssga_ref, cnt_ref, (kt_hbm, vt_hbm), scratch[6:8],
                              scratch[8], sya_ref)

    def key_value_block(rows):
        v_bf = v_ref[rows, :].astype(BF16)
        v_ext = jnp.concatenate([jnp.where(head_a, v_bf, one),
                                 jnp.where(head_a, one, v_bf)], axis=1)
        return k_ref[rows, :].astype(BF16), v_ext

    def group(rows, blk, carry):
        k_prev, v_prev = carry
        k_cur, v_cur = key_value_block(rows)
        k2 = jnp.concatenate([k_prev, k_cur], axis=0)
        v4 = jnp.concatenate([v_prev, v_cur], axis=0)
        qf = q_ref[rows, :]
        q2 = jnp.concatenate([jnp.where(head_a, qf, 0.0),
                              jnp.where(head_a, 0.0, qf)], axis=0).astype(BF16)
        s = _dot_nt(q2, k2).reshape(2, BAND, 2 * BAND)
        s = s + bias_ref[jnp.minimum(blk, 1)][None]
        m = jnp.max(s, axis=-1, keepdims=True)
        p = jnp.exp2(s - m).reshape(2 * BAND, 2 * BAND).astype(BF16)
        o = _dot(p, v4)
        acc = jnp.where(head_a, o[:BAND, :HEAD_PAIR], o[BAND:, HEAD_PAIR:])
        den = jnp.where(head_a, o[:BAND, HEAD_PAIR:], o[BAND:, :HEAD_PAIR])
        return acc, den, jnp.where(head_a, m[0], m[1]), (k_cur, v_cur)

    zero_block = (jnp.zeros((BAND, HEAD_PAIR), BF16), jnp.zeros((BAND, 2 * HEAD_PAIR), BF16))

    def strided_pattern(d, first, count, acc_ref, l_ref, m_ref):
        span = BAND * d
        n_blk = seq // span

        def body(i, carry):
            g = first + i
            blk = g % n_blk
            rows = pl.ds(blk * span + g // n_blk, BAND, stride=d)
            acc, den, m, carry = group(rows, blk, carry)
            acc_ref[rows, :] = acc
            l_ref[rows, :] = den
            m_ref[rows, :] = m
            return carry

        lax.fori_loop(0, count, body, zero_block, unroll=True)

    half = n_groups // 2

    @pl.when(phase == 0)
    def _():
        row = lax.broadcasted_iota(jnp.int32, (BAND, 2 * BAND), 0)
        col = lax.broadcasted_iota(jnp.int32, (BAND, 2 * BAND), 1)
        band = (col >= row) & (col <= row + N_DIL_KEYS)
        bias_ref[0] = jnp.where(band & (col >= BAND), 0.0, NEG_INF)
        bias_ref[1] = jnp.where(band, 0.0, NEG_INF)
        sample_steps()
        strided_pattern(DILATIONS[1], 0, n_groups, *stats[0:3])
        strided_pattern(DILATIONS[2], 0, half, *stats[3:6])

    @pl.when(phase == 1)
    def _():
        sample_steps()
        strided_pattern(DILATIONS[2], half, half, *stats[3:6])

        def body(g, carry):
            rows = pl.ds(pl.multiple_of(g * BAND, BAND), BAND)
            acc, den, m, carry = group(rows, g, carry)
            parts = [(acc, den, m)] + [(stats[3 * t][rows, :], stats[3 * t + 1][rows, :],
                                        stats[3 * t + 2][rows, :]) for t in range(2)]
            m_max = jnp.maximum(jnp.maximum(parts[0][2], parts[1][2]), parts[2][2])
            e = [jnp.exp2(pt[2] - m_max) for pt in parts]
            num = sum(ei * pt[0] for ei, pt in zip(e, parts))
            tot = sum(ei * pt[1] for ei, pt in zip(e, parts))
            o_ref[rows, :] = (num / tot * sga_ref[rows, :].astype(F32)).astype(BF16)
            return carry

        lax.fori_loop(0, n_groups, body, zero_block, unroll=True)


def _prompt_attn(q, k, v, sga, sample):
    b, s, _ = q.shape
    grid = (b, D_ATT // HEAD_PAIR, ATTN_PHASES)
    s_ops, s_in, s_out, s_shape, s_scratch = _sample_attn_operands(sample, grid, ATTN_SAMPLE_ROWS)
    spec = pl.BlockSpec((None, s, HEAD_PAIR), lambda i, j, c: (i, 0, j))
    return pl.pallas_call(
        functools.partial(_prompt_attn_kernel, sample[-1],
                          grid[0] * grid[1] * grid[2] * ATTN_SAMPLE_ROWS),
        grid=grid,
        in_specs=[spec, spec, spec, spec] + s_in,
        out_specs=[spec, s_out],
        out_shape=[jax.ShapeDtypeStruct((b, s, D_ATT), BF16), s_shape],
        scratch_shapes=[pltpu.VMEM((2, BAND, 2 * BAND), F32)]
                       + [pltpu.VMEM((s, HEAD_PAIR), F32) for _ in range(6)] + s_scratch,
        compiler_params=pltpu.CompilerParams(
            dimension_semantics=("arbitrary", "arbitrary", "arbitrary"),
            vmem_limit_bytes=VMEM_LIMIT),
        name="prompt_attn",
    )(q, k, v, sga, *s_ops)


def _finish_kernel(rows, ya_hbm, hl_hbm, x_hbm, wo_ref, ln_ref, o_hbm):
    steps = x_hbm.shape[0] // rows
    buffers = min(FINISH_BUFFERS, steps)

    def tile(ya_ref, hl_ref, x_ref, o_ref):
        sub_rows = min(rows, FINISH_SUB_ROWS)
        for r in range(0, rows, sub_rows):
            part = pl.ds(r, sub_rows)
            sub = (_dot(ya_ref[part, :], wo_ref[0:D_ATT, :])
                   + _dot(hl_ref[part, :], wo_ref[D_ATT:D_ATT + D_LRU, :]))
            z = ALPHA * x_ref[part, :] + sub
            mu = jnp.mean(z, axis=-1, keepdims=True)
            zc = z - mu
            var = jnp.mean(zc * zc, axis=-1, keepdims=True)
            o_ref[part, :] = (zc * lax.rsqrt(var + LN_EPS) * ln_ref[0:1, :]
                              + ln_ref[1:2, :])

    def row_in(width):
        if buffers > 2:
            return pl.BlockSpec((rows, width), lambda i: (i, 0),
                                pipeline_mode=pl.Buffered(buffers))
        return pl.BlockSpec((rows, width), lambda i: (i, 0))

    pltpu.emit_pipeline(
        tile, grid=(steps,),
        in_specs=[row_in(D_ATT), row_in(D_LRU), row_in(D_MODEL)],
        out_specs=[pl.BlockSpec((rows, D_MODEL), lambda i: (i, 0))],
    )(ya_hbm, hl_hbm, x_hbm, o_hbm)


def _finish(ya, hl, x, w_o, ln, name):
    n = x.shape[0]
    rows = min(n // FINISH_MIN_STEPS, FINISH_ROWS)
    hbm = pl.BlockSpec(memory_space=pl.ANY)
    vmem = pl.BlockSpec(memory_space=pltpu.VMEM)
    return pl.pallas_call(
        functools.partial(_finish_kernel, rows),
        in_specs=[hbm, hbm, hbm, vmem, vmem],
        out_specs=hbm,
        out_shape=jax.ShapeDtypeStruct((n, D_MODEL), F32),
        compiler_params=pltpu.CompilerParams(vmem_limit_bytes=VMEM_LIMIT),
        name=name,
    )(ya, hl, x, w_o, ln)


def _sample_proj_kernel(x_ref, sc_ref, sh_ref, w_ref, wg_ref, vec_ref,
                        q_ref, k_ref, v_ref, hl_ref, sga_ref, conv_ref, hlast_ref, wbf_ref):
    t_steps, nb, _ = x_ref.shape
    wbf_ref[:, C_Q:C_K] = (w_ref[:, C_Q:C_K] * HEAD_DIM ** -0.5).astype(BF16)
    wbf_ref[:, C_K:D_IN] = w_ref[:, C_K:D_IN].astype(BF16)
    xb = x_ref[...].reshape(t_steps * nb, D_MODEL).astype(BF16)
    split = lambda y: y.reshape(t_steps, nb, y.shape[-1])
    q_ref[...] = split(_dot(xb, wbf_ref[:, C_Q:C_K]))
    k_ref[...] = split(_dot(xb, wbf_ref[:, C_K:C_V]))
    v_ref[...] = split(_dot(xb, wbf_ref[:, C_V:C_GA]))
    sga_ref[...] = split(_silu(_dot(xb, wbf_ref[:, C_GA:C_XL])))

    xl = _dot(xb, wbf_ref[:, C_XL:C_GL])
    xc = [sc_ref[i] for i in range(CONV_WIDTH - 1)]
    xc += [xl[t * nb:(t + 1) * nb, :] for t in range(t_steps)]
    u = jnp.concatenate(
        [vec_ref[4:5, :] + sum(xc[t + w] * vec_ref[w:w + 1, :] for w in range(CONV_WIDTH))
         for t in range(t_steps)], axis=0)
    for i in range(CONV_WIDTH - 1):
        conv_ref[i] = xc[t_steps + i]

    a, bx = _lru_gates(u, wg_ref, vec_ref)
    sgl = _silu(_dot(xb, wbf_ref[:, C_GL:D_IN]))
    h = sh_ref[...]
    for t in range(t_steps):
        rows = slice(t * nb, (t + 1) * nb)
        h = a[rows, :] * h + bx[rows, :]
        hl_ref[t] = (h * sgl[rows, :]).astype(BF16)
    hlast_ref[...] = h


def _sample_proj(xs_t, sc_t, sh, w_f32, w_gate, vecs):
    t_steps, nb, _ = xs_t.shape
    full = lambda shape: pl.BlockSpec(shape, lambda i: (0,) * len(shape),
                                      pipeline_mode=pl.Buffered(1))
    tok = lambda dt: jax.ShapeDtypeStruct((t_steps, nb, D_ATT), dt)
    return pl.pallas_call(
        _sample_proj_kernel,
        grid=(1,),
        in_specs=[full(xs_t.shape), full(sc_t.shape), full(sh.shape),
                  full(w_f32.shape), full(w_gate.shape), full(vecs.shape)],
        out_specs=[full((t_steps, nb, D_ATT))] * 5
                  + [full(sc_t.shape), full(sh.shape), full(w_f32.shape)],
        out_shape=[tok(F32), tok(F32), tok(F32), tok(BF16), tok(F32),
                   jax.ShapeDtypeStruct(sc_t.shape, F32),
                   jax.ShapeDtypeStruct(sh.shape, F32),
                   jax.ShapeDtypeStruct(w_f32.shape, BF16)],
        compiler_params=pltpu.CompilerParams(
            dimension_semantics=("arbitrary",), vmem_limit_bytes=VMEM_LIMIT),
        name="sample_proj_lru",
    )(xs_t, sc_t, sh, w_f32, w_gate, vecs)


def kernel(x_prompt, x_sample, cache_k, cache_v, state_conv, state_h, w_in, conv_w, conv_b, w_ra, b_ra, w_ri, b_ri, lru_lambda, w_out, ln_g, ln_b):
    assert w_in.shape[0] == 1, "single layer"
    b, s, _ = x_prompt.shape
    nb, t_steps, _ = x_sample.shape
    w_buf = cache_k.shape[2]

    eye = jnp.eye(LRU_BLOCKS, dtype=F32)
    block_diag = lambda wb: (eye[:, None, :, None] * wb[:, :, None, :]).reshape(D_LRU, D_LRU)
    w_gate = (0.5 * jnp.concatenate([block_diag(w_ra[0]), block_diag(w_ri[0])], axis=1)).astype(BF16)
    vecs = jnp.concatenate([conv_w[0], conv_b, b_ra, b_ri, lru_lambda], axis=0)
    w_o = w_out[0].astype(BF16)
    ln = jnp.concatenate([ln_g, ln_b], axis=0)
    heads = lambda a: a.reshape(a.shape[:-1] + (N_HEADS, HEAD_DIM))

    xs_t = jnp.transpose(x_sample, (1, 0, 2))
    sc_t = jnp.transpose(state_conv[0], (1, 0, 2))
    q_s, k_s, v_s, hl_s, sga_s, conv_s, h_s, w_bf = _sample_proj(
        xs_t, sc_t, state_h[0], w_in[0], w_gate, vecs)
    kt = jnp.transpose(cache_k[0], (0, 2, 3, 1)).reshape(nb, D_ATT, w_buf)
    vt = jnp.transpose(cache_v[0], (0, 2, 3, 1)).reshape(nb, D_ATT, w_buf)
    sample = (q_s, k_s, v_s, sga_s, kt, vt)

    n_proj = b * (s // PROJ_ROWS) * PROJ_PHASES
    n_attn = b * (D_ATT // HEAD_PAIR) * ATTN_PHASES * ATTN_SAMPLE_ROWS
    assert n_proj + n_attn == nb
    n_keep = min(w_buf, s)
    q, k, v, hl, sga, conv_p, h_p, kt_p, vt_p, ya_s0 = _proj_lru(
        x_prompt, w_bf, w_gate, vecs, sample + (0,), n_keep)
    ya, ya_s1 = _prompt_attn(q, k, v, sga, sample + (n_proj,))
    y_p = _finish(ya.reshape(b * s, D_ATT), hl.reshape(b * s, D_LRU),
                  x_prompt.reshape(b * s, D_MODEL), w_o, ln, "prompt_finish")
    window = lambda a: jnp.transpose(a.reshape(b, N_HEADS, HEAD_DIM, n_keep), (0, 3, 1, 2))[None]
    k_p, v_p = window(kt_p), window(vt_p)
    ya_s = jnp.concatenate([ya_s0, ya_s1], axis=1)
    y_s = _finish(ya_s.reshape(t_steps * nb, D_ATT).astype(BF16),
                  hl_s.reshape(t_steps * nb, D_LRU),
                  xs_t.reshape(t_steps * nb, D_MODEL), w_o, ln, "sample_finish")
    to_batch_major = lambda a: jnp.transpose(a, (1, 0, 2))

    return (y_p.reshape(b, s, D_MODEL),
            to_batch_major(y_s.reshape(t_steps, nb, D_MODEL)),
            k_p, v_p,
            conv_p[None], h_p.reshape(1, b, D_LRU),
            heads(to_batch_major(k_s))[None], heads(to_batch_major(v_s))[None],
            to_batch_major(conv_s)[None], h_s[None])
```
